```python
import jax
import jax.numpy as jnp
from jax import lax
import numpy as np

D_MODEL = 2048
BATCH = 4
SEQ = 4096
DEPTH = 2

D_MIX = D_MODEL
N_MIXERS = 4
D_GROUP = D_MIX // N_MIXERS
N_IN_SLICES = 11
D_IN = N_IN_SLICES * D_GROUP
ATT_HEADS = 4
ATT_HEAD_DIM = D_GROUP // ATT_HEADS
MOBA_BLOCK = 256
MOBA_TOPK = 3
MOBA_Q_CHUNK = 64
MASK_VALUE = -1e30
RG_BLOCKS = 4
RG_BLOCK_DIM = D_GROUP // RG_BLOCKS
RG_CONV = 4
RG_C = 8.0
RG_A_MIN = 0.9
RG_A_MAX = 0.999
CV_WIDTH = 31
CV_GROUPS = 4
HG_HEADS = 4
HG_HEAD_DIM = D_GROUP // HG_HEADS
HG_CHUNK = 64
D_FF = 5632
LN_EPS = 1e-5
ALPHA = (2 * DEPTH) ** 0.25
BETA = (8 * DEPTH) ** -0.25

kernel_name = 'hybrid_moba_rglru_conformer_hgrn2_block'


def _standardize(x):
    x32 = x.astype(jnp.float32)
    mu = jnp.mean(x32, axis=-1, keepdims=True)
    var = jnp.mean(jnp.square(x32 - mu), axis=-1, keepdims=True)
    return (x32 - mu) * lax.rsqrt(var + LN_EPS)


def layer_norm(x, g, b):
    return (_standardize(x) * g + b).astype(x.dtype)


def swiglu(x, w_gate, w_up, w_down):
    return (jax.nn.silu(x @ w_gate) * (x @ w_up)) @ w_down


def causal_depthwise_conv(x, w, b):
    width, ch = w.shape
    y = lax.conv_general_dilated(
        x, w[:, None, :].astype(x.dtype), window_strides=(1,), padding=[(width - 1, 0)],
        dimension_numbers=('NWC', 'WIO', 'NWC'), feature_group_count=ch)
    return y + b


def split_heads(t, n_heads):
    bsz, seq, width = t.shape
    return t.reshape(bsz, seq, n_heads, width // n_heads)


def moba_attention(q, k, v):
    bsz, seq, nh, dh = q.shape
    nb = -(-seq // MOBA_BLOCK)
    s_pad = nb * MOBA_BLOCK
    pad = ((0, 0), (0, s_pad - seq), (0, 0), (0, 0))
    q, k, v = [jnp.pad(t, pad).transpose(0, 2, 1, 3) for t in (q, k, v)]
    kb = k.reshape(bsz, nh, nb, MOBA_BLOCK, dh)
    vb = v.reshape(bsz, nh, nb, MOBA_BLOCK, dh)
    k_mean = jnp.mean(kb, axis=3)
    topk = min(MOBA_TOPK, nb - 1)
    n_chunks = s_pad // MOBA_Q_CHUNK
    q_chunks = q.reshape(bsz, nh, n_chunks, MOBA_Q_CHUNK, dh).transpose(2, 0, 1, 3, 4)
    scale = dh ** -0.5
    b_idx = jnp.arange(bsz)[:, None, None, None]
    h_idx = jnp.arange(nh)[None, :, None, None]

    def one_chunk(args):
        c, q_c = args
        q_pos = c * MOBA_Q_CHUNK + jnp.arange(MOBA_Q_CHUNK)
        blk = (c * MOBA_Q_CHUNK) // MOBA_BLOCK
        k_own = lax.dynamic_index_in_dim(kb, blk, axis=2, keepdims=False)
        v_own = lax.dynamic_index_in_dim(vb, blk, axis=2, keepdims=False)
        k_pos = blk * MOBA_BLOCK + jnp.arange(MOBA_BLOCK)
        s_own = jnp.einsum('bhqd,bhkd->bhqk', q_c, k_own) * scale
        s_own = jnp.where(k_pos[None, :] <= q_pos[:, None], s_own, MASK_VALUE)
        if topk == 0:
            p_own = jax.nn.softmax(s_own.astype(jnp.float32), axis=-1)
            return jnp.einsum('bhqk,bhkd->bhqd', p_own, v_own)
        gate = jnp.einsum('bhqd,bhnd->bhqn', q_c, k_mean)
        gate = jnp.where(jnp.arange(nb) < blk, gate, MASK_VALUE)
        _, idx = lax.top_k(gate, topk)
        valid = idx < blk
        k_sel = kb[b_idx, h_idx, idx]
        v_sel = vb[b_idx, h_idx, idx]
        s_sel = jnp.einsum('bhqd,bhqnkd->bhqnk', q_c, k_sel) * scale
        s_sel = jnp.where(valid[..., None], s_sel, MASK_VALUE)
        s_all = jnp.concatenate(
            [s_own, s_sel.reshape(bsz, nh, MOBA_Q_CHUNK, topk * MOBA_BLOCK)], axis=-1)
        p = jax.nn.softmax(s_all.astype(jnp.float32), axis=-1)
        p_own = p[..., :MOBA_BLOCK]
        p_sel = p[..., MOBA_BLOCK:].reshape(bsz, nh, MOBA_Q_CHUNK, topk, MOBA_BLOCK)
        return (jnp.einsum('bhqk,bhkd->bhqd', p_own, v_own)
                + jnp.einsum('bhqnk,bhqnkd->bhqd', p_sel, v_sel))

    out = lax.map(one_chunk, (jnp.arange(n_chunks), q_chunks))
    out = out.transpose(1, 0, 3, 2, 4).reshape(bsz, s_pad, nh * dh)
    return out[:, :seq]


def _linear_recurrence_combine(left, right):
    a_l, b_l = left
    a_r, b_r = right
    return a_l * a_r, a_r * b_l + b_r


def rglru_mixer(gate_in, x_in, conv_w, conv_b, w_a, b_a, w_x, b_x, lam):
    bsz, seq, _ = x_in.shape
    xc = causal_depthwise_conv(x_in, conv_w, conv_b)
    xb = xc.reshape(bsz, seq, RG_BLOCKS, RG_BLOCK_DIM)
    r = jax.nn.sigmoid(jnp.einsum('bsgi,gio->bsgo', xb, w_a).reshape(bsz, seq, D_GROUP) + b_a)
    i = jax.nn.sigmoid(jnp.einsum('bsgi,gio->bsgo', xb, w_x).reshape(bsz, seq, D_GROUP) + b_x)
    log_a = -RG_C * r * jax.nn.softplus(-lam)
    a = jnp.exp(log_a)
    u = jnp.sqrt(jnp.maximum(-jnp.expm1(2.0 * log_a), 0.0)) * (i * xc)
    _, h = lax.associative_scan(_linear_recurrence_combine, (a, u), axis=1)
    return h * jax.nn.gelu(gate_in, approximate=True)


def conformer_conv_mixer(val, gate, conv_w, conv_b, norm_g, norm_b):
    bsz, seq, _ = val.shape
    u = causal_depthwise_conv(val * jax.nn.sigmoid(gate), conv_w, conv_b)
    u = _standardize(u.reshape(bsz, seq, CV_GROUPS, D_GROUP // CV_GROUPS)).reshape(bsz, seq, D_GROUP)
    return jax.nn.silu(u * norm_g + norm_b)


def hgrn2_mixer(q, f_logit, v, g, lb, norm_g):
    bsz, seq, _ = q.shape
    nc = seq // HG_CHUNK
    sig = jax.nn.sigmoid(f_logit)
    log_f = jnp.log(lb + (1.0 - lb) * sig)
    k = (1.0 - lb) * (1.0 - sig)

    def chunks(t):
        return t.reshape(bsz, nc, HG_CHUNK, HG_HEADS, HG_HEAD_DIM).transpose(1, 0, 3, 2, 4)

    causal = jnp.tril(jnp.ones((HG_CHUNK, HG_CHUNK), dtype=bool))[:, :, None]

    def step(state, inp):
        q_c, k_c, v_c, lf_c = inp
        b = jnp.cumsum(lf_c, axis=2)
        o_inter = jnp.einsum('bhtk,bhkv->bhtv', q_c * jnp.exp(b), state)
        diff = b[:, :, :, None, :] - b[:, :, None, :, :]
        decay = jnp.where(causal, jnp.exp(jnp.where(causal, diff, 0.0)), 0.0)
        att = jnp.einsum('bhtk,bhsk,bhtsk->bhts', q_c, k_c, decay)
        o = o_inter + jnp.einsum('bhts,bhsv->bhtv', att, v_c)
        b_last = b[:, :, -1:, :]
        state = (jnp.exp(b_last[:, :, 0, :, None]) * state
                 + jnp.einsum('bhsk,bhsv->bhkv', k_c * jnp.exp(b_last - b), v_c))
        return state, o

    s0 = jnp.zeros((bsz, HG_HEADS, HG_HEAD_DIM, HG_HEAD_DIM), jnp.float32)
    _, o = lax.scan(step, s0, (chunks(q), chunks(k), chunks(v), chunks(log_f)))
    o = o.transpose(1, 0, 3, 2, 4).reshape(bsz, seq, HG_HEADS, HG_HEAD_DIM)
    o = o * lax.rsqrt(jnp.mean(jnp.square(o), axis=-1, keepdims=True) + LN_EPS)
    o = o * norm_g.reshape(HG_HEADS, HG_HEAD_DIM)
    return o.reshape(bsz, seq, D_GROUP) * jax.nn.silu(g)


def setup_inputs(seed: int = 0) -> dict:
    key = jax.random.key(seed)
    ks = jax.random.split(key, 24)
    nrm = jax.random.normal
    f32 = jnp.float32
    x = nrm(ks[0], (BATCH, SEQ, D_MODEL), f32)
    ln_g = 1.0 + 0.02 * nrm(ks[1], (DEPTH, 3, D_MODEL), f32)
    ln_b = 0.02 * nrm(ks[2], (DEPTH, 3, D_MODEL), f32)
    ffn_w_gate = nrm(ks[3], (DEPTH, 2, D_MODEL, D_FF), f32) * D_MODEL ** -0.5
    ffn_w_up = nrm(ks[4], (DEPTH, 2, D_MODEL, D_FF), f32) * D_MODEL ** -0.5
    ffn_w_down = nrm(ks[5], (DEPTH, 2, D_FF, D_MODEL), f32) * (D_FF ** -0.5 * BETA)
    w_in = nrm(ks[6], (DEPTH, D_MODEL, D_IN), f32) * D_MODEL ** -0.5
    w_out = nrm(ks[7], (DEPTH, D_MIX, D_MODEL), f32) * (D_MIX ** -0.5 * BETA)
    rg_conv_w = nrm(ks[8], (DEPTH, RG_CONV, D_GROUP), f32) * RG_CONV ** -0.5
    rg_conv_b = 0.02 * nrm(ks[9], (DEPTH, D_GROUP), f32)
    rg_w_a = nrm(ks[10], (DEPTH, RG_BLOCKS, RG_BLOCK_DIM, RG_BLOCK_DIM), f32) * RG_BLOCK_DIM ** -0.5
    rg_b_a = 0.02 * nrm(ks[11], (DEPTH, D_GROUP), f32)
    rg_w_x = nrm(ks[12], (DEPTH, RG_BLOCKS, RG_BLOCK_DIM, RG_BLOCK_DIM), f32) * RG_BLOCK_DIM ** -0.5
    rg_b_x = 0.02 * nrm(ks[13], (DEPTH, D_GROUP), f32)
    a_c = jax.random.uniform(ks[14], (DEPTH, D_GROUP), f32, RG_A_MIN, RG_A_MAX)
    a0 = a_c ** (1.0 / RG_C)
    rg_lambda = jnp.log(a0) - jnp.log1p(-a0)
    cv_w = nrm(ks[15], (DEPTH, CV_WIDTH, D_GROUP), f32) * CV_WIDTH ** -0.5
    cv_b = 0.02 * nrm(ks[16], (DEPTH, D_GROUP), f32)
    cv_ln_g = 1.0 + 0.02 * nrm(ks[17], (DEPTH, D_GROUP), f32)
    cv_ln_b = 0.02 * nrm(ks[18], (DEPTH, D_GROUP), f32)
    hg_lower_bounds = 0.5 * nrm(ks[19], (DEPTH, D_GROUP), f32)
    hg_norm_g = 1.0 + 0.02 * nrm(ks[20], (DEPTH, D_GROUP), f32)
    return {'x': x, 'ln_g': ln_g, 'ln_b': ln_b, 'ffn_w_gate': ffn_w_gate, 'ffn_w_up': ffn_w_up,
            'ffn_w_down': ffn_w_down, 'w_in': w_in, 'w_out': w_out, 'rg_conv_w': rg_conv_w,
            'rg_conv_b': rg_conv_b, 'rg_w_a': rg_w_a, 'rg_b_a': rg_b_a, 'rg_w_x': rg_w_x,
            'rg_b_x': rg_b_x, 'rg_lambda': rg_lambda, 'cv_w': cv_w, 'cv_b': cv_b,
            'cv_ln_g': cv_ln_g, 'cv_ln_b': cv_ln_b, 'hg_lower_bounds': hg_lower_bounds,
            'hg_norm_g': hg_norm_g}


def reference(x, ln_g, ln_b, ffn_w_gate, ffn_w_up, ffn_w_down, w_in, w_out, rg_conv_w,
              rg_conv_b, rg_w_a, rg_b_a, rg_w_x, rg_b_x, rg_lambda, cv_w, cv_b, cv_ln_g,
              cv_ln_b, hg_lower_bounds, hg_norm_g):
    f32 = jnp.float32
    sm = jax.nn.softmax(hg_lower_bounds.astype(f32), axis=0)
    lower_bounds = jnp.cumsum(sm, axis=0) - sm[0:1]
    for l in range(DEPTH):
        x = layer_norm(ALPHA * x + 0.5 * swiglu(x, ffn_w_gate[l, 0], ffn_w_up[l, 0], ffn_w_down[l, 0]),
                       ln_g[l, 0], ln_b[l, 0])
        (a_q, a_k, a_v, b_gate, b_x, c_val, c_gate,
         d_q, d_f, d_i, d_g) = jnp.split((x @ w_in[l]).astype(f32), N_IN_SLICES, axis=-1)
        y_a = moba_attention(split_heads(a_q, ATT_HEADS), split_heads(a_k, ATT_HEADS),
                             split_heads(a_v, ATT_HEADS))
        y_b = rglru_mixer(b_gate, b_x, rg_conv_w[l], rg_conv_b[l], rg_w_a[l], rg_b_a[l],
                          rg_w_x[l], rg_b_x[l], rg_lambda[l])
        y_c = conformer_conv_mixer(c_val, c_gate, cv_w[l], cv_b[l], cv_ln_g[l], cv_ln_b[l])
        y_d = hgrn2_mixer(d_q, d_f, d_i, d_g, lower_bounds[l], hg_norm_g[l])
        y = jnp.concatenate([y_a, y_b, y_c, y_d], axis=-1).astype(x.dtype) @ w_out[l]
        x = layer_norm(ALPHA * x + y, ln_g[l, 1], ln_b[l, 1])
        x = layer_norm(ALPHA * x + 0.5 * swiglu(x, ffn_w_gate[l, 1], ffn_w_up[l, 1], ffn_w_down[l, 1]),
                       ln_g[l, 2], ln_b[l, 2])
    return x
```

```python
import functools

import jax
import jax.numpy as jnp
from jax import lax
from jax.experimental import pallas as pl
from jax.experimental.pallas import tpu as pltpu

F32 = jnp.float32
BF16 = jnp.bfloat16

LANES = 128
SUBLANES = 8
VMEM_LIMIT_BYTES = 56 * 1024 * 1024

D_GROUP = 512
HEAD_DIM = 128
N_HEADS = D_GROUP // HEAD_DIM
MOBA_BLOCK = 256
MOBA_TOPK = 3
MASK_VALUE = -1e30
RG_C = 8.0
RG_CONV = 4
CV_WIDTH = 31
CV_HALO = 32
HG_CHUNK = 64
HG_SUB = 16
LN_EPS = 1e-5

NT_DIMS = (((1,), (1,)), ((), ()))
TN_DIMS = (((0,), (0,)), ((), ()))


def _params(*sem):
    return pltpu.CompilerParams(dimension_semantics=sem, vmem_limit_bytes=VMEM_LIMIT_BYTES)


def _standardize(z):
    mu = jnp.mean(z, axis=-1, keepdims=True)
    zc = z - mu
    var = jnp.mean(zc * zc, axis=-1, keepdims=True)
    return zc * lax.rsqrt(var + LN_EPS)


def _silu(z):
    return z * jax.nn.sigmoid(z)


def _shift_rows(ext, d, halo, rows):
    if d % SUBLANES == 0:
        return ext[halo - d:halo - d + rows]
    r = d % SUBLANES
    base = d - r
    rolled = pltpu.roll(ext, r, axis=0)
    return rolled[halo - base:halo - base + rows]


def _ffn_ln_kernel(x_ref, wg_ref, wu_ref, wd_ref, g_ref, b_ref, o_ref, xb_ref, *, alpha):
    f = pl.program_id(1)

    @pl.when(f == 0)
    def _():
        xb_ref[...] = x_ref[...].astype(BF16)

    xb = xb_ref[...]
    hg = jnp.dot(xb, wg_ref[...], preferred_element_type=F32)
    hu = jnp.dot(xb, wu_ref[...], preferred_element_type=F32)
    h = (_silu(hg) * hu).astype(BF16)
    part = jnp.dot(h, wd_ref[...], preferred_element_type=F32)

    @pl.when(f == 0)
    def _():
        o_ref[...] = part

    @pl.when(f > 0)
    def _():
        o_ref[...] += part

    @pl.when(f == pl.num_programs(1) - 1)
    def _():
        z = alpha * x_ref[...] + 0.5 * o_ref[...]
        o_ref[...] = _standardize(z) * g_ref[...] + b_ref[...]


def ffn_ln(x, wg, wu, wd, ln_g, ln_b, l, j, jn, alpha, tm, tf):
    m, d = x.shape
    dff = wg.shape[-1]
    tm = min(tm, m)
    tf = min(tf, dff)
    assert m % tm == 0 and dff % tf == 0
    grid = (m // tm, dff // tf)
    return pl.pallas_call(
        functools.partial(_ffn_ln_kernel, alpha=alpha),
        grid=grid,
        in_specs=[
            pl.BlockSpec((tm, d), lambda i, f: (i, 0)),
            pl.BlockSpec((None, None, d, tf), lambda i, f: (l, j, 0, f)),
            pl.BlockSpec((None, None, d, tf), lambda i, f: (l, j, 0, f)),
            pl.BlockSpec((None, None, tf, d), lambda i, f: (l, j, f, 0)),
            pl.BlockSpec((None, 1, d), lambda i, f: (jn, 0, 0)),
            pl.BlockSpec((None, 1, d), lambda i, f: (jn, 0, 0)),
        ],
        out_specs=pl.BlockSpec((tm, d), lambda i, f: (i, 0)),
        out_shape=jax.ShapeDtypeStruct((m, d), F32),
        scratch_shapes=[pltpu.VMEM((tm, d), BF16)],
        compiler_params=_params("parallel", "arbitrary"),
        name="ffn_ln",
    )(x, wg, wu, wd, ln_g, ln_b)


def _in_proj_kernel(x_ref, w_ref, o_ref, xb_ref):
    @pl.when(pl.program_id(1) == 0)
    def _():
        xb_ref[...] = x_ref[...].astype(BF16)

    o_ref[...] = jnp.dot(xb_ref[...], w_ref[...], preferred_element_type=F32)


def in_proj(x, w_in, l, tm, tn):
    m, d = x.shape
    n = w_in.shape[-1]
    tm = min(tm, m)
    assert m % tm == 0 and n % tn == 0
    return pl.pallas_call(
        _in_proj_kernel,
        grid=(m // tm, n // tn),
        in_specs=[
            pl.BlockSpec((tm, d), lambda i, c: (i, 0)),
            pl.BlockSpec((None, d, tn), lambda i, c: (l, 0, c)),
        ],
        out_specs=pl.BlockSpec((tm, tn), lambda i, c: (i, c)),
        out_shape=jax.ShapeDtypeStruct((m, n), F32),
        scratch_shapes=[pltpu.VMEM((tm, d), BF16)],
        compiler_params=_params("parallel", "arbitrary"),
        name="in_proj",
    )(x, w_in)


def _moba_kernel(q_ref, k_ref, v_ref, o_ref, kb_ref, vb_ref, km_ref, *, nb, scale):
    i = pl.program_id(2)
    blk = MOBA_BLOCK

    @pl.when(i == 0)
    def _():
        k = k_ref[...]
        kb_ref[...] = k.astype(BF16)
        vb_ref[...] = v_ref[...].astype(BF16)
        km_ref[...] = jnp.zeros(km_ref.shape, F32)
        km_ref[0:nb, :] = jnp.mean(k.reshape(nb, blk, HEAD_DIM), axis=1)

    qb = q_ref[...].astype(BF16)

    gate = lax.dot_general(qb, km_ref[...].astype(BF16), NT_DIMS, preferred_element_type=F32)
    lane = lax.broadcasted_iota(jnp.int32, (blk, LANES), 1)
    lane_f = lane.astype(F32)
    past = lane < i
    g = jnp.where(past, gate, MASK_VALUE)
    sel = jnp.zeros((blk, LANES), F32)
    for _ in range(MOBA_TOPK):
        top = jnp.max(g, axis=-1, keepdims=True)
        first = jnp.min(jnp.where(g == top, lane_f, float(LANES)), axis=-1, keepdims=True)
        pick = lane_f == first
        sel = jnp.where(pick, 1.0, sel)
        g = jnp.where(pick, -jnp.inf, g)
    sel = jnp.where(past, sel, 0.0)

    own = pl.multiple_of(i * blk, blk)
    s = lax.dot_general(qb, kb_ref[pl.ds(own, blk), :], NT_DIMS, preferred_element_type=F32) * scale
    row = lax.broadcasted_iota(jnp.int32, (blk, blk), 0)
    col = lax.broadcasted_iota(jnp.int32, (blk, blk), 1)
    s = jnp.where(col <= row, s, MASK_VALUE)
    m0 = jnp.max(s, axis=-1, keepdims=True)
    p = jnp.exp(s - m0)
    l0 = jnp.sum(p, axis=-1, keepdims=True)
    acc0 = jnp.dot(p.astype(BF16), vb_ref[pl.ds(own, blk), :], preferred_element_type=F32)

    def body(n, carry):
        m, l, acc = carry
        start = pl.multiple_of(n * blk, blk)
        s = lax.dot_general(qb, kb_ref[pl.ds(start, blk), :], NT_DIMS, preferred_element_type=F32) * scale
        chosen = jnp.sum(jnp.where(lane == n, sel, 0.0), axis=-1, keepdims=True)
        s = jnp.where(chosen > 0.5, s, MASK_VALUE)
        m_new = jnp.maximum(m, jnp.max(s, axis=-1, keepdims=True))
        a = jnp.exp(m - m_new)
        p = jnp.exp(s - m_new)
        l = a * l + jnp.sum(p, axis=-1, keepdims=True)
        acc = a * acc + jnp.dot(p.astype(BF16), vb_ref[pl.ds(start, blk), :], preferred_element_type=F32)
        return m_new, l, acc

    _, l, acc = lax.fori_loop(0, i, body, (m0, l0, acc0))
    o_ref[...] = acc / l


def moba(p3):
    bsz, seq, _ = p3.shape
    blk = MOBA_BLOCK
    assert seq % blk == 0
    nb = seq // blk
    assert nb <= LANES
    kernel = functools.partial(_moba_kernel, nb=nb, scale=HEAD_DIM ** -0.5)
    return pl.pallas_call(
        kernel,
        grid=(bsz, N_HEADS, nb),
        in_specs=[
            pl.BlockSpec((None, blk, HEAD_DIM), lambda b, h, i: (b, i, h)),
            pl.BlockSpec((None, seq, HEAD_DIM), lambda b, h, i: (b, 0, N_HEADS + h)),
            pl.BlockSpec((None, seq, HEAD_DIM), lambda b, h, i: (b, 0, 2 * N_HEADS + h)),
        ],
        out_specs=pl.BlockSpec((None, blk, HEAD_DIM), lambda b, h, i: (b, i, h)),
        out_shape=jax.ShapeDtypeStruct((bsz, seq, D_GROUP), F32),
        scratch_shapes=[
            pltpu.VMEM((seq, HEAD_DIM), BF16),
            pltpu.VMEM((seq, HEAD_DIM), BF16),
            pltpu.VMEM((LANES, HEAD_DIM), F32),
        ],
        compiler_params=_params("parallel", "parallel", "arbitrary"),
        name="moba",
    )(p3, p3, p3)


def _rglru_kernel(gate_ref, x_ref, cw_ref, cb_ref, wa_ref, ba_ref, wx_ref, bx_ref, lam_ref,
                  o_ref, halo_ref, h_ref):
    t = pl.program_id(1)
    rows = x_ref.shape[0]

    @pl.when(t == 0)
    def _():
        halo_ref[...] = jnp.zeros(halo_ref.shape, F32)
        h_ref[...] = jnp.zeros(h_ref.shape, F32)

    x = x_ref[...]
    ext = jnp.concatenate([halo_ref[...], x], axis=0)
    halo_ref[...] = x[rows - SUBLANES:]
    xc = jnp.zeros_like(x) + cb_ref[...]
    for j in range(RG_CONV):
        xc = xc + cw_ref[j:j + 1, :] * _shift_rows(ext, RG_CONV - 1 - j, SUBLANES, rows)

    xcb = xc.astype(BF16)
    r_parts, i_parts = [], []
    for g in range(D_GROUP // HEAD_DIM):
        blk = xcb[:, g * HEAD_DIM:(g + 1) * HEAD_DIM]
        r_parts.append(jnp.dot(blk, wa_ref[g].astype(BF16), preferred_element_type=F32))
        i_parts.append(jnp.dot(blk, wx_ref[g].astype(BF16), preferred_element_type=F32))
    r = jax.nn.sigmoid(jnp.concatenate(r_parts, axis=-1) + ba_ref[...])
    ig = jax.nn.sigmoid(jnp.concatenate(i_parts, axis=-1) + bx_ref[...])
    neg_lam = -lam_ref[...]
    softplus = jnp.maximum(neg_lam, 0.0) + jnp.log1p(jnp.exp(-jnp.abs(neg_lam)))
    log_a = -RG_C * r * softplus
    a = jnp.exp(log_a)
    u = jnp.sqrt(jnp.maximum(1.0 - jnp.exp(2.0 * log_a), 0.0)) * (ig * xc)

    row = lax.broadcasted_iota(jnp.int32, a.shape, 0)
    big_a, big_b = a, u
    k = 1
    while k < rows:
        a_sh = jnp.where(row >= k, pltpu.roll(big_a, k, axis=0), 1.0)
        b_sh = jnp.where(row >= k, pltpu.roll(big_b, k, axis=0), 0.0)
        big_b = big_a * b_sh + big_b
        big_a = big_a * a_sh
        k *= 2
    h = big_a * h_ref[0:1, :] + big_b
    h_ref[...] = h[rows - SUBLANES:]
    h_ref[0:1, :] = h[rows - 1:rows]
    o_ref[...] = h * jax.nn.gelu(gate_ref[...], approximate=True)


def rglru(p3, conv_w, conv_b, w_a, b_a, w_x, b_x, lam, l, tt):
    bsz, seq, _ = p3.shape
    tt = min(tt, seq)
    assert seq % tt == 0
    c = D_GROUP
    vec = lambda: pl.BlockSpec((None, 1, c), lambda b, t: (l, 0, 0))
    return pl.pallas_call(
        _rglru_kernel,
        grid=(bsz, seq // tt),
        in_specs=[
            pl.BlockSpec((None, tt, c), lambda b, t: (b, t, 3)),
            pl.BlockSpec((None, tt, c), lambda b, t: (b, t, 4)),
            pl.BlockSpec((None, RG_CONV, c), lambda b, t: (l, 0, 0)),
            vec(),
            pl.BlockSpec((None, c // HEAD_DIM, HEAD_DIM, HEAD_DIM), lambda b, t: (l, 0, 0, 0)),
            vec(),
            pl.BlockSpec((None, c // HEAD_DIM, HEAD_DIM, HEAD_DIM), lambda b, t: (l, 0, 0, 0)),
            vec(),
            vec(),
        ],
        out_specs=pl.BlockSpec((None, tt, c), lambda b, t: (b, t, 0)),
        out_shape=jax.ShapeDtypeStruct((bsz, seq, c), F32),
        scratch_shapes=[pltpu.VMEM((SUBLANES, c), F32), pltpu.VMEM((SUBLANES, c), F32)],
        compiler_params=_params("parallel", "arbitrary"),
        name="rglru",
    )(p3, p3, conv_w, conv_b, w_a, b_a, w_x, b_x, lam)


def _conformer_kernel(val_ref, gate_ref, cw_ref, cb_ref, ng_ref, nb_ref, o_ref, halo_ref):
    t = pl.program_id(1)
    rows = val_ref.shape[0]

    @pl.when(t == 0)
    def _():
        halo_ref[...] = jnp.zeros(halo_ref.shape, F32)

    glu = val_ref[...] * jax.nn.sigmoid(gate_ref[...])
    ext = jnp.concatenate([halo_ref[...], glu], axis=0)
    halo_ref[...] = glu[rows - CV_HALO:]
    rolled = [ext] + [pltpu.roll(ext, r, axis=0) for r in range(1, SUBLANES)]
    u = jnp.zeros_like(glu) + cb_ref[...]
    for j in range(CV_WIDTH):
        d = CV_WIDTH - 1 - j
        r = d % SUBLANES
        start = CV_HALO - (d - r)
        u = u + cw_ref[j:j + 1, :] * rolled[r][start:start + rows]
    parts = []
    for g in range(D_GROUP // HEAD_DIM):
        parts.append(_standardize(u[:, g * HEAD_DIM:(g + 1) * HEAD_DIM]))
    un = jnp.concatenate(parts, axis=-1)
    o_ref[...] = _silu(un * ng_ref[...] + nb_ref[...])


def conformer(p3, cv_w, cv_b, ng, nb, l, tt):
    bsz, seq, _ = p3.shape
    tt = min(tt, seq)
    assert seq % tt == 0 and tt >= CV_HALO
    c = D_GROUP
    vec = lambda: pl.BlockSpec((None, 1, c), lambda b, t: (l, 0, 0))
    return pl.pallas_call(
        _conformer_kernel,
        grid=(bsz, seq // tt),
        in_specs=[
            pl.BlockSpec((None, tt, c), lambda b, t: (b, t, 5)),
            pl.BlockSpec((None, tt, c), lambda b, t: (b, t, 6)),
            pl.BlockSpec((None, CV_WIDTH, c), lambda b, t: (l, 0, 0)),
            vec(), vec(), vec(),
        ],
        out_specs=pl.BlockSpec((None, tt, c), lambda b, t: (b, t, 0)),
        out_shape=jax.ShapeDtypeStruct((bsz, seq, c), F32),
        scratch_shapes=[pltpu.VMEM((CV_HALO, c), F32)],
        compiler_params=_params("parallel", "arbitrary"),
        name="conformer",
    )(p3, p3, cv_w, cv_b, ng, nb)


def _hgrn2_chunk(q, kk, v, lf, state_t):
    c = HG_CHUNK
    sub = HG_SUB
    row = lax.broadcasted_iota(jnp.int32, (c, HEAD_DIM), 0)
    b = lf
    k = 1
    while k < c:
        b = b + jnp.where(row >= k, pltpu.roll(b, k, axis=0), 0.0)
        k *= 2
    vb = v.astype(BF16)

    o = lax.dot_general((q * jnp.exp(b)).astype(BF16), state_t.astype(BF16), NT_DIMS,
                        preferred_element_type=F32)

    lane_s = lax.broadcasted_iota(jnp.int32, (sub, c), 1)
    row_s = lax.broadcasted_iota(jnp.int32, (sub, c), 0)
    row_k = lax.broadcasted_iota(jnp.int32, (sub, HEAD_DIM), 0)
    att_rows = []
    for blk in range(c // sub):
        lo = blk * sub
        q_i = q[lo:lo + sub]
        b_i = b[lo:lo + sub]
        k_i = kk[lo:lo + sub]
        att = jnp.zeros((sub, c), F32)
        if blk > 0:
            ref = b[lo - 1:lo]
            qs = (q_i * jnp.exp(b_i - ref)).astype(BF16)
            ks = kk[:lo] * jnp.exp(ref - b[:lo])
            ks = jnp.concatenate([ks, jnp.zeros((c - lo, HEAD_DIM), F32)], axis=0).astype(BF16)
            att = lax.dot_general(qs, ks, NT_DIMS, preferred_element_type=F32)
        for s in range(sub):
            decay = jnp.exp(jnp.minimum(b_i - b_i[s:s + 1], 0.0))
            w = jnp.sum(q_i * k_i[s:s + 1] * decay, axis=-1, keepdims=True)
            w = jnp.where(row_k[:, 0:1] >= s, w, 0.0)
            att = att + jnp.where((lane_s == lo + s) & (row_s >= s), w, 0.0)
        att_rows.append(att)
    att = jnp.concatenate(att_rows, axis=0)
    o = o + jnp.dot(att.astype(BF16), vb, preferred_element_type=F32)

    b_last = b[c - 1:c]
    ks = (kk * jnp.exp(b_last - b)).astype(BF16)
    new_state_t = state_t * jnp.exp(b_last) + lax.dot_general(vb, ks, TN_DIMS, preferred_element_type=F32)
    return o, new_state_t


def _hgrn2_kernel(q_ref, f_ref, v_ref, g_ref, lbp_ref, ng_ref, o_ref, state_ref, *, layer):
    t = pl.program_id(1)
    rows = q_ref.shape[0]

    @pl.when(t == 0)
    def _():
        state_ref[...] = jnp.zeros(state_ref.shape, F32)

    lbp = lbp_ref[...]
    e = jnp.exp(lbp - jnp.max(lbp, axis=0, keepdims=True))
    sm = e / jnp.sum(e, axis=0, keepdims=True)
    lb = jnp.sum(sm[0:layer + 1], axis=0, keepdims=True) - sm[0:1]

    for h in range(N_HEADS):
        cs = slice(h * HEAD_DIM, (h + 1) * HEAD_DIM)
        lb_h = lb[:, cs]
        state_t = state_ref[h]
        for ci in range(rows // HG_CHUNK):
            rs = slice(ci * HG_CHUNK, (ci + 1) * HG_CHUNK)
            sig = jax.nn.sigmoid(f_ref[rs, cs])
            lf = jnp.log(lb_h + (1.0 - lb_h) * sig)
            kk = (1.0 - lb_h) * (1.0 - sig)
            o, state_t = _hgrn2_chunk(q_ref[rs, cs], kk, v_ref[rs, cs], lf, state_t)
            o = o * lax.rsqrt(jnp.mean(o * o, axis=-1, keepdims=True) + LN_EPS)
            o_ref[rs, cs] = o * ng_ref[:, cs] * _silu(g_ref[rs, cs])
        state_ref[h] = state_t


def hgrn2(p3, lower_bound_params, norm_g, l, tt):
    bsz, seq, _ = p3.shape
    tt = min(tt, seq)
    assert seq % tt == 0 and tt % HG_CHUNK == 0
    c = D_GROUP
    depth = lower_bound_params.shape[0]
    tok = lambda col: pl.BlockSpec((None, tt, c), lambda b, t: (b, t, col))
    return pl.pallas_call(
        functools.partial(_hgrn2_kernel, layer=l),
        grid=(bsz, seq // tt),
        in_specs=[
            tok(7), tok(8), tok(9), tok(10),
            pl.BlockSpec((depth, c), lambda b, t: (0, 0)),
            pl.BlockSpec((None, 1, c), lambda b, t: (l, 0, 0)),
        ],
        out_specs=pl.BlockSpec((None, tt, c), lambda b, t: (b, t, 0)),
        out_shape=jax.ShapeDtypeStruct((bsz, seq, c), F32),
        scratch_shapes=[pltpu.VMEM((N_HEADS, HEAD_DIM, HEAD_DIM), F32)],
        compiler_params=_params("parallel", "arbitrary"),
        name="hgrn2",
    )(p3, p3, p3, p3, lower_bound_params, norm_g)


def _out_proj_ln_kernel(x_ref, ya_ref, yb_ref, yc_ref, yd_ref, w_ref, g_ref, b_ref, o_ref, *, alpha):
    c = D_GROUP
    y = jnp.dot(ya_ref[...].astype(BF16), w_ref[0:c, :], preferred_element_type=F32)
    y = y + jnp.dot(yb_ref[...].astype(BF16), w_ref[c:2 * c, :], preferred_element_type=F32)
    y = y + jnp.dot(yc_ref[...].astype(BF16), w_ref[2 * c:3 * c, :], preferred_element_type=F32)
    y = y + jnp.dot(yd_ref[...].astype(BF16), w_ref[3 * c:4 * c, :], preferred_element_type=F32)
    z = alpha * x_ref[...] + y
    o_ref[...] = _standardize(z) * g_ref[...] + b_ref[...]


def out_proj_ln(x, ya, yb, yc, yd, w_out, ln_g, ln_b, l, jn, alpha, tm):
    m, d = x.shape
    c = D_GROUP
    tm = min(tm, m)
    assert m % tm == 0
    mix = lambda: pl.BlockSpec((tm, c), lambda i: (i, 0))
    return pl.pallas_call(
        functools.partial(_out_proj_ln_kernel, alpha=alpha),
        grid=(m // tm,),
        in_specs=[
            pl.BlockSpec((tm, d), lambda i: (i, 0)),
            mix(), mix(), mix(), mix(),
            pl.BlockSpec((None, 4 * c, d), lambda i: (l, 0, 0)),
            pl.BlockSpec((None, 1, d), lambda i: (jn, 0, 0)),
            pl.BlockSpec((None, 1, d), lambda i: (jn, 0, 0)),
        ],
        out_specs=pl.BlockSpec((tm, d), lambda i: (i, 0)),
        out_shape=jax.ShapeDtypeStruct((m, d), F32),
        compiler_params=_params("parallel"),
        name="out_proj_ln",
    )(x, ya, yb, yc, yd, w_out, ln_g, ln_b)


def kernel(x, ln_g, ln_b, ffn_w_gate, ffn_w_up, ffn_w_down, w_in, w_out, rg_conv_w, rg_conv_b,
           rg_w_a, rg_b_a, rg_w_x, rg_b_x, rg_lambda, cv_w, cv_b, cv_ln_g, cv_ln_b,
           hg_lower_bounds, hg_norm_g):
    bsz, seq, d = x.shape
    depth = w_in.shape[0]
    alpha = (2 * depth) ** 0.25
    m = bsz * seq

    wg = ffn_w_gate.astype(BF16)
    wu = ffn_w_up.astype(BF16)
    wd = ffn_w_down.astype(BF16)
    w_in_b = w_in.astype(BF16)
    w_out_b = w_out.astype(BF16)
    ln_g3 = ln_g.reshape(depth * 3, 1, d)
    ln_b3 = ln_b.reshape(depth * 3, 1, d)
    row = lambda a: a.reshape(depth, 1, a.shape[-1])

    h = x.reshape(m, d)
    for l in range(depth):
        h = ffn_ln(h, wg, wu, wd, ln_g3, ln_b3, l, 0, 3 * l, alpha, tm=512, tf=512)
        p3 = in_proj(h, w_in_b, l, tm=1024, tn=D_GROUP).reshape(bsz, seq, -1)
        y_a = moba(p3)
        y_b = rglru(p3, rg_conv_w, row(rg_conv_b), rg_w_a, row(rg_b_a), rg_w_x, row(rg_b_x),
                    row(rg_lambda), l, tt=256)
        y_c = conformer(p3, cv_w, row(cv_b), row(cv_ln_g), row(cv_ln_b), l, tt=256)
        y_d = hgrn2(p3, hg_lower_bounds, row(hg_norm_g), l, tt=256)
        flat = lambda y: y.reshape(m, D_GROUP)
        h = out_proj_ln(h, flat(y_a), flat(y_b), flat(y_c), flat(y_d), w_out_b, ln_g3, ln_b3,
                        l, 3 * l + 1, alpha, tm=512)
        h = ffn_ln(h, wg, wu, wd, ln_g3, ln_b3, l, 1, 3 * l + 2, alpha, tm=512, tf=512)
    return h.reshape(bsz, seq, d)
```

```python
import functools

import jax
import jax.numpy as jnp
from jax import lax
from jax.experimental import pallas as pl
from jax.experimental.pallas import tpu as pltpu

F32 = jnp.float32
BF16 = jnp.bfloat16

LANES = 128
SUBLANES = 8
VMEM_LIMIT_BYTES = 56 * 1024 * 1024

D_GROUP = 512
HEAD_DIM = 128
N_HEADS = D_GROUP // HEAD_DIM
MOBA_BLOCK = 256
MOBA_TOPK = 3
MOBA_GROUP = 4
LOG2_E = 1.4426950408889634
MASK_VALUE = -1e30
RG_C = 8.0
RG_CONV = 4
CV_WIDTH = 31
CV_HALO = 32
FFN_DOWN_COLS = 512
HG_CHUNK = 64
HG_SUB = 16
LN_EPS = 1e-5

NT_DIMS = (((1,), (1,)), ((), ()))
TN_DIMS = (((0,), (0,)), ((), ()))


def _params(*sem):
    return pltpu.CompilerParams(dimension_semantics=sem, vmem_limit_bytes=VMEM_LIMIT_BYTES)


def _standardize(z):
    mu = jnp.mean(z, axis=-1, keepdims=True)
    zc = z - mu
    var = jnp.mean(zc * zc, axis=-1, keepdims=True)
    return zc * lax.rsqrt(var + LN_EPS)


def _silu(z):
    return z * jax.nn.sigmoid(z)


def _shift_rows(ext, d, halo, rows):
    if d % SUBLANES == 0:
        return ext[halo - d:halo - d + rows]
    r = d % SUBLANES
    base = d - r
    rolled = pltpu.roll(ext, r, axis=0)
    return rolled[halo - base:halo - base + rows]


def _ffn_ln_kernel(x_ref, wg_ref, wu_ref, wd_ref, g_ref, b_ref, o_ref, xb_ref, *, alpha):
    f = pl.program_id(1)

    @pl.when(f == 0)
    def _():
        xb_ref[...] = x_ref[...].astype(BF16)
        o_ref[...] = jnp.zeros(o_ref.shape, F32)

    xb = xb_ref[...]
    hg = jnp.dot(xb, wg_ref[...], preferred_element_type=F32)
    hu = jnp.dot(xb, wu_ref[...], preferred_element_type=F32)
    h = (_silu(hg) * hu).astype(BF16)
    d = o_ref.shape[1]
    for c0 in range(0, d, FFN_DOWN_COLS):
        cs = slice(c0, min(c0 + FFN_DOWN_COLS, d))
        o_ref[:, cs] += jnp.dot(h, wd_ref[:, cs], preferred_element_type=F32)

    @pl.when(f == pl.num_programs(1) - 1)
    def _():
        z = alpha * x_ref[...] + 0.5 * o_ref[...]
        o_ref[...] = _standardize(z) * g_ref[...] + b_ref[...]


def ffn_ln(x, wg, wu, wd, ln_g, ln_b, l, j, jn, alpha, tm, tf):
    m, d = x.shape
    dff = wg.shape[-1]
    tm = min(tm, m)
    tf = min(tf, dff)
    assert m % tm == 0 and dff % tf == 0
    grid = (m // tm, dff // tf)
    return pl.pallas_call(
        functools.partial(_ffn_ln_kernel, alpha=alpha),
        grid=grid,
        in_specs=[
            pl.BlockSpec((tm, d), lambda i, f: (i, 0), pipeline_mode=pl.Buffered(1)),
            pl.BlockSpec((None, None, d, tf), lambda i, f: (l, j, 0, f)),
            pl.BlockSpec((None, None, d, tf), lambda i, f: (l, j, 0, f)),
            pl.BlockSpec((None, None, tf, d), lambda i, f: (l, j, f, 0)),
            pl.BlockSpec((None, 1, d), lambda i, f: (jn, 0, 0)),
            pl.BlockSpec((None, 1, d), lambda i, f: (jn, 0, 0)),
        ],
        out_specs=pl.BlockSpec((tm, d), lambda i, f: (i, 0)),
        out_shape=jax.ShapeDtypeStruct((m, d), F32),
        scratch_shapes=[pltpu.VMEM((tm, d), BF16)],
        compiler_params=_params("parallel", "arbitrary"),
        name="ffn_ln",
    )(x, wg, wu, wd, ln_g, ln_b)


def _in_proj_kernel(x_ref, w_ref, o_ref, xb_ref):
    @pl.when(pl.program_id(1) == 0)
    def _():
        xb_ref[...] = x_ref[...].astype(BF16)

    o_ref[...] = jnp.dot(xb_ref[...], w_ref[...], preferred_element_type=F32)


def in_proj(x, w_in, l, tm, tn):
    m, d = x.shape
    n = w_in.shape[-1]
    tm = min(tm, m)
    assert m % tm == 0 and n % tn == 0
    return pl.pallas_call(
        _in_proj_kernel,
        grid=(m // tm, n // tn),
        in_specs=[
            pl.BlockSpec((tm, d), lambda i, c: (i, 0)),
            pl.BlockSpec((None, d, tn), lambda i, c: (l, 0, c)),
        ],
        out_specs=pl.BlockSpec((tm, tn), lambda i, c: (i, c)),
        out_shape=jax.ShapeDtypeStruct((m, n), F32),
        scratch_shapes=[pltpu.VMEM((tm, d), BF16)],
        compiler_params=_params("parallel", "arbitrary"),
        name="in_proj",
    )(x, w_in)


def _moba_kernel(q_ref, k_ref, v_ref, o_ref, kb_ref, vb_ref, km_ref, s_ref, *, nb, scale):
    i = pl.program_id(2)
    blk = MOBA_BLOCK
    grp = MOBA_GROUP
    half = blk // 2

    @pl.when(i == 0)
    def _():
        k = k_ref[...]
        kb_ref[...] = k.astype(BF16)
        vb_ref[...] = v_ref[...].astype(BF16)
        km_ref[...] = jnp.zeros(km_ref.shape, F32)
        km_ref[0:nb, :] = jnp.mean(k.reshape(nb, blk, HEAD_DIM), axis=1)

    q = q_ref[...]
    qs = (q * (scale * LOG2_E)).astype(BF16)

    gate = lax.dot_general(q.astype(BF16), km_ref[...].astype(BF16), NT_DIMS, preferred_element_type=F32)
    lane = lax.broadcasted_iota(jnp.int32, (blk, LANES), 1)
    lane_f = lane.astype(F32)
    past = lane < i
    g = jnp.where(past, gate, MASK_VALUE)
    sel = jnp.zeros((blk, LANES), F32)
    for _ in range(MOBA_TOPK):
        top = jnp.max(g, axis=-1, keepdims=True)
        first = jnp.min(jnp.where(g == top, lane_f, float(LANES)), axis=-1, keepdims=True)
        pick = lane_f == first
        sel = jnp.where(pick, 1.0, sel)
        g = jnp.where(pick, -jnp.inf, g)
    bias = jnp.where(past & (sel > 0.5), 0.0, MASK_VALUE)

    own = pl.multiple_of(i * blk, blk)
    s = lax.dot_general(qs, kb_ref[pl.ds(own, blk), :], NT_DIMS, preferred_element_type=F32)
    row = lax.broadcasted_iota(jnp.int32, (blk, blk), 0)
    col = lax.broadcasted_iota(jnp.int32, (blk, blk), 1)
    s = jnp.where(col <= row, s, MASK_VALUE)
    s_ref[nb] = s
    mrun0 = jnp.maximum(s[:, :half], s[:, half:])

    n_groups = (i + grp - 1) // grp

    def scores(gi, mrun):
        start = pl.multiple_of(gi * (grp * blk), grp * blk)
        s = lax.dot_general(qs, kb_ref[pl.ds(start, grp * blk), :], NT_DIMS, preferred_element_type=F32)
        for j in range(grp):
            n = gi * grp + j
            bias_n = jnp.sum(jnp.where(lane == n, bias, 0.0), axis=-1, keepdims=True)
            s_n = s[:, j * blk:(j + 1) * blk] + bias_n
            s_ref[n] = s_n
            mrun = jnp.maximum(mrun, jnp.maximum(s_n[:, :half], s_n[:, half:]))
        return mrun

    mrun = lax.fori_loop(0, n_groups, scores, mrun0)
    m = jnp.max(mrun, axis=-1, keepdims=True)

    p = jnp.exp2(s_ref[nb] - m)
    lrun0 = p[:, :half] + p[:, half:]
    acc0 = jnp.dot(p.astype(BF16), vb_ref[pl.ds(own, blk), :], preferred_element_type=F32)

    def weighted(gi, carry):
        lrun, acc = carry
        start = pl.multiple_of(gi * (grp * blk), grp * blk)
        ps = []
        for j in range(grp):
            p = jnp.exp2(s_ref[gi * grp + j] - m)
            lrun = lrun + (p[:, :half] + p[:, half:])
            ps.append(p.astype(BF16))
        pcat = jnp.concatenate(ps, axis=-1)
        acc = acc + jnp.dot(pcat, vb_ref[pl.ds(start, grp * blk), :], preferred_element_type=F32)
        return lrun, acc

    lrun, acc = lax.fori_loop(0, n_groups, weighted, (lrun0, acc0))
    o_ref[...] = acc / jnp.sum(lrun, axis=-1, keepdims=True)


def moba(p3):
    bsz, seq, _ = p3.shape
    blk = MOBA_BLOCK
    assert seq % (blk * MOBA_GROUP) == 0
    nb = seq // blk
    assert nb <= LANES
    kernel = functools.partial(_moba_kernel, nb=nb, scale=HEAD_DIM ** -0.5)
    return pl.pallas_call(
        kernel,
        grid=(bsz, N_HEADS, nb),
        in_specs=[
            pl.BlockSpec((None, blk, HEAD_DIM), lambda b, h, i: (b, i, h)),
            pl.BlockSpec((None, seq, HEAD_DIM), lambda b, h, i: (b, 0, N_HEADS + h)),
            pl.BlockSpec((None, seq, HEAD_DIM), lambda b, h, i: (b, 0, 2 * N_HEADS + h)),
        ],
        out_specs=pl.BlockSpec((None, blk, HEAD_DIM), lambda b, h, i: (b, i, h)),
        out_shape=jax.ShapeDtypeStruct((bsz, seq, D_GROUP), F32),
        scratch_shapes=[
            pltpu.VMEM((seq, HEAD_DIM), BF16),
            pltpu.VMEM((seq, HEAD_DIM), BF16),
            pltpu.VMEM((LANES, HEAD_DIM), F32),
            pltpu.VMEM((nb + 1, blk, blk), F32),
        ],
        compiler_params=_params("parallel", "parallel", "arbitrary"),
        name="moba",
    )(p3, p3, p3)


def _rglru_kernel(gate_ref, x_ref, cw_ref, cb_ref, wa_ref, ba_ref, wx_ref, bx_ref, lam_ref,
                  o_ref, halo_ref, h_ref):
    t = pl.program_id(1)
    rows = x_ref.shape[0]

    @pl.when(t == 0)
    def _():
        halo_ref[...] = jnp.zeros(halo_ref.shape, F32)
        h_ref[...] = jnp.zeros(h_ref.shape, F32)

    x = x_ref[...]
    ext = jnp.concatenate([halo_ref[...], x], axis=0)
    halo_ref[...] = x[rows - SUBLANES:]
    xc = jnp.zeros_like(x) + cb_ref[...]
    for j in range(RG_CONV):
        xc = xc + cw_ref[j:j + 1, :] * _shift_rows(ext, RG_CONV - 1 - j, SUBLANES, rows)

    xcb = xc.astype(BF16)
    r_parts, i_parts = [], []
    for g in range(D_GROUP // HEAD_DIM):
        blk = xcb[:, g * HEAD_DIM:(g + 1) * HEAD_DIM]
        r_parts.append(jnp.dot(blk, wa_ref[g].astype(BF16), preferred_element_type=F32))
        i_parts.append(jnp.dot(blk, wx_ref[g].astype(BF16), preferred_element_type=F32))
    r = jax.nn.sigmoid(jnp.concatenate(r_parts, axis=-1) + ba_ref[...])
    ig = jax.nn.sigmoid(jnp.concatenate(i_parts, axis=-1) + bx_ref[...])
    neg_lam = -lam_ref[...]
    softplus = jnp.maximum(neg_lam, 0.0) + jnp.log1p(jnp.exp(-jnp.abs(neg_lam)))
    log_a = -RG_C * r * softplus
    a = jnp.exp(log_a)
    u = jnp.sqrt(jnp.maximum(1.0 - jnp.exp(2.0 * log_a), 0.0)) * (ig * xc)

    row = lax.broadcasted_iota(jnp.int32, a.shape, 0)
    big_a, big_b = a, u
    k = 1
    while k < rows:
        a_sh = jnp.where(row >= k, pltpu.roll(big_a, k, axis=0), 1.0)
        b_sh = jnp.where(row >= k, pltpu.roll(big_b, k, axis=0), 0.0)
        big_b = big_a * b_sh + big_b
        big_a = big_a * a_sh
        k *= 2
    h = big_a * h_ref[0:1, :] + big_b
    h_ref[...] = h[rows - SUBLANES:]
    h_ref[0:1, :] = h[rows - 1:rows]
    o_ref[...] = h * jax.nn.gelu(gate_ref[...], approximate=True)


def rglru(p3, conv_w, conv_b, w_a, b_a, w_x, b_x, lam, l, tt):
    bsz, seq, _ = p3.shape
    tt = min(tt, seq)
    assert seq % tt == 0
    c = D_GROUP
    vec = lambda: pl.BlockSpec((None, 1, c), lambda b, t: (l, 0, 0))
    return pl.pallas_call(
        _rglru_kernel,
        grid=(bsz, seq // tt),
        in_specs=[
            pl.BlockSpec((None, tt, c), lambda b, t: (b, t, 3)),
            pl.BlockSpec((None, tt, c), lambda b, t: (b, t, 4)),
            pl.BlockSpec((None, RG_CONV, c), lambda b, t: (l, 0, 0)),
            vec(),
            pl.BlockSpec((None, c // HEAD_DIM, HEAD_DIM, HEAD_DIM), lambda b, t: (l, 0, 0, 0)),
            vec(),
            pl.BlockSpec((None, c // HEAD_DIM, HEAD_DIM, HEAD_DIM), lambda b, t: (l, 0, 0, 0)),
            vec(),
            vec(),
        ],
        out_specs=pl.BlockSpec((None, tt, c), lambda b, t: (b, t, 0)),
        out_shape=jax.ShapeDtypeStruct((bsz, seq, c), F32),
        scratch_shapes=[pltpu.VMEM((SUBLANES, c), F32), pltpu.VMEM((SUBLANES, c), F32)],
        compiler_params=_params("parallel", "arbitrary"),
        name="rglru",
    )(p3, p3, conv_w, conv_b, w_a, b_a, w_x, b_x, lam)


def _conformer_kernel(val_ref, gate_ref, cw_ref, cb_ref, ng_ref, nb_ref, o_ref, halo_ref):
    t = pl.program_id(1)
    rows = val_ref.shape[0]

    @pl.when(t == 0)
    def _():
        halo_ref[...] = jnp.zeros(halo_ref.shape, F32)

    glu = val_ref[...] * jax.nn.sigmoid(gate_ref[...])
    ext = jnp.concatenate([halo_ref[...], glu], axis=0)
    halo_ref[...] = glu[rows - CV_HALO:]
    rolled = [ext] + [pltpu.roll(ext, r, axis=0) for r in range(1, SUBLANES)]
    u = jnp.zeros_like(glu) + cb_ref[...]
    for j in range(CV_WIDTH):
        d = CV_WIDTH - 1 - j
        r = d % SUBLANES
        start = CV_HALO - (d - r)
        u = u + cw_ref[j:j + 1, :] * rolled[r][start:start + rows]
    parts = []
    for g in range(D_GROUP // HEAD_DIM):
        parts.append(_standardize(u[:, g * HEAD_DIM:(g + 1) * HEAD_DIM]))
    un = jnp.concatenate(parts, axis=-1)
    o_ref[...] = _silu(un * ng_ref[...] + nb_ref[...])


def conformer(p3, cv_w, cv_b, ng, nb, l, tt):
    bsz, seq, _ = p3.shape
    tt = min(tt, seq)
    assert seq % tt == 0 and tt >= CV_HALO
    c = D_GROUP
    vec = lambda: pl.BlockSpec((None, 1, c), lambda b, t: (l, 0, 0))
    return pl.pallas_call(
        _conformer_kernel,
        grid=(bsz, seq // tt),
        in_specs=[
            pl.BlockSpec((None, tt, c), lambda b, t: (b, t, 5)),
            pl.BlockSpec((None, tt, c), lambda b, t: (b, t, 6)),
            pl.BlockSpec((None, CV_WIDTH, c), lambda b, t: (l, 0, 0)),
            vec(), vec(), vec(),
        ],
        out_specs=pl.BlockSpec((None, tt, c), lambda b, t: (b, t, 0)),
        out_shape=jax.ShapeDtypeStruct((bsz, seq, c), F32),
        scratch_shapes=[pltpu.VMEM((CV_HALO, c), F32)],
        compiler_params=_params("parallel", "arbitrary"),
        name="conformer",
    )(p3, p3, cv_w, cv_b, ng, nb)


def _hgrn2_chunk(q, kk, v, lf, state_t):
    c = HG_CHUNK
    sub = HG_SUB
    row = lax.broadcasted_iota(jnp.int32, (c, HEAD_DIM), 0)
    b = lf
    k = 1
    while k < c:
        b = b + jnp.where(row >= k, pltpu.roll(b, k, axis=0), 0.0)
        k *= 2
    vb = v.astype(BF16)

    o = lax.dot_general((q * jnp.exp(b)).astype(BF16), state_t.astype(BF16), NT_DIMS,
                        preferred_element_type=F32)

    lane_s = lax.broadcasted_iota(jnp.int32, (sub, c), 1)
    row_s = lax.broadcasted_iota(jnp.int32, (sub, c), 0)
    row_k = lax.broadcasted_iota(jnp.int32, (sub, HEAD_DIM), 0)
    att_rows = []
    for blk in range(c // sub):
        lo = blk * sub
        q_i = q[lo:lo + sub]
        b_i = b[lo:lo + sub]
        k_i = kk[lo:lo + sub]
        att = jnp.zeros((sub, c), F32)
        if blk > 0:
            ref = b[lo - 1:lo]
            qs = (q_i * jnp.exp(b_i - ref)).astype(BF16)
            ks = kk[:lo] * jnp.exp(ref - b[:lo])
            ks = jnp.concatenate([ks, jnp.zeros((c - lo, HEAD_DIM), F32)], axis=0).astype(BF16)
            att = lax.dot_general(qs, ks, NT_DIMS, preferred_element_type=F32)
        for s in range(sub):
            decay = jnp.exp(jnp.minimum(b_i - b_i[s:s + 1], 0.0))
            w = jnp.sum(q_i * k_i[s:s + 1] * decay, axis=-1, keepdims=True)
            w = jnp.where(row_k[:, 0:1] >= s, w, 0.0)
            att = att + jnp.where((lane_s == lo + s) & (row_s >= s), w, 0.0)
        att_rows.append(att)
    att = jnp.concatenate(att_rows, axis=0)
    o = o + jnp.dot(att.astype(BF16), vb, preferred_element_type=F32)

    b_last = b[c - 1:c]
    ks = (kk * jnp.exp(b_last - b)).astype(BF16)
    new_state_t = state_t * jnp.exp(b_last) + lax.dot_general(vb, ks, TN_DIMS, preferred_element_type=F32)
    return o, new_state_t


def _hgrn2_kernel(q_ref, f_ref, v_ref, g_ref, lbp_ref, ng_ref, o_ref, state_ref, *, layer):
    t = pl.program_id(1)
    rows = q_ref.shape[0]

    @pl.when(t == 0)
    def _():
        state_ref[...] = jnp.zeros(state_ref.shape, F32)

    lbp = lbp_ref[...]
    e = jnp.exp(lbp - jnp.max(lbp, axis=0, keepdims=True))
    sm = e / jnp.sum(e, axis=0, keepdims=True)
    lb = jnp.sum(sm[0:layer + 1], axis=0, keepdims=True) - sm[0:1]

    for h in range(N_HEADS):
        cs = slice(h * HEAD_DIM, (h + 1) * HEAD_DIM)
        lb_h = lb[:, cs]
        state_t = state_ref[h]
        for ci in range(rows // HG_CHUNK):
            rs = slice(ci * HG_CHUNK, (ci + 1) * HG_CHUNK)
            sig = jax.nn.sigmoid(f_ref[rs, cs])
            lf = jnp.log(lb_h + (1.0 - lb_h) * sig)
            kk = (1.0 - lb_h) * (1.0 - sig)
            o, state_t = _hgrn2_chunk(q_ref[rs, cs], kk, v_ref[rs, cs], lf, state_t)
            o = o * lax.rsqrt(jnp.mean(o * o, axis=-1, keepdims=True) + LN_EPS)
            o_ref[rs, cs] = o * ng_ref[:, cs] * _silu(g_ref[rs, cs])
        state_ref[h] = state_t


def hgrn2(p3, lower_bound_params, norm_g, l, tt):
    bsz, seq, _ = p3.shape
    tt = min(tt, seq)
    assert seq % tt == 0 and tt % HG_CHUNK == 0
    c = D_GROUP
    depth = lower_bound_params.shape[0]
    tok = lambda col: pl.BlockSpec((None, tt, c), lambda b, t: (b, t, col))
    return pl.pallas_call(
        functools.partial(_hgrn2_kernel, layer=l),
        grid=(bsz, seq // tt),
        in_specs=[
            tok(7), tok(8), tok(9), tok(10),
            pl.BlockSpec((depth, c), lambda b, t: (0, 0)),
            pl.BlockSpec((None, 1, c), lambda b, t: (l, 0, 0)),
        ],
        out_specs=pl.BlockSpec((None, tt, c), lambda b, t: (b, t, 0)),
        out_shape=jax.ShapeDtypeStruct((bsz, seq, c), F32),
        scratch_shapes=[pltpu.VMEM((N_HEADS, HEAD_DIM, HEAD_DIM), F32)],
        compiler_params=_params("parallel", "arbitrary"),
        name="hgrn2",
    )(p3, p3, p3, p3, lower_bound_params, norm_g)


def _out_proj_ln_kernel(x_ref, ya_ref, yb_ref, yc_ref, yd_ref, w_ref, g_ref, b_ref, o_ref, *, alpha):
    c = D_GROUP
    y = jnp.dot(ya_ref[...].astype(BF16), w_ref[0:c, :], preferred_element_type=F32)
    y = y + jnp.dot(yb_ref[...].astype(BF16), w_ref[c:2 * c, :], preferred_element_type=F32)
    y = y + jnp.dot(yc_ref[...].astype(BF16), w_ref[2 * c:3 * c, :], preferred_element_type=F32)
    y = y + jnp.dot(yd_ref[...].astype(BF16), w_ref[3 * c:4 * c, :], preferred_element_type=F32)
    z = alpha * x_ref[...] + y
    o_ref[...] = _standardize(z) * g_ref[...] + b_ref[...]


def out_proj_ln(x, ya, yb, yc, yd, w_out, ln_g, ln_b, l, jn, alpha, tm):
    m, d = x.shape
    c = D_GROUP
    tm = min(tm, m)
    assert m % tm == 0
    mix = lambda: pl.BlockSpec((tm, c), lambda i: (i, 0))
    return pl.pallas_call(
        functools.partial(_out_proj_ln_kernel, alpha=alpha),
        grid=(m // tm,),
        in_specs=[
            pl.BlockSpec((tm, d), lambda i: (i, 0)),
            mix(), mix(), mix(), mix(),
            pl.BlockSpec((None, 4 * c, d), lambda i: (l, 0, 0)),
            pl.BlockSpec((None, 1, d), lambda i: (jn, 0, 0)),
            pl.BlockSpec((None, 1, d), lambda i: (jn, 0, 0)),
        ],
        out_specs=pl.BlockSpec((tm, d), lambda i: (i, 0)),
        out_shape=jax.ShapeDtypeStruct((m, d), F32),
        compiler_params=_params("parallel"),
        name="out_proj_ln",
    )(x, ya, yb, yc, yd, w_out, ln_g, ln_b)


def kernel(x, ln_g, ln_b, ffn_w_gate, ffn_w_up, ffn_w_down, w_in, w_out, rg_conv_w, rg_conv_b,
           rg_w_a, rg_b_a, rg_w_x, rg_b_x, rg_lambda, cv_w, cv_b, cv_ln_g, cv_ln_b,
           hg_lower_bounds, hg_norm_g):
    bsz, seq, d = x.shape
    depth = w_in.shape[0]
    alpha = (2 * depth) ** 0.25
    m = bsz * seq

    wg = ffn_w_gate.astype(BF16)
    wu = ffn_w_up.astype(BF16)
    wd = ffn_w_down.astype(BF16)
    w_in_b = w_in.astype(BF16)
    w_out_b = w_out.astype(BF16)
    ln_g3 = ln_g.reshape(depth * 3, 1, d)
    ln_b3 = ln_b.reshape(depth * 3, 1, d)
    row = lambda a: a.reshape(depth, 1, a.shape[-1])

    h = x.reshape(m, d)
    for l in range(depth):
        h = ffn_ln(h, wg, wu, wd, ln_g3, ln_b3, l, 0, 3 * l, alpha, tm=1024, tf=512)
        p3 = in_proj(h, w_in_b, l, tm=1024, tn=D_GROUP).reshape(bsz, seq, -1)
        y_a = moba(p3)
        y_b = rglru(p3, rg_conv_w, row(rg_conv_b), rg_w_a, row(rg_b_a), rg_w_x, row(rg_b_x),
                    row(rg_lambda), l, tt=256)
        y_c = conformer(p3, cv_w, row(cv_b), row(cv_ln_g), row(cv_ln_b), l, tt=256)
        y_d = hgrn2(p3, hg_lower_bounds, row(hg_norm_g), l, tt=256)
        flat = lambda y: y.reshape(m, D_GROUP)
        h = out_proj_ln(h, flat(y_a), flat(y_b), flat(y_c), flat(y_d), w_out_b, ln_g3, ln_b3,
                        l, 3 * l + 1, alpha, tm=512)
        h = ffn_ln(h, wg, wu, wd, ln_g3, ln_b3, l, 1, 3 * l + 2, alpha, tm=1024, tf=512)
    return h.reshape(bsz, seq, d)
```

```python
import functools

import jax
import jax.numpy as jnp
from jax import lax
from jax.experimental import pallas as pl
from jax.experimental.pallas import tpu as pltpu

F32 = jnp.float32
BF16 = jnp.bfloat16

LANES = 128
SUBLANES = 8
VMEM_LIMIT_BYTES = 56 * 1024 * 1024

D_GROUP = 512
HEAD_DIM = 128
N_HEADS = D_GROUP // HEAD_DIM
MOBA_BLOCK = 256
MOBA_TOPK = 3
MOBA_GROUP = 4
LOG2_E = 1.4426950408889634
MASK_VALUE = -1e30
RG_C = 8.0
RG_CONV = 4
CV_WIDTH = 31
CV_HALO = 32
FFN_DOWN_COLS = 512
HG_CHUNK = 64
HG_SUB = 8
LN_EPS = 1e-5

NT_DIMS = (((1,), (1,)), ((), ()))
TN_DIMS = (((0,), (0,)), ((), ()))


def _params(*sem):
    return pltpu.CompilerParams(dimension_semantics=sem, vmem_limit_bytes=VMEM_LIMIT_BYTES)


def _standardize(z):
    mu = jnp.mean(z, axis=-1, keepdims=True)
    zc = z - mu
    var = jnp.mean(zc * zc, axis=-1, keepdims=True)
    return zc * lax.rsqrt(var + LN_EPS)


def _silu(z):
    return z * jax.nn.sigmoid(z)


def _shift_rows(ext, d, halo, rows):
    if d % SUBLANES == 0:
        return ext[halo - d:halo - d + rows]
    r = d % SUBLANES
    base = d - r
    rolled = pltpu.roll(ext, r, axis=0)
    return rolled[halo - base:halo - base + rows]


def _ffn_ln_kernel(x_ref, wg_ref, wu_ref, wd_ref, g_ref, b_ref, o_ref, xb_ref, *, alpha):
    f = pl.program_id(1)

    @pl.when(f == 0)
    def _():
        xb_ref[...] = x_ref[...].astype(BF16)
        o_ref[...] = jnp.zeros(o_ref.shape, F32)

    xb = xb_ref[...]
    hg = jnp.dot(xb, wg_ref[...], preferred_element_type=F32)
    hu = jnp.dot(xb, wu_ref[...], preferred_element_type=F32)
    h = (_silu(hg) * hu).astype(BF16)
    d = o_ref.shape[1]
    for c0 in range(0, d, FFN_DOWN_COLS):
        cs = slice(c0, min(c0 + FFN_DOWN_COLS, d))
        o_ref[:, cs] += jnp.dot(h, wd_ref[:, cs], preferred_element_type=F32)

    @pl.when(f == pl.num_programs(1) - 1)
    def _():
        z = alpha * x_ref[...] + 0.5 * o_ref[...]
        o_ref[...] = _standardize(z) * g_ref[...] + b_ref[...]


def ffn_ln(x, wg, wu, wd, ln_g, ln_b, l, j, jn, alpha, tm, tf):
    m, d = x.shape
    dff = wg.shape[-1]
    tm = min(tm, m)
    tf = min(tf, dff)
    assert m % tm == 0 and dff % tf == 0
    grid = (m // tm, dff // tf)
    return pl.pallas_call(
        functools.partial(_ffn_ln_kernel, alpha=alpha),
        grid=grid,
        in_specs=[
            pl.BlockSpec((tm, d), lambda i, f: (i, 0), pipeline_mode=pl.Buffered(1)),
            pl.BlockSpec((None, None, d, tf), lambda i, f: (l, j, 0, f)),
            pl.BlockSpec((None, None, d, tf), lambda i, f: (l, j, 0, f)),
            pl.BlockSpec((None, None, tf, d), lambda i, f: (l, j, f, 0)),
            pl.BlockSpec((None, 1, d), lambda i, f: (jn, 0, 0)),
            pl.BlockSpec((None, 1, d), lambda i, f: (jn, 0, 0)),
        ],
        out_specs=pl.BlockSpec((tm, d), lambda i, f: (i, 0)),
        out_shape=jax.ShapeDtypeStruct((m, d), F32),
        scratch_shapes=[pltpu.VMEM((tm, d), BF16)],
        compiler_params=_params("parallel", "arbitrary"),
        name="ffn_ln",
    )(x, wg, wu, wd, ln_g, ln_b)


def _in_proj_kernel(x_ref, w_ref, o_ref, xb_ref):
    @pl.when(pl.program_id(1) == 0)
    def _():
        xb_ref[...] = x_ref[...].astype(BF16)

    o_ref[...] = jnp.dot(xb_ref[...], w_ref[...], preferred_element_type=F32)


def in_proj(x, w_in, l, tm, tn):
    m, d = x.shape
    n = w_in.shape[-1]
    tm = min(tm, m)
    assert m % tm == 0 and n % tn == 0
    return pl.pallas_call(
        _in_proj_kernel,
        grid=(m // tm, n // tn),
        in_specs=[
            pl.BlockSpec((tm, d), lambda i, c: (i, 0)),
            pl.BlockSpec((None, d, tn), lambda i, c: (l, 0, c)),
        ],
        out_specs=pl.BlockSpec((tm, tn), lambda i, c: (i, c)),
        out_shape=jax.ShapeDtypeStruct((m, n), F32),
        scratch_shapes=[pltpu.VMEM((tm, d), BF16)],
        compiler_params=_params("parallel", "arbitrary"),
        name="in_proj",
    )(x, w_in)


def _lane_chunks(a, op):
    out = a[:, :LANES]
    for c in range(LANES, a.shape[1], LANES):
        out = op(out, a[:, c:c + LANES])
    return out


def _moba_kernel(q_ref, k_ref, v_ref, o_ref, qa_ref, kb_ref, kat_ref, va_ref, s_ref, so_ref, *, nb, scale):
    i = pl.program_id(2)
    blk = MOBA_BLOCK
    grp = MOBA_GROUP
    dh = HEAD_DIM
    seq = nb * blk

    @pl.when(i == 0)
    def _():
        k = k_ref[...]
        lane = lax.broadcasted_iota(jnp.int32, (seq, LANES), 1)
        tile = lax.broadcasted_iota(jnp.int32, (nb, blk, LANES), 0).reshape(seq, LANES)
        kb_ref[...] = k.astype(BF16)
        kat_ref[0:dh, :] = k.T.astype(BF16)
        key_blk = lax.shift_right_logical(lax.broadcasted_iota(jnp.int32, (dh, seq), 1), blk.bit_length() - 1)
        blk_id = lax.broadcasted_iota(jnp.int32, (dh, seq), 0)
        kat_ref[dh:2 * dh, :] = jnp.where(key_blk == blk_id, 1.0, 0.0).astype(BF16)
        va_ref[:, 0:dh] = v_ref[...].astype(BF16)
        va_ref[:, dh:2 * dh] = jnp.ones((seq, dh), BF16)
        k_mean = jnp.mean(k.reshape(nb, blk, dh), axis=1)
        k_mean = jnp.concatenate([k_mean, jnp.zeros((LANES - nb, dh), F32)], axis=0)

        q = q_ref[...]
        gate = lax.dot_general(q.astype(BF16), k_mean.astype(BF16), NT_DIMS, preferred_element_type=F32)
        lane_f = lane.astype(F32)
        past = lane < tile
        g = jnp.where(past, gate, MASK_VALUE)
        sel = jnp.zeros((seq, LANES), F32)
        for _ in range(MOBA_TOPK):
            top = jnp.max(g, axis=-1, keepdims=True)
            first = jnp.min(jnp.where(g == top, lane_f, float(LANES)), axis=-1, keepdims=True)
            pick = lane_f == first
            sel = jnp.where(pick, 1.0, sel)
            g = jnp.where(pick, -jnp.inf, g)
        bias = jnp.where(past & (sel > 0.5), 0.0, MASK_VALUE)
        qa_ref[:, 0:dh] = (q * (scale * LOG2_E)).astype(BF16)
        qa_ref[:, dh:2 * dh] = bias.astype(BF16)

    own = pl.multiple_of(i * blk, blk)
    qa = qa_ref[pl.ds(own, blk), :]

    s = lax.dot_general(qa[:, 0:dh], kb_ref[pl.ds(own, blk), :], NT_DIMS, preferred_element_type=F32)
    row = lax.broadcasted_iota(jnp.int32, (blk, blk), 0)
    col = lax.broadcasted_iota(jnp.int32, (blk, blk), 1)
    s = jnp.where(col <= row, s, MASK_VALUE)
    so_ref[...] = s
    mrun0 = _lane_chunks(s, jnp.maximum)

    n_groups = (i + grp - 1) // grp
    span = grp * blk

    def attend(count):
        mrun = mrun0
        for n in range(count * grp):
            s = jnp.dot(qa, kat_ref[:, n * blk:(n + 1) * blk], preferred_element_type=F32)
            s_ref[n // grp, :, (n % grp) * blk:(n % grp + 1) * blk] = s
            mrun = jnp.maximum(mrun, _lane_chunks(s, jnp.maximum))
        m = jnp.max(mrun, axis=-1, keepdims=True)
        p = jnp.exp2((so_ref[...] - m).astype(BF16))
        accs = [jnp.dot(p, va_ref[pl.ds(own, blk), :], preferred_element_type=F32), None]
        for n in range(count * grp):
            p = jnp.exp2((s_ref[n // grp, :, (n % grp) * blk:(n % grp + 1) * blk] - m).astype(BF16))
            part = jnp.dot(p, va_ref[n * blk:(n + 1) * blk, :], preferred_element_type=F32)
            accs[n % 2] = part if accs[n % 2] is None else accs[n % 2] + part
        acc = accs[0] if accs[1] is None else accs[0] + accs[1]
        o_ref[...] = acc[:, 0:dh] / acc[:, dh:2 * dh]

    for count in range(nb // grp + 1):
        pl.when(n_groups == count)(functools.partial(attend, count))


def moba(p3):
    bsz, seq, _ = p3.shape
    blk = MOBA_BLOCK
    grp = MOBA_GROUP
    assert seq % (blk * grp) == 0
    nb = seq // blk
    assert nb <= LANES
    kernel = functools.partial(_moba_kernel, nb=nb, scale=HEAD_DIM ** -0.5)
    col = lambda c: pl.BlockSpec((None, seq, HEAD_DIM), lambda b, h, i: (b, 0, c * N_HEADS + h))
    return pl.pallas_call(
        kernel,
        grid=(bsz, N_HEADS, nb),
        in_specs=[col(0), col(1), col(2)],
        out_specs=pl.BlockSpec((None, blk, HEAD_DIM), lambda b, h, i: (b, i, h)),
        out_shape=jax.ShapeDtypeStruct((bsz, seq, D_GROUP), F32),
        scratch_shapes=[
            pltpu.VMEM((seq, 2 * HEAD_DIM), BF16),
            pltpu.VMEM((seq, HEAD_DIM), BF16),
            pltpu.VMEM((2 * HEAD_DIM, seq), BF16),
            pltpu.VMEM((seq, 2 * HEAD_DIM), BF16),
            pltpu.VMEM((nb // grp, blk, grp * blk), F32),
            pltpu.VMEM((blk, blk), F32),
        ],
        compiler_params=_params("parallel", "parallel", "arbitrary"),
        name="moba",
    )(p3, p3, p3)


def _rglru_kernel(gate_ref, x_ref, cw_ref, cb_ref, wa_ref, ba_ref, wx_ref, bx_ref, lam_ref,
                  o_ref, halo_ref, h_ref):
    t = pl.program_id(1)
    rows = x_ref.shape[0]

    @pl.when(t == 0)
    def _():
        halo_ref[...] = jnp.zeros(halo_ref.shape, F32)
        h_ref[...] = jnp.zeros(h_ref.shape, F32)

    x = x_ref[...]
    ext = jnp.concatenate([halo_ref[...], x], axis=0)
    halo_ref[...] = x[rows - SUBLANES:]
    xc = jnp.zeros_like(x) + cb_ref[...]
    for j in range(RG_CONV):
        xc = xc + cw_ref[j:j + 1, :] * _shift_rows(ext, RG_CONV - 1 - j, SUBLANES, rows)

    xcb = xc.astype(BF16)
    r_parts, i_parts = [], []
    for g in range(D_GROUP // HEAD_DIM):
        blk = xcb[:, g * HEAD_DIM:(g + 1) * HEAD_DIM]
        r_parts.append(jnp.dot(blk, wa_ref[g].astype(BF16), preferred_element_type=F32))
        i_parts.append(jnp.dot(blk, wx_ref[g].astype(BF16), preferred_element_type=F32))
    r = jax.nn.sigmoid(jnp.concatenate(r_parts, axis=-1) + ba_ref[...])
    ig = jax.nn.sigmoid(jnp.concatenate(i_parts, axis=-1) + bx_ref[...])
    neg_lam = -lam_ref[...]
    softplus = jnp.maximum(neg_lam, 0.0) + jnp.log1p(jnp.exp(-jnp.abs(neg_lam)))
    log_a = -RG_C * r * softplus
    a = jnp.exp(log_a)
    u = jnp.sqrt(jnp.maximum(1.0 - jnp.exp(2.0 * log_a), 0.0)) * (ig * xc)

    row = lax.broadcasted_iota(jnp.int32, a.shape, 0)
    big_a, big_b = a, u
    k = 1
    while k < rows:
        a_sh = jnp.where(row >= k, pltpu.roll(big_a, k, axis=0), 1.0)
        b_sh = jnp.where(row >= k, pltpu.roll(big_b, k, axis=0), 0.0)
        big_b = big_a * b_sh + big_b
        big_a = big_a * a_sh
        k *= 2
    h = big_a * h_ref[0:1, :] + big_b
    h_ref[...] = h[rows - SUBLANES:]
    h_ref[0:1, :] = h[rows - 1:rows]
    o_ref[...] = h * jax.nn.gelu(gate_ref[...], approximate=True)


def rglru(p3, conv_w, conv_b, w_a, b_a, w_x, b_x, lam, l, tt):
    bsz, seq, _ = p3.shape
    tt = min(tt, seq)
    assert seq % tt == 0
    c = D_GROUP
    vec = lambda: pl.BlockSpec((None, 1, c), lambda b, t: (l, 0, 0))
    return pl.pallas_call(
        _rglru_kernel,
        grid=(bsz, seq // tt),
        in_specs=[
            pl.BlockSpec((None, tt, c), lambda b, t: (b, t, 3)),
            pl.BlockSpec((None, tt, c), lambda b, t: (b, t, 4)),
            pl.BlockSpec((None, RG_CONV, c), lambda b, t: (l, 0, 0)),
            vec(),
            pl.BlockSpec((None, c // HEAD_DIM, HEAD_DIM, HEAD_DIM), lambda b, t: (l, 0, 0, 0)),
            vec(),
            pl.BlockSpec((None, c // HEAD_DIM, HEAD_DIM, HEAD_DIM), lambda b, t: (l, 0, 0, 0)),
            vec(),
            vec(),
        ],
        out_specs=pl.BlockSpec((None, tt, c), lambda b, t: (b, t, 0)),
        out_shape=jax.ShapeDtypeStruct((bsz, seq, c), F32),
        scratch_shapes=[pltpu.VMEM((SUBLANES, c), F32), pltpu.VMEM((SUBLANES, c), F32)],
        compiler_params=_params("parallel", "arbitrary"),
        name="rglru",
    )(p3, p3, conv_w, conv_b, w_a, b_a, w_x, b_x, lam)


def _conformer_kernel(val_ref, gate_ref, cw_ref, cb_ref, ng_ref, nb_ref, o_ref, halo_ref):
    t = pl.program_id(1)
    rows = val_ref.shape[0]

    @pl.when(t == 0)
    def _():
        halo_ref[...] = jnp.zeros(halo_ref.shape, F32)

    glu = val_ref[...] * jax.nn.sigmoid(gate_ref[...])
    ext = jnp.concatenate([halo_ref[...], glu], axis=0)
    halo_ref[...] = glu[rows - CV_HALO:]
    rolled = [ext] + [pltpu.roll(ext, r, axis=0) for r in range(1, SUBLANES)]
    u = jnp.zeros_like(glu) + cb_ref[...]
    for j in range(CV_WIDTH):
        d = CV_WIDTH - 1 - j
        r = d % SUBLANES
        start = CV_HALO - (d - r)
        u = u + cw_ref[j:j + 1, :] * rolled[r][start:start + rows]
    parts = []
    for g in range(D_GROUP // HEAD_DIM):
        parts.append(_standardize(u[:, g * HEAD_DIM:(g + 1) * HEAD_DIM]))
    un = jnp.concatenate(parts, axis=-1)
    o_ref[...] = _silu(un * ng_ref[...] + nb_ref[...])


def conformer(p3, cv_w, cv_b, ng, nb, l, tt):
    bsz, seq, _ = p3.shape
    tt = min(tt, seq)
    assert seq % tt == 0 and tt >= CV_HALO
    c = D_GROUP
    vec = lambda: pl.BlockSpec((None, 1, c), lambda b, t: (l, 0, 0))
    return pl.pallas_call(
        _conformer_kernel,
        grid=(bsz, seq // tt),
        in_specs=[
            pl.BlockSpec((None, tt, c), lambda b, t: (b, t, 5)),
            pl.BlockSpec((None, tt, c), lambda b, t: (b, t, 6)),
            pl.BlockSpec((None, CV_WIDTH, c), lambda b, t: (l, 0, 0)),
            vec(), vec(), vec(),
        ],
        out_specs=pl.BlockSpec((None, tt, c), lambda b, t: (b, t, 0)),
        out_shape=jax.ShapeDtypeStruct((bsz, seq, c), F32),
        scratch_shapes=[pltpu.VMEM((CV_HALO, c), F32)],
        compiler_params=_params("parallel", "arbitrary"),
        name="conformer",
    )(p3, p3, cv_w, cv_b, ng, nb)


def _hgrn2_chunk(q, kk, v, lf, state_t):
    c = HG_CHUNK
    sub = HG_SUB
    row = lax.broadcasted_iota(jnp.int32, (c, HEAD_DIM), 0)
    b = lf
    k = 1
    while k < c:
        b = b + jnp.where(row >= k, pltpu.roll(b, k, axis=0), 0.0)
        k *= 2
    vb = v.astype(BF16)

    o = lax.dot_general((q * jnp.exp(b)).astype(BF16), state_t.astype(BF16), NT_DIMS,
                        preferred_element_type=F32)

    lane_s = lax.broadcasted_iota(jnp.int32, (sub, c), 1)
    row_s = lax.broadcasted_iota(jnp.int32, (sub, c), 0)
    att_rows = []
    for blk in range(c // sub):
        lo = blk * sub
        q_i = q[lo:lo + sub]
        b_i = b[lo:lo + sub]
        k_i = kk[lo:lo + sub]
        att = jnp.zeros((sub, c), F32)
        if blk > 0:
            ref = b[lo - 1:lo]
            qs = (q_i * jnp.exp(b_i - ref)).astype(BF16)
            ks = kk[:lo] * jnp.exp(ref - b[:lo])
            ks = jnp.concatenate([ks, jnp.zeros((c - lo, HEAD_DIM), F32)], axis=0).astype(BF16)
            att = lax.dot_general(qs, ks, NT_DIMS, preferred_element_type=F32)
        for s in range(sub):
            decay = jnp.exp(jnp.minimum(b_i - b_i[s:s + 1], 0.0))
            w = jnp.sum(q_i * k_i[s:s + 1] * decay, axis=-1, keepdims=True)
            att = att + jnp.where((lane_s == lo + s) & (row_s >= s), w, 0.0)
        att_rows.append(att)
    att = jnp.concatenate(att_rows, axis=0)
    o = o + jnp.dot(att.astype(BF16), vb, preferred_element_type=F32)

    b_last = b[c - 1:c]
    ks = (kk * jnp.exp(b_last - b)).astype(BF16)
    new_state_t = state_t * jnp.exp(b_last) + lax.dot_general(vb, ks, TN_DIMS, preferred_element_type=F32)
    return o, new_state_t


def _hgrn2_kernel(q_ref, f_ref, v_ref, g_ref, lbp_ref, ng_ref, o_ref, state_ref, *, layer):
    t = pl.program_id(1)
    rows = q_ref.shape[0]

    @pl.when(t == 0)
    def _():
        state_ref[...] = jnp.zeros(state_ref.shape, F32)

    lbp = lbp_ref[...]
    e = jnp.exp(lbp - jnp.max(lbp, axis=0, keepdims=True))
    sm = e / jnp.sum(e, axis=0, keepdims=True)
    lb = jnp.sum(sm[0:layer + 1], axis=0, keepdims=True) - sm[0:1]

    for h in range(N_HEADS):
        cs = slice(h * HEAD_DIM, (h + 1) * HEAD_DIM)
        lb_h = lb[:, cs]
        state_t = state_ref[h]
        for ci in range(rows // HG_CHUNK):
            rs = slice(ci * HG_CHUNK, (ci + 1) * HG_CHUNK)
            sig = jax.nn.sigmoid(f_ref[rs, cs])
            lf = jnp.log(lb_h + (1.0 - lb_h) * sig)
            kk = (1.0 - lb_h) * (1.0 - sig)
            o, state_t = _hgrn2_chunk(q_ref[rs, cs], kk, v_ref[rs, cs], lf, state_t)
            o = o * lax.rsqrt(jnp.mean(o * o, axis=-1, keepdims=True) + LN_EPS)
            o_ref[rs, cs] = o * ng_ref[:, cs] * _silu(g_ref[rs, cs])
        state_ref[h] = state_t


def hgrn2(p3, lower_bound_params, norm_g, l, tt):
    bsz, seq, _ = p3.shape
    tt = min(tt, seq)
    assert seq % tt == 0 and tt % HG_CHUNK == 0
    c = D_GROUP
    depth = lower_bound_params.shape[0]
    tok = lambda col: pl.BlockSpec((None, tt, c), lambda b, t: (b, t, col))
    return pl.pallas_call(
        functools.partial(_hgrn2_kernel, layer=l),
        grid=(bsz, seq // tt),
        in_specs=[
            tok(7), tok(8), tok(9), tok(10),
            pl.BlockSpec((depth, c), lambda b, t: (0, 0)),
            pl.BlockSpec((None, 1, c), lambda b, t: (l, 0, 0)),
        ],
        out_specs=pl.BlockSpec((None, tt, c), lambda b, t: (b, t, 0)),
        out_shape=jax.ShapeDtypeStruct((bsz, seq, c), F32),
        scratch_shapes=[pltpu.VMEM((N_HEADS, HEAD_DIM, HEAD_DIM), F32)],
        compiler_params=_params("parallel", "arbitrary"),
        name="hgrn2",
    )(p3, p3, p3, p3, lower_bound_params, norm_g)


def _out_proj_ln_kernel(x_ref, ya_ref, yb_ref, yc_ref, yd_ref, w_ref, g_ref, b_ref, o_ref, *, alpha):
    c = D_GROUP
    y = jnp.dot(ya_ref[...].astype(BF16), w_ref[0:c, :], preferred_element_type=F32)
    y = y + jnp.dot(yb_ref[...].astype(BF16), w_ref[c:2 * c, :], preferred_element_type=F32)
    y = y + jnp.dot(yc_ref[...].astype(BF16), w_ref[2 * c:3 * c, :], preferred_element_type=F32)
    y = y + jnp.dot(yd_ref[...].astype(BF16), w_ref[3 * c:4 * c, :], preferred_element_type=F32)
    z = alpha * x_ref[...] + y
    o_ref[...] = _standardize(z) * g_ref[...] + b_ref[...]


def out_proj_ln(x, ya, yb, yc, yd, w_out, ln_g, ln_b, l, jn, alpha, tm):
    m, d = x.shape
    c = D_GROUP
    tm = min(tm, m)
    assert m % tm == 0
    mix = lambda: pl.BlockSpec((tm, c), lambda i: (i, 0))
    return pl.pallas_call(
        functools.partial(_out_proj_ln_kernel, alpha=alpha),
        grid=(m // tm,),
        in_specs=[
            pl.BlockSpec((tm, d), lambda i: (i, 0)),
            mix(), mix(), mix(), mix(),
            pl.BlockSpec((None, 4 * c, d), lambda i: (l, 0, 0)),
            pl.BlockSpec((None, 1, d), lambda i: (jn, 0, 0)),
            pl.BlockSpec((None, 1, d), lambda i: (jn, 0, 0)),
        ],
        out_specs=pl.BlockSpec((tm, d), lambda i: (i, 0)),
        out_shape=jax.ShapeDtypeStruct((m, d), F32),
        compiler_params=_params("parallel"),
        name="out_proj_ln",
    )(x, ya, yb, yc, yd, w_out, ln_g, ln_b)


def kernel(x, ln_g, ln_b, ffn_w_gate, ffn_w_up, ffn_w_down, w_in, w_out, rg_conv_w, rg_conv_b,
           rg_w_a, rg_b_a, rg_w_x, rg_b_x, rg_lambda, cv_w, cv_b, cv_ln_g, cv_ln_b,
           hg_lower_bounds, hg_norm_g):
    bsz, seq, d = x.shape
    depth = w_in.shape[0]
    alpha = (2 * depth) ** 0.25
    m = bsz * seq

    wg = ffn_w_gate.astype(BF16)
    wu = ffn_w_up.astype(BF16)
    wd = ffn_w_down.astype(BF16)
    w_in_b = w_in.astype(BF16)
    w_out_b = w_out.astype(BF16)
    ln_g3 = ln_g.reshape(depth * 3, 1, d)
    ln_b3 = ln_b.reshape(depth * 3, 1, d)
    row = lambda a: a.reshape(depth, 1, a.shape[-1])

    h = x.reshape(m, d)
    for l in range(depth):
        h = ffn_ln(h, wg, wu, wd, ln_g3, ln_b3, l, 0, 3 * l, alpha, tm=1024, tf=512)
        p3 = in_proj(h, w_in_b, l, tm=1024, tn=D_GROUP).reshape(bsz, seq, -1)
        y_a = moba(p3)
        y_b = rglru(p3, rg_conv_w, row(rg_conv_b), rg_w_a, row(rg_b_a), rg_w_x, row(rg_b_x),
                    row(rg_lambda), l, tt=256)
        y_c = conformer(p3, cv_w, row(cv_b), row(cv_ln_g), row(cv_ln_b), l, tt=256)
        y_d = hgrn2(p3, hg_lower_bounds, row(hg_norm_g), l, tt=256)
        flat = lambda y: y.reshape(m, D_GROUP)
        h = out_proj_ln(h, flat(y_a), flat(y_b), flat(y_c), flat(y_d), w_out_b, ln_g3, ln_b3,
                        l, 3 * l + 1, alpha, tm=512)
        h = ffn_ln(h, wg, wu, wd, ln_g3, ln_b3, l, 1, 3 * l + 2, alpha, tm=1024, tf=512)
    return h.reshape(bsz, seq, d)
```

```python
import functools

import jax
import jax.numpy as jnp
from jax import lax
from jax.experimental import pallas as pl
from jax.experimental.pallas import tpu as pltpu

F32 = jnp.float32
BF16 = jnp.bfloat16

LANES = 128
SUBLANES = 8
VMEM_LIMIT_BYTES = 56 * 1024 * 1024

D_GROUP = 512
HEAD_DIM = 128
N_HEADS = D_GROUP // HEAD_DIM
MOBA_BLOCK = 256
MOBA_TOPK = 3
MOBA_GROUP = 4
LOG2_E = 1.4426950408889634
MASK_VALUE = -1e30
RG_C = 8.0
RG_CONV = 4
CV_WIDTH = 31
CV_HALO = 32
IN_PROJ_COLS = 1408
OUT_PROJ_ROWS = 256
FFN_DOWN_COLS = 512
HG_CHUNK = 64
HG_SUB = 8
LN_EPS = 1e-5

NT_DIMS = (((1,), (1,)), ((), ()))
TN_DIMS = (((0,), (0,)), ((), ()))


def _params(*sem):
    return pltpu.CompilerParams(dimension_semantics=sem, vmem_limit_bytes=VMEM_LIMIT_BYTES)


def _standardize(z):
    mu = jnp.mean(z, axis=-1, keepdims=True)
    zc = z - mu
    var = jnp.mean(zc * zc, axis=-1, keepdims=True)
    return zc * lax.rsqrt(var + LN_EPS)


def _silu(z):
    return z * jax.nn.sigmoid(z)


def _shift_rows(ext, d, halo, rows):
    if d % SUBLANES == 0:
        return ext[halo - d:halo - d + rows]
    r = d % SUBLANES
    base = d - r
    rolled = pltpu.roll(ext, r, axis=0)
    return rolled[halo - base:halo - base + rows]


def _ffn_ln_kernel(x_ref, wg_ref, wu_ref, wd_ref, g_ref, b_ref, o_ref, xb_ref, *, alpha):
    f = pl.program_id(1)

    @pl.when(f == 0)
    def _():
        xb_ref[...] = x_ref[...].astype(BF16)
        o_ref[...] = jnp.zeros(o_ref.shape, F32)

    xb = xb_ref[...]
    hg = jnp.dot(xb, wg_ref[...], preferred_element_type=F32)
    hu = jnp.dot(xb, wu_ref[...], preferred_element_type=F32)
    h = (_silu(hg) * hu).astype(BF16)
    d = o_ref.shape[1]
    for c0 in range(0, d, FFN_DOWN_COLS):
        cs = slice(c0, min(c0 + FFN_DOWN_COLS, d))
        o_ref[:, cs] += jnp.dot(h, wd_ref[:, cs], preferred_element_type=F32)

    @pl.when(f == pl.num_programs(1) - 1)
    def _():
        z = alpha * x_ref[...] + 0.5 * o_ref[...]
        o_ref[...] = _standardize(z) * g_ref[...] + b_ref[...]


def ffn_ln(x, wg, wu, wd, ln_g, ln_b, l, j, jn, alpha, tm, tf):
    m, d = x.shape
    dff = wg.shape[-1]
    tm = min(tm, m)
    tf = min(tf, dff)
    assert m % tm == 0 and dff % tf == 0
    grid = (m // tm, dff // tf)
    return pl.pallas_call(
        functools.partial(_ffn_ln_kernel, alpha=alpha),
        grid=grid,
        in_specs=[
            pl.BlockSpec((tm, d), lambda i, f: (i, 0), pipeline_mode=pl.Buffered(1)),
            pl.BlockSpec((None, None, d, tf), lambda i, f: (l, j, 0, f)),
            pl.BlockSpec((None, None, d, tf), lambda i, f: (l, j, 0, f)),
            pl.BlockSpec((None, None, tf, d), lambda i, f: (l, j, f, 0)),
            pl.BlockSpec((None, 1, d), lambda i, f: (jn, 0, 0)),
            pl.BlockSpec((None, 1, d), lambda i, f: (jn, 0, 0)),
        ],
        out_specs=pl.BlockSpec((tm, d), lambda i, f: (i, 0)),
        out_shape=jax.ShapeDtypeStruct((m, d), F32),
        scratch_shapes=[pltpu.VMEM((tm, d), BF16)],
        compiler_params=_params("parallel", "arbitrary"),
        name="ffn_ln",
    )(x, wg, wu, wd, ln_g, ln_b)


def _in_proj_kernel(x_ref, w_ref, o_ref, xb_ref):
    @pl.when(pl.program_id(1) == 0)
    def _():
        xb_ref[...] = x_ref[...].astype(BF16)

    o_ref[...] = jnp.dot(xb_ref[...], w_ref[...], preferred_element_type=F32)


def in_proj(x, w_in, l, tm, tn):
    m, d = x.shape
    n = w_in.shape[-1]
    tm = min(tm, m)
    assert m % tm == 0 and n % tn == 0
    return pl.pallas_call(
        _in_proj_kernel,
        grid=(m // tm, n // tn),
        in_specs=[
            pl.BlockSpec((tm, d), lambda i, c: (i, 0)),
            pl.BlockSpec((None, d, tn), lambda i, c: (l, 0, c)),
        ],
        out_specs=pl.BlockSpec((tm, tn), lambda i, c: (i, c)),
        out_shape=jax.ShapeDtypeStruct((m, n), F32),
        scratch_shapes=[pltpu.VMEM((tm, d), BF16)],
        compiler_params=_params("parallel", "arbitrary"),
        name="in_proj",
    )(x, w_in)


def _lane_chunks(a, op):
    out = a[:, :LANES]
    for c in range(LANES, a.shape[1], LANES):
        out = op(out, a[:, c:c + LANES])
    return out


def _moba_kernel(q_ref, k_ref, v_ref, o_ref, qa_ref, kb_ref, kat_ref, va_ref, s_ref, so_ref, *, nb, scale):
    i = pl.program_id(2)
    blk = MOBA_BLOCK
    grp = MOBA_GROUP
    dh = HEAD_DIM
    seq = nb * blk

    @pl.when(i == 0)
    def _():
        k = k_ref[...]
        lane = lax.broadcasted_iota(jnp.int32, (seq, LANES), 1)
        tile = lax.broadcasted_iota(jnp.int32, (nb, blk, LANES), 0).reshape(seq, LANES)
        kb_ref[...] = k.astype(BF16)
        kat_ref[0:dh, :] = k.T.astype(BF16)
        key_blk = lax.shift_right_logical(lax.broadcasted_iota(jnp.int32, (dh, seq), 1), blk.bit_length() - 1)
        blk_id = lax.broadcasted_iota(jnp.int32, (dh, seq), 0)
        kat_ref[dh:2 * dh, :] = jnp.where(key_blk == blk_id, 1.0, 0.0).astype(BF16)
        va_ref[:, 0:dh] = v_ref[...].astype(BF16)
        va_ref[:, dh:2 * dh] = jnp.ones((seq, dh), BF16)
        k_mean = jnp.mean(k.reshape(nb, blk, dh), axis=1)
        k_mean = jnp.concatenate([k_mean, jnp.zeros((LANES - nb, dh), F32)], axis=0)

        q = q_ref[...]
        gate = lax.dot_general(q.astype(BF16), k_mean.astype(BF16), NT_DIMS, preferred_element_type=F32)
        lane_f = lane.astype(F32)
        past = lane < tile
        g = jnp.where(past, gate, MASK_VALUE)
        sel = jnp.zeros((seq, LANES), F32)
        for _ in range(MOBA_TOPK):
            top = jnp.max(g, axis=-1, keepdims=True)
            first = jnp.min(jnp.where(g == top, lane_f, float(LANES)), axis=-1, keepdims=True)
            pick = lane_f == first
            sel = jnp.where(pick, 1.0, sel)
            g = jnp.where(pick, -jnp.inf, g)
        bias = jnp.where(past & (sel > 0.5), 0.0, MASK_VALUE)
        qa_ref[:, 0:dh] = (q * (scale * LOG2_E)).astype(BF16)
        qa_ref[:, dh:2 * dh] = bias.astype(BF16)

    own = pl.multiple_of(i * blk, blk)
    qa = qa_ref[pl.ds(own, blk), :]

    s = lax.dot_general(qa[:, 0:dh], kb_ref[pl.ds(own, blk), :], NT_DIMS, preferred_element_type=F32)
    row = lax.broadcasted_iota(jnp.int32, (blk, blk), 0)
    col = lax.broadcasted_iota(jnp.int32, (blk, blk), 1)
    s = jnp.where(col <= row, s, MASK_VALUE)
    so_ref[...] = s
    mrun0 = _lane_chunks(s, jnp.maximum)

    n_groups = (i + grp - 1) // grp
    span = grp * blk

    def attend(count):
        mrun = mrun0
        for n in range(count * grp):
            s = jnp.dot(qa, kat_ref[:, n * blk:(n + 1) * blk], preferred_element_type=F32)
            s_ref[n // grp, :, (n % grp) * blk:(n % grp + 1) * blk] = s
            mrun = jnp.maximum(mrun, _lane_chunks(s, jnp.maximum))
        m = jnp.max(mrun, axis=-1, keepdims=True)
        p = jnp.exp2((so_ref[...] - m).astype(BF16))
        accs = [jnp.dot(p, va_ref[pl.ds(own, blk), :], preferred_element_type=F32), None]
        for n in range(count * grp):
            p = jnp.exp2((s_ref[n // grp, :, (n % grp) * blk:(n % grp + 1) * blk] - m).astype(BF16))
            part = jnp.dot(p, va_ref[n * blk:(n + 1) * blk, :], preferred_element_type=F32)
            accs[n % 2] = part if accs[n % 2] is None else accs[n % 2] + part
        acc = accs[0] if accs[1] is None else accs[0] + accs[1]
        o_ref[...] = acc[:, 0:dh] / acc[:, dh:2 * dh]

    for count in range(nb // grp + 1):
        pl.when(n_groups == count)(functools.partial(attend, count))


def moba(p3):
    bsz, seq, _ = p3.shape
    blk = MOBA_BLOCK
    grp = MOBA_GROUP
    assert seq % (blk * grp) == 0
    nb = seq // blk
    assert nb <= LANES
    kernel = functools.partial(_moba_kernel, nb=nb, scale=HEAD_DIM ** -0.5)
    col = lambda c: pl.BlockSpec((None, seq, HEAD_DIM), lambda b, h, i: (b, 0, c * N_HEADS + h))
    return pl.pallas_call(
        kernel,
        grid=(bsz, N_HEADS, nb),
        in_specs=[col(0), col(1), col(2)],
        out_specs=pl.BlockSpec((None, blk, HEAD_DIM), lambda b, h, i: (b, i, h)),
        out_shape=jax.ShapeDtypeStruct((bsz, seq, D_GROUP), F32),
        scratch_shapes=[
            pltpu.VMEM((seq, 2 * HEAD_DIM), BF16),
            pltpu.VMEM((seq, HEAD_DIM), BF16),
            pltpu.VMEM((2 * HEAD_DIM, seq), BF16),
            pltpu.VMEM((seq, 2 * HEAD_DIM), BF16),
            pltpu.VMEM((nb // grp, blk, grp * blk), F32),
            pltpu.VMEM((blk, blk), F32),
        ],
        compiler_params=_params("parallel", "parallel", "arbitrary"),
        name="moba",
    )(p3, p3, p3)


def _rglru_kernel(gate_ref, x_ref, cw_ref, cb_ref, wa_ref, ba_ref, wx_ref, bx_ref, lam_ref,
                  o_ref, halo_ref, h_ref):
    t = pl.program_id(1)
    rows = x_ref.shape[0]

    @pl.when(t == 0)
    def _():
        halo_ref[...] = jnp.zeros(halo_ref.shape, F32)
        h_ref[...] = jnp.zeros(h_ref.shape, F32)

    x = x_ref[...]
    ext = jnp.concatenate([halo_ref[...], x], axis=0)
    halo_ref[...] = x[rows - SUBLANES:]
    xc = jnp.zeros_like(x) + cb_ref[...]
    for j in range(RG_CONV):
        xc = xc + cw_ref[j:j + 1, :] * _shift_rows(ext, RG_CONV - 1 - j, SUBLANES, rows)

    xcb = xc.astype(BF16)
    r_parts, i_parts = [], []
    for g in range(D_GROUP // HEAD_DIM):
        blk = xcb[:, g * HEAD_DIM:(g + 1) * HEAD_DIM]
        r_parts.append(jnp.dot(blk, wa_ref[g].astype(BF16), preferred_element_type=F32))
        i_parts.append(jnp.dot(blk, wx_ref[g].astype(BF16), preferred_element_type=F32))
    r = jax.nn.sigmoid(jnp.concatenate(r_parts, axis=-1) + ba_ref[...])
    ig = jax.nn.sigmoid(jnp.concatenate(i_parts, axis=-1) + bx_ref[...])
    neg_lam = -lam_ref[...]
    softplus = jnp.maximum(neg_lam, 0.0) + jnp.log1p(jnp.exp(-jnp.abs(neg_lam)))
    log_a = -RG_C * r * softplus
    a = jnp.exp(log_a)
    u = jnp.sqrt(jnp.maximum(1.0 - jnp.exp(2.0 * log_a), 0.0)) * (ig * xc)

    row = lax.broadcasted_iota(jnp.int32, a.shape, 0)
    big_a, big_b = a, u
    k = 1
    while k < rows:
        a_sh = jnp.where(row >= k, pltpu.roll(big_a, k, axis=0), 1.0)
        b_sh = jnp.where(row >= k, pltpu.roll(big_b, k, axis=0), 0.0)
        big_b = big_a * b_sh + big_b
        big_a = big_a * a_sh
        k *= 2
    h = big_a * h_ref[0:1, :] + big_b
    h_ref[...] = h[rows - SUBLANES:]
    h_ref[0:1, :] = h[rows - 1:rows]
    o_ref[...] = h * jax.nn.gelu(gate_ref[...], approximate=True)


def rglru(p3, conv_w, conv_b, w_a, b_a, w_x, b_x, lam, l, tt):
    bsz, seq, _ = p3.shape
    tt = min(tt, seq)
    assert seq % tt == 0
    c = D_GROUP
    vec = lambda: pl.BlockSpec((None, 1, c), lambda b, t: (l, 0, 0))
    return pl.pallas_call(
        _rglru_kernel,
        grid=(bsz, seq // tt),
        in_specs=[
            pl.BlockSpec((None, tt, c), lambda b, t: (b, t, 3)),
            pl.BlockSpec((None, tt, c), lambda b, t: (b, t, 4)),
            pl.BlockSpec((None, RG_CONV, c), lambda b, t: (l, 0, 0)),
            vec(),
            pl.BlockSpec((None, c // HEAD_DIM, HEAD_DIM, HEAD_DIM), lambda b, t: (l, 0, 0, 0)),
            vec(),
            pl.BlockSpec((None, c // HEAD_DIM, HEAD_DIM, HEAD_DIM), lambda b, t: (l, 0, 0, 0)),
            vec(),
            vec(),
        ],
        out_specs=pl.BlockSpec((None, tt, c), lambda b, t: (b, t, 0)),
        out_shape=jax.ShapeDtypeStruct((bsz, seq, c), F32),
        scratch_shapes=[pltpu.VMEM((SUBLANES, c), F32), pltpu.VMEM((SUBLANES, c), F32)],
        compiler_params=_params("parallel", "arbitrary"),
        name="rglru",
    )(p3, p3, conv_w, conv_b, w_a, b_a, w_x, b_x, lam)


def _conformer_kernel(val_ref, gate_ref, cw_ref, cb_ref, ng_ref, nb_ref, o_ref, halo_ref):
    t = pl.program_id(1)
    rows = val_ref.shape[0]

    @pl.when(t == 0)
    def _():
        halo_ref[...] = jnp.zeros(halo_ref.shape, F32)

    glu = val_ref[...] * jax.nn.sigmoid(gate_ref[...])
    ext = jnp.concatenate([halo_ref[...], glu], axis=0)
    halo_ref[...] = glu[rows - CV_HALO:]
    rolled = [ext] + [pltpu.roll(ext, r, axis=0) for r in range(1, SUBLANES)]
    u = jnp.zeros_like(glu) + cb_ref[...]
    for j in range(CV_WIDTH):
        d = CV_WIDTH - 1 - j
        r = d % SUBLANES
        start = CV_HALO - (d - r)
        u = u + cw_ref[j:j + 1, :] * rolled[r][start:start + rows]
    parts = []
    for g in range(D_GROUP // HEAD_DIM):
        parts.append(_standardize(u[:, g * HEAD_DIM:(g + 1) * HEAD_DIM]))
    un = jnp.concatenate(parts, axis=-1)
    o_ref[...] = _silu(un * ng_ref[...] + nb_ref[...])


def conformer(p3, cv_w, cv_b, ng, nb, l, tt):
    bsz, seq, _ = p3.shape
    tt = min(tt, seq)
    assert seq % tt == 0 and tt >= CV_HALO
    c = D_GROUP
    vec = lambda: pl.BlockSpec((None, 1, c), lambda b, t: (l, 0, 0))
    return pl.pallas_call(
        _conformer_kernel,
        grid=(bsz, seq // tt),
        in_specs=[
            pl.BlockSpec((None, tt, c), lambda b, t: (b, t, 5)),
            pl.BlockSpec((None, tt, c), lambda b, t: (b, t, 6)),
            pl.BlockSpec((None, CV_WIDTH, c), lambda b, t: (l, 0, 0)),
            vec(), vec(), vec(),
        ],
        out_specs=pl.BlockSpec((None, tt, c), lambda b, t: (b, t, 0)),
        out_shape=jax.ShapeDtypeStruct((bsz, seq, c), F32),
        scratch_shapes=[pltpu.VMEM((CV_HALO, c), F32)],
        compiler_params=_params("parallel", "arbitrary"),
        name="conformer",
    )(p3, p3, cv_w, cv_b, ng, nb)


def _hgrn2_chunk(q, kk, v, lf, state_t):
    c = HG_CHUNK
    sub = HG_SUB
    row = lax.broadcasted_iota(jnp.int32, (c, HEAD_DIM), 0)
    b = lf
    k = 1
    while k < c:
        b = b + jnp.where(row >= k, pltpu.roll(b, k, axis=0), 0.0)
        k *= 2
    vb = v.astype(BF16)

    o = lax.dot_general((q * jnp.exp(b)).astype(BF16), state_t.astype(BF16), NT_DIMS,
                        preferred_element_type=F32)

    lane_s = lax.broadcasted_iota(jnp.int32, (sub, c), 1)
    row_s = lax.broadcasted_iota(jnp.int32, (sub, c), 0)
    att_rows = []
    for blk in range(c // sub):
        lo = blk * sub
        q_i = q[lo:lo + sub]
        b_i = b[lo:lo + sub]
        k_i = kk[lo:lo + sub]
        att = jnp.zeros((sub, c), F32)
        if blk > 0:
            ref = b[lo - 1:lo]
            qs = (q_i * jnp.exp(b_i - ref)).astype(BF16)
            ks = kk[:lo] * jnp.exp(ref - b[:lo])
            ks = jnp.concatenate([ks, jnp.zeros((c - lo, HEAD_DIM), F32)], axis=0).astype(BF16)
            att = lax.dot_general(qs, ks, NT_DIMS, preferred_element_type=F32)
        for s in range(sub):
            decay = jnp.exp(jnp.minimum(b_i - b_i[s:s + 1], 0.0))
            w = jnp.sum(q_i * k_i[s:s + 1] * decay, axis=-1, keepdims=True)
            att = att + jnp.where((lane_s == lo + s) & (row_s >= s), w, 0.0)
        att_rows.append(att)
    att = jnp.concatenate(att_rows, axis=0)
    o = o + jnp.dot(att.astype(BF16), vb, preferred_element_type=F32)

    b_last = b[c - 1:c]
    ks = (kk * jnp.exp(b_last - b)).astype(BF16)
    new_state_t = state_t * jnp.exp(b_last) + lax.dot_general(vb, ks, TN_DIMS, preferred_element_type=F32)
    return o, new_state_t


def _hgrn2_kernel(q_ref, f_ref, v_ref, g_ref, lbp_ref, ng_ref, o_ref, state_ref, *, layer):
    t = pl.program_id(1)
    rows = q_ref.shape[0]

    @pl.when(t == 0)
    def _():
        state_ref[...] = jnp.zeros(state_ref.shape, F32)

    lbp = lbp_ref[...]
    e = jnp.exp(lbp - jnp.max(lbp, axis=0, keepdims=True))
    sm = e / jnp.sum(e, axis=0, keepdims=True)
    lb = jnp.sum(sm[0:layer + 1], axis=0, keepdims=True) - sm[0:1]

    for h in range(N_HEADS):
        cs = slice(h * HEAD_DIM, (h + 1) * HEAD_DIM)
        lb_h = lb[:, cs]
        state_t = state_ref[h]
        for ci in range(rows // HG_CHUNK):
            rs = slice(ci * HG_CHUNK, (ci + 1) * HG_CHUNK)
            sig = jax.nn.sigmoid(f_ref[rs, cs])
            lf = jnp.log(lb_h + (1.0 - lb_h) * sig)
            kk = (1.0 - lb_h) * (1.0 - sig)
            o, state_t = _hgrn2_chunk(q_ref[rs, cs], kk, v_ref[rs, cs], lf, state_t)
            o = o * lax.rsqrt(jnp.mean(o * o, axis=-1, keepdims=True) + LN_EPS)
            o_ref[rs, cs] = o * ng_ref[:, cs] * _silu(g_ref[rs, cs])
        state_ref[h] = state_t


def hgrn2(p3, lower_bound_params, norm_g, l, tt):
    bsz, seq, _ = p3.shape
    tt = min(tt, seq)
    assert seq % tt == 0 and tt % HG_CHUNK == 0
    c = D_GROUP
    depth = lower_bound_params.shape[0]
    tok = lambda col: pl.BlockSpec((None, tt, c), lambda b, t: (b, t, col))
    return pl.pallas_call(
        functools.partial(_hgrn2_kernel, layer=l),
        grid=(bsz, seq // tt),
        in_specs=[
            tok(7), tok(8), tok(9), tok(10),
            pl.BlockSpec((depth, c), lambda b, t: (0, 0)),
            pl.BlockSpec((None, 1, c), lambda b, t: (l, 0, 0)),
        ],
        out_specs=pl.BlockSpec((None, tt, c), lambda b, t: (b, t, 0)),
        out_shape=jax.ShapeDtypeStruct((bsz, seq, c), F32),
        scratch_shapes=[pltpu.VMEM((N_HEADS, HEAD_DIM, HEAD_DIM), F32)],
        compiler_params=_params("parallel", "arbitrary"),
        name="hgrn2",
    )(p3, p3, p3, p3, lower_bound_params, norm_g)


def _out_proj_ln_kernel(x_ref, ya_ref, yb_ref, yc_ref, yd_ref, w_ref, g_ref, b_ref, o_ref, *, alpha):
    c = D_GROUP
    rows = x_ref.shape[0]
    step = min(rows, OUT_PROJ_ROWS)
    for r0 in range(0, rows, step):
        rs = slice(r0, r0 + step)
        y = jnp.dot(ya_ref[rs, :].astype(BF16), w_ref[0:c, :], preferred_element_type=F32)
        y = y + jnp.dot(yb_ref[rs, :].astype(BF16), w_ref[c:2 * c, :], preferred_element_type=F32)
        y = y + jnp.dot(yc_ref[rs, :].astype(BF16), w_ref[2 * c:3 * c, :], preferred_element_type=F32)
        y = y + jnp.dot(yd_ref[rs, :].astype(BF16), w_ref[3 * c:4 * c, :], preferred_element_type=F32)
        z = alpha * x_ref[rs, :] + y
        o_ref[rs, :] = _standardize(z) * g_ref[...] + b_ref[...]


def out_proj_ln(x, ya, yb, yc, yd, w_out, ln_g, ln_b, l, jn, alpha, tm):
    m, d = x.shape
    c = D_GROUP
    tm = min(tm, m)
    assert m % tm == 0
    mix = lambda: pl.BlockSpec((tm, c), lambda i: (i, 0))
    return pl.pallas_call(
        functools.partial(_out_proj_ln_kernel, alpha=alpha),
        grid=(m // tm,),
        in_specs=[
            pl.BlockSpec((tm, d), lambda i: (i, 0)),
            mix(), mix(), mix(), mix(),
            pl.BlockSpec((None, 4 * c, d), lambda i: (l, 0, 0)),
            pl.BlockSpec((None, 1, d), lambda i: (jn, 0, 0)),
            pl.BlockSpec((None, 1, d), lambda i: (jn, 0, 0)),
        ],
        out_specs=pl.BlockSpec((tm, d), lambda i: (i, 0)),
        out_shape=jax.ShapeDtypeStruct((m, d), F32),
        compiler_params=_params("parallel"),
        name="out_proj_ln",
    )(x, ya, yb, yc, yd, w_out, ln_g, ln_b)


def kernel(x, ln_g, ln_b, ffn_w_gate, ffn_w_up, ffn_w_down, w_in, w_out, rg_conv_w, rg_conv_b,
           rg_w_a, rg_b_a, rg_w_x, rg_b_x, rg_lambda, cv_w, cv_b, cv_ln_g, cv_ln_b,
           hg_lower_bounds, hg_norm_g):
    bsz, seq, d = x.shape
    depth = w_in.shape[0]
    alpha = (2 * depth) ** 0.25
    m = bsz * seq

    wg = ffn_w_gate.astype(BF16)
    wu = ffn_w_up.astype(BF16)
    wd = ffn_w_down.astype(BF16)
    w_in_b = w_in.astype(BF16)
    w_out_b = w_out.astype(BF16)
    ln_g3 = ln_g.reshape(depth * 3, 1, d)
    ln_b3 = ln_b.reshape(depth * 3, 1, d)
    row = lambda a: a.reshape(depth, 1, a.shape[-1])

    h = x.reshape(m, d)
    for l in range(depth):
        h = ffn_ln(h, wg, wu, wd, ln_g3, ln_b3, l, 0, 3 * l, alpha, tm=1024, tf=512)
        p3 = in_proj(h, w_in_b, l, tm=1024, tn=IN_PROJ_COLS).reshape(bsz, seq, -1)
        y_a = moba(p3)
        y_b = rglru(p3, rg_conv_w, row(rg_conv_b), rg_w_a, row(rg_b_a), rg_w_x, row(rg_b_x),
                    row(rg_lambda), l, tt=256)
        y_c = conformer(p3, cv_w, row(cv_b), row(cv_ln_g), row(cv_ln_b), l, tt=512)
        y_d = hgrn2(p3, hg_lower_bounds, row(hg_norm_g), l, tt=256)
        flat = lambda y: y.reshape(m, D_GROUP)
        h = out_proj_ln(h, flat(y_a), flat(y_b), flat(y_c), flat(y_d), w_out_b, ln_g3, ln_b3,
                        l, 3 * l + 1, alpha, tm=512)
        h = ffn_ln(h, wg, wu, wd, ln_g3, ln_b3, l, 1, 3 * l + 2, alpha, tm=1024, tf=512)
    return h.reshape(bsz, seq, d)
```

```python
import functools

import jax
import jax.numpy as jnp
from jax import lax
from jax.experimental import pallas as pl
from jax.experimental.pallas import tpu as pltpu

F32 = jnp.float32
BF16 = jnp.bfloat16

LANES = 128
SUBLANES = 8
VMEM_LIMIT_BYTES = 56 * 1024 * 1024

D_GROUP = 512
HEAD_DIM = 128
N_HEADS = D_GROUP // HEAD_DIM
MOBA_BLOCK = 256
MOBA_TOPK = 3
MOBA_GROUP = 4
MOBA_Q_BLOCKS = 2
LOG2_E = 1.4426950408889634
MASK_VALUE = -1e30
RG_C = 8.0
RG_CONV = 4
CV_WIDTH = 31
CV_HALO = 32
IN_PROJ_COLS = 1408
OUT_PROJ_ROWS = 256
FFN_DOWN_COLS = 512
HG_CHUNK = 64
HG_SUB = 8
LN_EPS = 1e-5

NT_DIMS = (((1,), (1,)), ((), ()))
TN_DIMS = (((0,), (0,)), ((), ()))


def _params(*sem):
    return pltpu.CompilerParams(dimension_semantics=sem, vmem_limit_bytes=VMEM_LIMIT_BYTES)


def _standardize(z):
    mu = jnp.mean(z, axis=-1, keepdims=True)
    zc = z - mu
    var = jnp.mean(zc * zc, axis=-1, keepdims=True)
    return zc * lax.rsqrt(var + LN_EPS)


def _silu(z):
    return z * jax.nn.sigmoid(z)


def _shift_rows(ext, d, halo, rows):
    if d % SUBLANES == 0:
        return ext[halo - d:halo - d + rows]
    r = d % SUBLANES
    base = d - r
    rolled = pltpu.roll(ext, r, axis=0)
    return rolled[halo - base:halo - base + rows]


def _ffn_ln_kernel(x_ref, wg_ref, wu_ref, wd_ref, g_ref, b_ref, o_ref, xb_ref, *, alpha):
    f = pl.program_id(1)

    @pl.when(f == 0)
    def _():
        xb_ref[...] = x_ref[...].astype(BF16)
        o_ref[...] = jnp.zeros(o_ref.shape, F32)

    xb = xb_ref[...]
    hg = jnp.dot(xb, wg_ref[...], preferred_element_type=F32)
    hu = jnp.dot(xb, wu_ref[...], preferred_element_type=F32)
    h = (_silu(hg) * hu).astype(BF16)
    d = o_ref.shape[1]
    for c0 in range(0, d, FFN_DOWN_COLS):
        cs = slice(c0, min(c0 + FFN_DOWN_COLS, d))
        o_ref[:, cs] += jnp.dot(h, wd_ref[:, cs], preferred_element_type=F32)

    @pl.when(f == pl.num_programs(1) - 1)
    def _():
        z = alpha * x_ref[...] + 0.5 * o_ref[...]
        o_ref[...] = _standardize(z) * g_ref[...] + b_ref[...]


def ffn_ln(x, wg, wu, wd, ln_g, ln_b, l, j, jn, alpha, tm, tf):
    m, d = x.shape
    dff = wg.shape[-1]
    tm = min(tm, m)
    tf = min(tf, dff)
    assert m % tm == 0 and dff % tf == 0
    grid = (m // tm, dff // tf)
    return pl.pallas_call(
        functools.partial(_ffn_ln_kernel, alpha=alpha),
        grid=grid,
        in_specs=[
            pl.BlockSpec((tm, d), lambda i, f: (i, 0), pipeline_mode=pl.Buffered(1)),
            pl.BlockSpec((None, None, d, tf), lambda i, f: (l, j, 0, f)),
            pl.BlockSpec((None, None, d, tf), lambda i, f: (l, j, 0, f)),
            pl.BlockSpec((None, None, tf, d), lambda i, f: (l, j, f, 0)),
            pl.BlockSpec((None, 1, d), lambda i, f: (jn, 0, 0)),
            pl.BlockSpec((None, 1, d), lambda i, f: (jn, 0, 0)),
        ],
        out_specs=pl.BlockSpec((tm, d), lambda i, f: (i, 0)),
        out_shape=jax.ShapeDtypeStruct((m, d), F32),
        scratch_shapes=[pltpu.VMEM((tm, d), BF16)],
        compiler_params=_params("parallel", "arbitrary"),
        name="ffn_ln",
    )(x, wg, wu, wd, ln_g, ln_b)


def _in_proj_kernel(x_ref, w_ref, o_ref, xb_ref):
    @pl.when(pl.program_id(1) == 0)
    def _():
        xb_ref[...] = x_ref[...].astype(BF16)

    o_ref[...] = jnp.dot(xb_ref[...], w_ref[...], preferred_element_type=F32)


def in_proj(x, w_in, l, tm, tn):
    m, d = x.shape
    n = w_in.shape[-1]
    tm = min(tm, m)
    assert m % tm == 0 and n % tn == 0
    return pl.pallas_call(
        _in_proj_kernel,
        grid=(m // tm, n // tn),
        in_specs=[
            pl.BlockSpec((tm, d), lambda i, c: (i, 0)),
            pl.BlockSpec((None, d, tn), lambda i, c: (l, 0, c)),
        ],
        out_specs=pl.BlockSpec((tm, tn), lambda i, c: (i, c)),
        out_shape=jax.ShapeDtypeStruct((m, n), F32),
        scratch_shapes=[pltpu.VMEM((tm, d), BF16)],
        compiler_params=_params("parallel", "arbitrary"),
        name="in_proj",
    )(x, w_in)


def _fold_rows_max(a):
    rows, cols = a.shape
    return jnp.max(a.reshape(rows // SUBLANES, SUBLANES, cols), axis=0)


def _moba_kernel(q_ref, k_ref, v_ref, o_ref, qat_ref, ka_ref, vat_ref, vat3_ref, s_ref, so_ref, *, nb, scale):
    j = pl.program_id(2)
    blk = MOBA_BLOCK
    grp = MOBA_GROUP
    dh = HEAD_DIM
    seq = nb * blk
    tq = MOBA_Q_BLOCKS * blk
    span = grp * blk
    nbp = -(-nb // SUBLANES) * SUBLANES

    @pl.when(j == 0)
    def _():
        k = k_ref[...]
        lane = lax.broadcasted_iota(jnp.int32, (seq, LANES), 1)
        tile = lax.broadcasted_iota(jnp.int32, (nb, blk, LANES), 0).reshape(seq, LANES)
        ka_ref[:, 0:dh] = k.astype(BF16)
        ka_ref[:, dh:2 * dh] = jnp.where(tile == lane, 1.0, 0.0).astype(BF16)
        vat = jnp.concatenate([v_ref[...].T, jnp.ones((dh, seq), F32)], axis=0).astype(BF16)
        vat_ref[...] = vat
        for n in range(nb):
            vat3_ref[n] = vat[:, n * blk:(n + 1) * blk]
        k_mean = jnp.mean(k.reshape(nb, blk, dh), axis=1)
        k_mean = jnp.concatenate([k_mean, jnp.zeros((LANES - nb, dh), F32)], axis=0)

        q = q_ref[...]
        gate = lax.dot_general(k_mean.astype(BF16), q.astype(BF16), NT_DIMS, preferred_element_type=F32)
        g = gate[0:nbp]
        blk_row = lax.broadcasted_iota(jnp.int32, (nbp, seq), 0)
        q_blk = lax.shift_right_logical(lax.broadcasted_iota(jnp.int32, (nbp, seq), 1), blk.bit_length() - 1)
        row_f = blk_row.astype(F32)
        past = blk_row < q_blk
        g = jnp.where(past, g, MASK_VALUE)
        sel = jnp.zeros((nbp, seq), F32)
        for _ in range(MOBA_TOPK):
            top = jnp.max(g, axis=0, keepdims=True)
            first = jnp.min(jnp.where(g == top, row_f, float(LANES)), axis=0, keepdims=True)
            pick = row_f == first
            sel = jnp.where(pick, 1.0, sel)
            g = jnp.where(pick, -jnp.inf, g)
        bias = jnp.where(past & (sel > 0.5), 0.0, MASK_VALUE)
        qt = (q * (scale * LOG2_E)).T
        qat = jnp.concatenate([qt, bias, jnp.zeros((dh - nbp, seq), F32)], axis=0).astype(BF16)
        for t in range(seq // tq):
            qat_ref[t] = qat[:, t * tq:(t + 1) * tq]

    qat = qat_ref[j]
    first_blk = j * MOBA_Q_BLOCKS

    row = lax.broadcasted_iota(jnp.int32, (blk, blk), 0)
    col = lax.broadcasted_iota(jnp.int32, (blk, blk), 1)
    own_max = []
    for c in range(MOBA_Q_BLOCKS):
        start = pl.multiple_of((first_blk + c) * blk, blk)
        s = jnp.dot(ka_ref[pl.ds(start, blk), 0:dh], qat[0:dh, c * blk:(c + 1) * blk],
                    preferred_element_type=F32)
        s = jnp.where(row <= col, s, MASK_VALUE)
        so_ref[c] = s
        own_max.append(_fold_rows_max(s))
    mrun0 = jnp.concatenate(own_max, axis=1)

    dense_blocks = first_blk + MOBA_Q_BLOCKS - 1
    n_groups = (dense_blocks + grp - 1) // grp

    def attend(count):
        mrun = mrun0
        for g in range(count):
            s = jnp.dot(ka_ref[g * span:(g + 1) * span, :], qat, preferred_element_type=F32)
            s_ref[g] = s
            mrun = jnp.maximum(mrun, _fold_rows_max(s))
        m = jnp.max(mrun, axis=0, keepdims=True)
        cols = []
        for c in range(MOBA_Q_BLOCKS):
            p = jnp.exp2((so_ref[c] - m[:, c * blk:(c + 1) * blk]).astype(BF16))
            cols.append(jnp.dot(vat3_ref[first_blk + c], p, preferred_element_type=F32))
        acc = jnp.concatenate(cols, axis=1)
        for g in range(count):
            p = jnp.exp2((s_ref[g] - m).astype(BF16))
            acc = acc + jnp.dot(vat_ref[:, g * span:(g + 1) * span], p, preferred_element_type=F32)
        o_ref[...] = (acc[0:dh] / acc[dh:2 * dh]).T

    min_count = (MOBA_Q_BLOCKS - 1 + grp - 1) // grp
    max_count = (nb - 1 + grp - 1) // grp
    for count in range(min_count, max_count + 1):
        pl.when(n_groups == count)(functools.partial(attend, count))


def moba(p3):
    bsz, seq, _ = p3.shape
    blk = MOBA_BLOCK
    grp = MOBA_GROUP
    tq = MOBA_Q_BLOCKS * blk
    assert seq % (blk * grp) == 0 and seq % tq == 0
    nb = seq // blk
    assert nb <= HEAD_DIM
    max_count = (nb - 1 + grp - 1) // grp
    kernel = functools.partial(_moba_kernel, nb=nb, scale=HEAD_DIM ** -0.5)
    col = lambda c: pl.BlockSpec((None, seq, HEAD_DIM), lambda b, h, j: (b, 0, c * N_HEADS + h))
    return pl.pallas_call(
        kernel,
        grid=(bsz, N_HEADS, seq // tq),
        in_specs=[col(0), col(1), col(2)],
        out_specs=pl.BlockSpec((None, tq, HEAD_DIM), lambda b, h, j: (b, j, h)),
        out_shape=jax.ShapeDtypeStruct((bsz, seq, D_GROUP), F32),
        scratch_shapes=[
            pltpu.VMEM((seq // tq, 2 * HEAD_DIM, tq), BF16),
            pltpu.VMEM((seq, 2 * HEAD_DIM), BF16),
            pltpu.VMEM((2 * HEAD_DIM, seq), BF16),
            pltpu.VMEM((nb, 2 * HEAD_DIM, blk), BF16),
            pltpu.VMEM((max_count, grp * blk, tq), F32),
            pltpu.VMEM((MOBA_Q_BLOCKS, blk, blk), F32),
        ],
        compiler_params=_params("parallel", "parallel", "arbitrary"),
        name="moba",
    )(p3, p3, p3)


def _rglru_kernel(gate_ref, x_ref, cw_ref, cb_ref, wa_ref, ba_ref, wx_ref, bx_ref, lam_ref,
                  o_ref, halo_ref, h_ref):
    t = pl.program_id(1)
    rows = x_ref.shape[0]

    @pl.when(t == 0)
    def _():
        halo_ref[...] = jnp.zeros(halo_ref.shape, F32)
        h_ref[...] = jnp.zeros(h_ref.shape, F32)

    x = x_ref[...]
    ext = jnp.concatenate([halo_ref[...], x], axis=0)
    halo_ref[...] = x[rows - SUBLANES:]
    xc = jnp.zeros_like(x) + cb_ref[...]
    for j in range(RG_CONV):
        xc = xc + cw_ref[j:j + 1, :] * _shift_rows(ext, RG_CONV - 1 - j, SUBLANES, rows)

    xcb = xc.astype(BF16)
    r_parts, i_parts = [], []
    for g in range(D_GROUP // HEAD_DIM):
        blk = xcb[:, g * HEAD_DIM:(g + 1) * HEAD_DIM]
        r_parts.append(jnp.dot(blk, wa_ref[g].astype(BF16), preferred_element_type=F32))
        i_parts.append(jnp.dot(blk, wx_ref[g].astype(BF16), preferred_element_type=F32))
    r = jax.nn.sigmoid(jnp.concatenate(r_parts, axis=-1) + ba_ref[...])
    ig = jax.nn.sigmoid(jnp.concatenate(i_parts, axis=-1) + bx_ref[...])
    neg_lam = -lam_ref[...]
    softplus = jnp.maximum(neg_lam, 0.0) + jnp.log1p(jnp.exp(-jnp.abs(neg_lam)))
    log_a = -RG_C * r * softplus
    a = jnp.exp(log_a)
    u = jnp.sqrt(jnp.maximum(1.0 - jnp.exp(2.0 * log_a), 0.0)) * (ig * xc)

    row = lax.broadcasted_iota(jnp.int32, a.shape, 0)
    big_a, big_b = a, u
    k = 1
    while k < rows:
        a_sh = jnp.where(row >= k, pltpu.roll(big_a, k, axis=0), 1.0)
        b_sh = jnp.where(row >= k, pltpu.roll(big_b, k, axis=0), 0.0)
        big_b = big_a * b_sh + big_b
        big_a = big_a * a_sh
        k *= 2
    h = big_a * h_ref[0:1, :] + big_b
    h_ref[...] = h[rows - SUBLANES:]
    h_ref[0:1, :] = h[rows - 1:rows]
    o_ref[...] = h * jax.nn.gelu(gate_ref[...], approximate=True)


def rglru(p3, conv_w, conv_b, w_a, b_a, w_x, b_x, lam, l, tt):
    bsz, seq, _ = p3.shape
    tt = min(tt, seq)
    assert seq % tt == 0
    c = D_GROUP
    vec = lambda: pl.BlockSpec((None, 1, c), lambda b, t: (l, 0, 0))
    return pl.pallas_call(
        _rglru_kernel,
        grid=(bsz, seq // tt),
        in_specs=[
            pl.BlockSpec((None, tt, c), lambda b, t: (b, t, 3)),
            pl.BlockSpec((None, tt, c), lambda b, t: (b, t, 4)),
            pl.BlockSpec((None, RG_CONV, c), lambda b, t: (l, 0, 0)),
            vec(),
            pl.BlockSpec((None, c // HEAD_DIM, HEAD_DIM, HEAD_DIM), lambda b, t: (l, 0, 0, 0)),
            vec(),
            pl.BlockSpec((None, c // HEAD_DIM, HEAD_DIM, HEAD_DIM), lambda b, t: (l, 0, 0, 0)),
            vec(),
            vec(),
        ],
        out_specs=pl.BlockSpec((None, tt, c), lambda b, t: (b, t, 0)),
        out_shape=jax.ShapeDtypeStruct((bsz, seq, c), F32),
        scratch_shapes=[pltpu.VMEM((SUBLANES, c), F32), pltpu.VMEM((SUBLANES, c), F32)],
        compiler_params=_params("parallel", "arbitrary"),
        name="rglru",
    )(p3, p3, conv_w, conv_b, w_a, b_a, w_x, b_x, lam)


def _conformer_kernel(val_ref, gate_ref, cw_ref, cb_ref, ng_ref, nb_ref, o_ref, halo_ref):
    t = pl.program_id(1)
    rows = val_ref.shape[0]

    @pl.when(t == 0)
    def _():
        halo_ref[...] = jnp.zeros(halo_ref.shape, F32)

    glu = val_ref[...] * jax.nn.sigmoid(gate_ref[...])
    ext = jnp.concatenate([halo_ref[...], glu], axis=0)
    halo_ref[...] = glu[rows - CV_HALO:]
    rolled = [ext] + [pltpu.roll(ext, r, axis=0) for r in range(1, SUBLANES)]
    u = jnp.zeros_like(glu) + cb_ref[...]
    for j in range(CV_WIDTH):
        d = CV_WIDTH - 1 - j
        r = d % SUBLANES
        start = CV_HALO - (d - r)
        u = u + cw_ref[j:j + 1, :] * rolled[r][start:start + rows]
    parts = []
    for g in range(D_GROUP // HEAD_DIM):
        parts.append(_standardize(u[:, g * HEAD_DIM:(g + 1) * HEAD_DIM]))
    un = jnp.concatenate(parts, axis=-1)
    o_ref[...] = _silu(un * ng_ref[...] + nb_ref[...])


def conformer(p3, cv_w, cv_b, ng, nb, l, tt):
    bsz, seq, _ = p3.shape
    tt = min(tt, seq)
    assert seq % tt == 0 and tt >= CV_HALO
    c = D_GROUP
    vec = lambda: pl.BlockSpec((None, 1, c), lambda b, t: (l, 0, 0))
    return pl.pallas_call(
        _conformer_kernel,
        grid=(bsz, seq // tt),
        in_specs=[
            pl.BlockSpec((None, tt, c), lambda b, t: (b, t, 5)),
            pl.BlockSpec((None, tt, c), lambda b, t: (b, t, 6)),
            pl.BlockSpec((None, CV_WIDTH, c), lambda b, t: (l, 0, 0)),
            vec(), vec(), vec(),
        ],
        out_specs=pl.BlockSpec((None, tt, c), lambda b, t: (b, t, 0)),
        out_shape=jax.ShapeDtypeStruct((bsz, seq, c), F32),
        scratch_shapes=[pltpu.VMEM((CV_HALO, c), F32)],
        compiler_params=_params("parallel", "arbitrary"),
        name="conformer",
    )(p3, p3, cv_w, cv_b, ng, nb)


def _hgrn2_chunk(q, kk, v, lf, state_t):
    c = HG_CHUNK
    sub = HG_SUB
    row = lax.broadcasted_iota(jnp.int32, (c, HEAD_DIM), 0)
    b = lf
    k = 1
    while k < c:
        b = b + jnp.where(row >= k, pltpu.roll(b, k, axis=0), 0.0)
        k *= 2
    vb = v.astype(BF16)

    o = lax.dot_general((q * jnp.exp(b)).astype(BF16), state_t.astype(BF16), NT_DIMS,
                        preferred_element_type=F32)

    lane_s = lax.broadcasted_iota(jnp.int32, (sub, c), 1)
    row_s = lax.broadcasted_iota(jnp.int32, (sub, c), 0)
    att_rows = []
    for blk in range(c // sub):
        lo = blk * sub
        q_i = q[lo:lo + sub]
        b_i = b[lo:lo + sub]
        k_i = kk[lo:lo + sub]
        att = jnp.zeros((sub, c), F32)
        if blk > 0:
            ref = b[lo - 1:lo]
            qs = (q_i * jnp.exp(b_i - ref)).astype(BF16)
            ks = kk[:lo] * jnp.exp(ref - b[:lo])
            ks = jnp.concatenate([ks, jnp.zeros((c - lo, HEAD_DIM), F32)], axis=0).astype(BF16)
            att = lax.dot_general(qs, ks, NT_DIMS, preferred_element_type=F32)
        for s in range(sub):
            decay = jnp.exp(jnp.minimum(b_i - b_i[s:s + 1], 0.0))
            w = jnp.sum(q_i * k_i[s:s + 1] * decay, axis=-1, keepdims=True)
            att = att + jnp.where((lane_s == lo + s) & (row_s >= s), w, 0.0)
        att_rows.append(att)
    att = jnp.concatenate(att_rows, axis=0)
    o = o + jnp.dot(att.astype(BF16), vb, preferred_element_type=F32)

    b_last = b[c - 1:c]
    ks = (kk * jnp.exp(b_last - b)).astype(BF16)
    new_state_t = state_t * jnp.exp(b_last) + lax.dot_general(vb, ks, TN_DIMS, preferred_element_type=F32)
    return o, new_state_t


def _hgrn2_kernel(q_ref, f_ref, v_ref, g_ref, lbp_ref, ng_ref, o_ref, state_ref, *, layer):
    t = pl.program_id(1)
    rows = q_ref.shape[0]

    @pl.when(t == 0)
    def _():
        state_ref[...] = jnp.zeros(state_ref.shape, F32)

    lbp = lbp_ref[...]
    e = jnp.exp(lbp - jnp.max(lbp, axis=0, keepdims=True))
    sm = e / jnp.sum(e, axis=0, keepdims=True)
    lb = jnp.sum(sm[0:layer + 1], axis=0, keepdims=True) - sm[0:1]

    for h in range(N_HEADS):
        cs = slice(h * HEAD_DIM, (h + 1) * HEAD_DIM)
        lb_h = lb[:, cs]
        state_t = state_ref[h]
        for ci in range(rows // HG_CHUNK):
            rs = slice(ci * HG_CHUNK, (ci + 1) * HG_CHUNK)
            sig = jax.nn.sigmoid(f_ref[rs, cs])
            lf = jnp.log(lb_h + (1.0 - lb_h) * sig)
            kk = (1.0 - lb_h) * (1.0 - sig)
            o, state_t = _hgrn2_chunk(q_ref[rs, cs], kk, v_ref[rs, cs], lf, state_t)
            o = o * lax.rsqrt(jnp.mean(o * o, axis=-1, keepdims=True) + LN_EPS)
            o_ref[rs, cs] = o * ng_ref[:, cs] * _silu(g_ref[rs, cs])
        state_ref[h] = state_t


def hgrn2(p3, lower_bound_params, norm_g, l, tt):
    bsz, seq, _ = p3.shape
    tt = min(tt, seq)
    assert seq % tt == 0 and tt % HG_CHUNK == 0
    c = D_GROUP
    depth = lower_bound_params.shape[0]
    tok = lambda col: pl.BlockSpec((None, tt, c), lambda b, t: (b, t, col))
    return pl.pallas_call(
        functools.partial(_hgrn2_kernel, layer=l),
        grid=(bsz, seq // tt),
        in_specs=[
            tok(7), tok(8), tok(9), tok(10),
            pl.BlockSpec((depth, c), lambda b, t: (0, 0)),
            pl.BlockSpec((None, 1, c), lambda b, t: (l, 0, 0)),
        ],
        out_specs=pl.BlockSpec((None, tt, c), lambda b, t: (b, t, 0)),
        out_shape=jax.ShapeDtypeStruct((bsz, seq, c), F32),
        scratch_shapes=[pltpu.VMEM((N_HEADS, HEAD_DIM, HEAD_DIM), F32)],
        compiler_params=_params("parallel", "arbitrary"),
        name="hgrn2",
    )(p3, p3, p3, p3, lower_bound_params, norm_g)


def _out_proj_ln_kernel(x_ref, ya_ref, yb_ref, yc_ref, yd_ref, w_ref, g_ref, b_ref, o_ref, *, alpha):
    c = D_GROUP
    rows = x_ref.shape[0]
    step = min(rows, OUT_PROJ_ROWS)
    for r0 in range(0, rows, step):
        rs = slice(r0, r0 + step)
        y = jnp.dot(ya_ref[rs, :].astype(BF16), w_ref[0:c, :], preferred_element_type=F32)
        y = y + jnp.dot(yb_ref[rs, :].astype(BF16), w_ref[c:2 * c, :], preferred_element_type=F32)
        y = y + jnp.dot(yc_ref[rs, :].astype(BF16), w_ref[2 * c:3 * c, :], preferred_element_type=F32)
        y = y + jnp.dot(yd_ref[rs, :].astype(BF16), w_ref[3 * c:4 * c, :], preferred_element_type=F32)
        z = alpha * x_ref[rs, :] + y
        o_ref[rs, :] = _standardize(z) * g_ref[...] + b_ref[...]


def out_proj_ln(x, ya, yb, yc, yd, w_out, ln_g, ln_b, l, jn, alpha, tm):
    m, d = x.shape
    c = D_GROUP
    tm = min(tm, m)
    assert m % tm == 0
    mix = lambda: pl.BlockSpec((tm, c), lambda i: (i, 0))
    return pl.pallas_call(
        functools.partial(_out_proj_ln_kernel, alpha=alpha),
        grid=(m // tm,),
        in_specs=[
            pl.BlockSpec((tm, d), lambda i: (i, 0)),
            mix(), mix(), mix(), mix(),
            pl.BlockSpec((None, 4 * c, d), lambda i: (l, 0, 0)),
            pl.BlockSpec((None, 1, d), lambda i: (jn, 0, 0)),
            pl.BlockSpec((None, 1, d), lambda i: (jn, 0, 0)),
        ],
        out_specs=pl.BlockSpec((tm, d), lambda i: (i, 0)),
        out_shape=jax.ShapeDtypeStruct((m, d), F32),
        compiler_params=_params("parallel"),
        name="out_proj_ln",
    )(x, ya, yb, yc, yd, w_out, ln_g, ln_b)


def kernel(x, ln_g, ln_b, ffn_w_gate, ffn_w_up, ffn_w_down, w_in, w_out, rg_conv_w, rg_conv_b,
           rg_w_a, rg_b_a, rg_w_x, rg_b_x, rg_lambda, cv_w, cv_b, cv_ln_g, cv_ln_b,
           hg_lower_bounds, hg_norm_g):
    bsz, seq, d = x.shape
    depth = w_in.shape[0]
    alpha = (2 * depth) ** 0.25
    m = bsz * seq

    wg = ffn_w_gate.astype(BF16)
    wu = ffn_w_up.astype(BF16)
    wd = ffn_w_down.astype(BF16)
    w_in_b = w_in.astype(BF16)
    w_out_b = w_out.astype(BF16)
    ln_g3 = ln_g.reshape(depth * 3, 1, d)
    ln_b3 = ln_b.reshape(depth * 3, 1, d)
    row = lambda a: a.reshape(depth, 1, a.shape[-1])

    h = x.reshape(m, d)
    for l in range(depth):
        h = ffn_ln(h, wg, wu, wd, ln_g3, ln_b3, l, 0, 3 * l, alpha, tm=1024, tf=512)
        p3 = in_proj(h, w_in_b, l, tm=1024, tn=IN_PROJ_COLS).reshape(bsz, seq, -1)
        y_a = moba(p3)
        y_b = rglru(p3, rg_conv_w, row(rg_conv_b), rg_w_a, row(rg_b_a), rg_w_x, row(rg_b_x),
                    row(rg_lambda), l, tt=256)
        y_c = conformer(p3, cv_w, row(cv_b), row(cv_ln_g), row(cv_ln_b), l, tt=512)
        y_d = hgrn2(p3, hg_lower_bounds, row(hg_norm_g), l, tt=256)
        flat = lambda y: y.reshape(m, D_GROUP)
        h = out_proj_ln(h, flat(y_a), flat(y_b), flat(y_c), flat(y_d), w_out_b, ln_g3, ln_b3,
                        l, 3 * l + 1, alpha, tm=512)
        h = ffn_ln(h, wg, wu, wd, ln_g3, ln_b3, l, 1, 3 * l + 2, alpha, tm=1024, tf=512)
    return h.reshape(bsz, seq, d)
```

```python
import functools

import jax
import jax.numpy as jnp
from jax import lax
from jax.experimental import pallas as pl
from jax.experimental.pallas import tpu as pltpu

F32 = jnp.float32
BF16 = jnp.bfloat16

LANES = 128
SUBLANES = 8
VMEM_LIMIT_BYTES = 56 * 1024 * 1024

D_GROUP = 512
HEAD_DIM = 128
N_HEADS = D_GROUP // HEAD_DIM
MOBA_BLOCK = 256
MOBA_TOPK = 3
MOBA_GROUP = 4
MOBA_Q_BLOCKS = 2
LOG2_E = 1.4426950408889634
MASK_VALUE = -1e30
RG_C = 8.0
RG_CONV = 4
CV_WIDTH = 31
CV_HALO = 32
FFN_COLS = 512
IN_PROJ_COLS = 1408
OUT_PROJ_ROWS = 256
FFN_DOWN_COLS = 512
HG_CHUNK = 64
HG_SUB = 8
LN_EPS = 1e-5

NT_DIMS = (((1,), (1,)), ((), ()))
TN_DIMS = (((0,), (0,)), ((), ()))


def _params(*sem):
    return pltpu.CompilerParams(dimension_semantics=sem, vmem_limit_bytes=VMEM_LIMIT_BYTES)


def _standardize(z):
    mu = jnp.mean(z, axis=-1, keepdims=True)
    zc = z - mu
    var = jnp.mean(zc * zc, axis=-1, keepdims=True)
    return zc * lax.rsqrt(var + LN_EPS)


def _silu(z):
    return z * jax.nn.sigmoid(z)


def _shift_rows(ext, d, halo, rows):
    if d % SUBLANES == 0:
        return ext[halo - d:halo - d + rows]
    r = d % SUBLANES
    base = d - r
    rolled = pltpu.roll(ext, r, axis=0)
    return rolled[halo - base:halo - base + rows]


def _ffn_ln_kernel(x_ref, wg_ref, wu_ref, wd_ref, g_ref, b_ref, o_ref, xb_ref, *, alpha):
    f = pl.program_id(1)

    @pl.when(f == 0)
    def _():
        xb_ref[...] = x_ref[...].astype(BF16)
        o_ref[...] = jnp.zeros(o_ref.shape, F32)

    xb = xb_ref[...]
    hg = jnp.dot(xb, wg_ref[...], preferred_element_type=F32)
    hu = jnp.dot(xb, wu_ref[...], preferred_element_type=F32)
    h = (_silu(hg) * hu).astype(BF16)
    d = o_ref.shape[1]
    for c0 in range(0, d, FFN_DOWN_COLS):
        cs = slice(c0, min(c0 + FFN_DOWN_COLS, d))
        o_ref[:, cs] += jnp.dot(h, wd_ref[:, cs], preferred_element_type=F32)

    @pl.when(f == pl.num_programs(1) - 1)
    def _():
        z = alpha * x_ref[...] + 0.5 * o_ref[...]
        o_ref[...] = _standardize(z) * g_ref[...] + b_ref[...]


def block_cols(w, t):
    *lead, d, f = w.shape
    assert f % t == 0
    return jnp.swapaxes(w.reshape(*lead, d, f // t, t), -3, -2)


def ffn_ln(x, wg, wu, wd, ln_g, ln_b, l, j, jn, alpha, tm):
    m, d = x.shape
    n_f, tf = wg.shape[-3], wg.shape[-1]
    tm = min(tm, m)
    assert m % tm == 0 and wd.shape[-2] == n_f * tf
    grid = (m // tm, n_f)
    return pl.pallas_call(
        functools.partial(_ffn_ln_kernel, alpha=alpha),
        grid=grid,
        in_specs=[
            pl.BlockSpec((tm, d), lambda i, f: (i, 0), pipeline_mode=pl.Buffered(1)),
            pl.BlockSpec((None, None, None, d, tf), lambda i, f: (l, j, f, 0, 0)),
            pl.BlockSpec((None, None, None, d, tf), lambda i, f: (l, j, f, 0, 0)),
            pl.BlockSpec((None, None, tf, d), lambda i, f: (l, j, f, 0)),
            pl.BlockSpec((None, 1, d), lambda i, f: (jn, 0, 0)),
            pl.BlockSpec((None, 1, d), lambda i, f: (jn, 0, 0)),
        ],
        out_specs=pl.BlockSpec((tm, d), lambda i, f: (i, 0)),
        out_shape=jax.ShapeDtypeStruct((m, d), F32),
        scratch_shapes=[pltpu.VMEM((tm, d), BF16)],
        compiler_params=_params("parallel", "arbitrary"),
        name="ffn_ln",
    )(x, wg, wu, wd, ln_g, ln_b)


def _in_proj_kernel(x_ref, w_ref, o_ref, xb_ref):
    @pl.when(pl.program_id(1) == 0)
    def _():
        xb_ref[...] = x_ref[...].astype(BF16)

    o_ref[...] = jnp.dot(xb_ref[...], w_ref[...], preferred_element_type=F32)


def in_proj(x, w_in, l, tm):
    m, d = x.shape
    n_c, tn = w_in.shape[-3], w_in.shape[-1]
    n = n_c * tn
    tm = min(tm, m)
    assert m % tm == 0
    return pl.pallas_call(
        _in_proj_kernel,
        grid=(m // tm, n_c),
        in_specs=[
            pl.BlockSpec((tm, d), lambda i, c: (i, 0)),
            pl.BlockSpec((None, None, d, tn), lambda i, c: (l, c, 0, 0)),
        ],
        out_specs=pl.BlockSpec((tm, tn), lambda i, c: (i, c)),
        out_shape=jax.ShapeDtypeStruct((m, n), F32),
        scratch_shapes=[pltpu.VMEM((tm, d), BF16)],
        compiler_params=_params("parallel", "arbitrary"),
        name="in_proj",
    )(x, w_in)


def _fold_rows_max(a):
    rows, cols = a.shape
    return jnp.max(a.reshape(rows // SUBLANES, SUBLANES, cols), axis=0)


def _moba_kernel(q_ref, k_ref, v_ref, o_ref, qat_ref, ka_ref, vat_ref, vat3_ref, s_ref, so_ref, *, nb, scale):
    j = pl.program_id(2)
    blk = MOBA_BLOCK
    grp = MOBA_GROUP
    dh = HEAD_DIM
    seq = nb * blk
    tq = MOBA_Q_BLOCKS * blk
    span = grp * blk
    nbp = -(-nb // SUBLANES) * SUBLANES

    @pl.when(j == 0)
    def _():
        k = k_ref[...]
        lane = lax.broadcasted_iota(jnp.int32, (seq, LANES), 1)
        tile = lax.broadcasted_iota(jnp.int32, (nb, blk, LANES), 0).reshape(seq, LANES)
        ka_ref[:, 0:dh] = k.astype(BF16)
        ka_ref[:, dh:2 * dh] = jnp.where(tile == lane, 1.0, 0.0).astype(BF16)
        vat = jnp.concatenate([v_ref[...].T, jnp.ones((dh, seq), F32)], axis=0).astype(BF16)
        vat_ref[...] = vat
        for n in range(nb):
            vat3_ref[n] = vat[:, n * blk:(n + 1) * blk]
        k_mean = jnp.mean(k.reshape(nb, blk, dh), axis=1)
        k_mean = jnp.concatenate([k_mean, jnp.zeros((LANES - nb, dh), F32)], axis=0)

        q = q_ref[...]
        gate = lax.dot_general(k_mean.astype(BF16), q.astype(BF16), NT_DIMS, preferred_element_type=F32)
        g = gate[0:nbp]
        blk_row = lax.broadcasted_iota(jnp.int32, (nbp, seq), 0)
        q_blk = lax.shift_right_logical(lax.broadcasted_iota(jnp.int32, (nbp, seq), 1), blk.bit_length() - 1)
        row_f = blk_row.astype(F32)
        past = blk_row < q_blk
        g = jnp.where(past, g, MASK_VALUE)
        sel = jnp.zeros((nbp, seq), F32)
        for _ in range(MOBA_TOPK):
            top = jnp.max(g, axis=0, keepdims=True)
            first = jnp.min(jnp.where(g == top, row_f, float(LANES)), axis=0, keepdims=True)
            pick = row_f == first
            sel = jnp.where(pick, 1.0, sel)
            g = jnp.where(pick, -jnp.inf, g)
        bias = jnp.where(past & (sel > 0.5), 0.0, MASK_VALUE)
        qt = (q * (scale * LOG2_E)).T
        qat = jnp.concatenate([qt, bias, jnp.zeros((dh - nbp, seq), F32)], axis=0).astype(BF16)
        for t in range(seq // tq):
            qat_ref[t] = qat[:, t * tq:(t + 1) * tq]

    qat = qat_ref[j]
    first_blk = j * MOBA_Q_BLOCKS

    row = lax.broadcasted_iota(jnp.int32, (blk, blk), 0)
    col = lax.broadcasted_iota(jnp.int32, (blk, blk), 1)
    own_max = []
    for c in range(MOBA_Q_BLOCKS):
        start = pl.multiple_of((first_blk + c) * blk, blk)
        s = jnp.dot(ka_ref[pl.ds(start, blk), 0:dh], qat[0:dh, c * blk:(c + 1) * blk],
                    preferred_element_type=F32)
        s = jnp.where(row <= col, s, MASK_VALUE)
        so_ref[c] = s
        own_max.append(_fold_rows_max(s))
    mrun0 = jnp.concatenate(own_max, axis=1)

    dense_blocks = first_blk + MOBA_Q_BLOCKS - 1
    n_groups = (dense_blocks + grp - 1) // grp

    def attend(count):
        mrun = mrun0
        for g in range(count):
            s = jnp.dot(ka_ref[g * span:(g + 1) * span, :], qat, preferred_element_type=F32)
            s_ref[g] = s
            mrun = jnp.maximum(mrun, _fold_rows_max(s))
        m = jnp.max(mrun, axis=0, keepdims=True)
        cols = []
        for c in range(MOBA_Q_BLOCKS):
            p = jnp.exp2((so_ref[c] - m[:, c * blk:(c + 1) * blk]).astype(BF16))
            cols.append(jnp.dot(vat3_ref[first_blk + c], p, preferred_element_type=F32))
        acc = jnp.concatenate(cols, axis=1)
        for g in range(count):
            p = jnp.exp2((s_ref[g] - m).astype(BF16))
            acc = acc + jnp.dot(vat_ref[:, g * span:(g + 1) * span], p, preferred_element_type=F32)
        o_ref[...] = (acc[0:dh] / acc[dh:2 * dh]).T

    min_count = (MOBA_Q_BLOCKS - 1 + grp - 1) // grp
    max_count = (nb - 1 + grp - 1) // grp
    for count in range(min_count, max_count + 1):
        pl.when(n_groups == count)(functools.partial(attend, count))


def moba(p3):
    bsz, seq, _ = p3.shape
    blk = MOBA_BLOCK
    grp = MOBA_GROUP
    tq = MOBA_Q_BLOCKS * blk
    assert seq % (blk * grp) == 0 and seq % tq == 0
    nb = seq // blk
    assert nb <= HEAD_DIM
    max_count = (nb - 1 + grp - 1) // grp
    kernel = functools.partial(_moba_kernel, nb=nb, scale=HEAD_DIM ** -0.5)
    col = lambda c: pl.BlockSpec((None, seq, HEAD_DIM), lambda b, h, j: (b, 0, c * N_HEADS + h))
    return pl.pallas_call(
        kernel,
        grid=(bsz, N_HEADS, seq // tq),
        in_specs=[col(0), col(1), col(2)],
        out_specs=pl.BlockSpec((None, tq, HEAD_DIM), lambda b, h, j: (b, j, h)),
        out_shape=jax.ShapeDtypeStruct((bsz, seq, D_GROUP), F32),
        scratch_shapes=[
            pltpu.VMEM((seq // tq, 2 * HEAD_DIM, tq), BF16),
            pltpu.VMEM((seq, 2 * HEAD_DIM), BF16),
            pltpu.VMEM((2 * HEAD_DIM, seq), BF16),
            pltpu.VMEM((nb, 2 * HEAD_DIM, blk), BF16),
            pltpu.VMEM((max_count, grp * blk, tq), F32),
            pltpu.VMEM((MOBA_Q_BLOCKS, blk, blk), F32),
        ],
        compiler_params=_params("parallel", "parallel", "arbitrary"),
        name="moba",
    )(p3, p3, p3)


def _rglru_kernel(gate_ref, x_ref, cw_ref, cb_ref, wa_ref, ba_ref, wx_ref, bx_ref, lam_ref,
                  o_ref, halo_ref, h_ref):
    t = pl.program_id(1)
    rows = x_ref.shape[0]

    @pl.when(t == 0)
    def _():
        halo_ref[...] = jnp.zeros(halo_ref.shape, F32)
        h_ref[...] = jnp.zeros(h_ref.shape, F32)

    x = x_ref[...]
    ext = jnp.concatenate([halo_ref[...], x], axis=0)
    halo_ref[...] = x[rows - SUBLANES:]
    xc = jnp.zeros_like(x) + cb_ref[...]
    for j in range(RG_CONV):
        xc = xc + cw_ref[j:j + 1, :] * _shift_rows(ext, RG_CONV - 1 - j, SUBLANES, rows)

    xcb = xc.astype(BF16)
    r_parts, i_parts = [], []
    for g in range(D_GROUP // HEAD_DIM):
        blk = xcb[:, g * HEAD_DIM:(g + 1) * HEAD_DIM]
        r_parts.append(jnp.dot(blk, wa_ref[g].astype(BF16), preferred_element_type=F32))
        i_parts.append(jnp.dot(blk, wx_ref[g].astype(BF16), preferred_element_type=F32))
    r = jax.nn.sigmoid(jnp.concatenate(r_parts, axis=-1) + ba_ref[...])
    ig = jax.nn.sigmoid(jnp.concatenate(i_parts, axis=-1) + bx_ref[...])
    neg_lam = -lam_ref[...]
    softplus = jnp.maximum(neg_lam, 0.0) + jnp.log1p(jnp.exp(-jnp.abs(neg_lam)))
    log_a = -RG_C * r * softplus
    a = jnp.exp(log_a)
    u = jnp.sqrt(jnp.maximum(1.0 - jnp.exp(2.0 * log_a), 0.0)) * (ig * xc)

    in_group = lax.broadcasted_iota(jnp.int32, a.shape, 0) & (SUBLANES - 1)
    big_a, big_b = a, u
    k = 1
    while k < SUBLANES:
        a_sh = jnp.where(in_group >= k, pltpu.roll(big_a, k, axis=0), 1.0)
        b_sh = jnp.where(in_group >= k, pltpu.roll(big_b, k, axis=0), 0.0)
        big_b = big_a * b_sh + big_b
        big_a = big_a * a_sh
        k *= 2
    carry = h_ref[0:1, :]
    groups = []
    for r0 in range(0, rows, SUBLANES):
        h_grp = big_a[r0:r0 + SUBLANES] * carry + big_b[r0:r0 + SUBLANES]
        groups.append(h_grp)
        carry = h_grp[SUBLANES - 1:SUBLANES]
    h = jnp.concatenate(groups, axis=0)
    h_ref[0:1, :] = carry
    o_ref[...] = h * jax.nn.gelu(gate_ref[...], approximate=True)


def rglru(p3, conv_w, conv_b, w_a, b_a, w_x, b_x, lam, l, tt):
    bsz, seq, _ = p3.shape
    tt = min(tt, seq)
    assert seq % tt == 0
    c = D_GROUP
    vec = lambda: pl.BlockSpec((None, 1, c), lambda b, t: (l, 0, 0))
    return pl.pallas_call(
        _rglru_kernel,
        grid=(bsz, seq // tt),
        in_specs=[
            pl.BlockSpec((None, tt, c), lambda b, t: (b, t, 3)),
            pl.BlockSpec((None, tt, c), lambda b, t: (b, t, 4)),
            pl.BlockSpec((None, RG_CONV, c), lambda b, t: (l, 0, 0)),
            vec(),
            pl.BlockSpec((None, c // HEAD_DIM, HEAD_DIM, HEAD_DIM), lambda b, t: (l, 0, 0, 0)),
            vec(),
            pl.BlockSpec((None, c // HEAD_DIM, HEAD_DIM, HEAD_DIM), lambda b, t: (l, 0, 0, 0)),
            vec(),
            vec(),
        ],
        out_specs=pl.BlockSpec((None, tt, c), lambda b, t: (b, t, 0)),
        out_shape=jax.ShapeDtypeStruct((bsz, seq, c), F32),
        scratch_shapes=[pltpu.VMEM((SUBLANES, c), F32), pltpu.VMEM((SUBLANES, c), F32)],
        compiler_params=_params("parallel", "arbitrary"),
        name="rglru",
    )(p3, p3, conv_w, conv_b, w_a, b_a, w_x, b_x, lam)


def _conformer_kernel(val_ref, gate_ref, cw_ref, cb_ref, ng_ref, nb_ref, o_ref, halo_ref):
    t = pl.program_id(1)
    rows = val_ref.shape[0]

    @pl.when(t == 0)
    def _():
        halo_ref[...] = jnp.zeros(halo_ref.shape, F32)

    glu = val_ref[...] * jax.nn.sigmoid(gate_ref[...])
    ext = jnp.concatenate([halo_ref[...], glu], axis=0)
    halo_ref[...] = glu[rows - CV_HALO:]
    rolled = [ext] + [pltpu.roll(ext, r, axis=0) for r in range(1, SUBLANES)]
    u = jnp.zeros_like(glu) + cb_ref[...]
    for j in range(CV_WIDTH):
        d = CV_WIDTH - 1 - j
        r = d % SUBLANES
        start = CV_HALO - (d - r)
        u = u + cw_ref[j:j + 1, :] * rolled[r][start:start + rows]
    parts = []
    for g in range(D_GROUP // HEAD_DIM):
        parts.append(_standardize(u[:, g * HEAD_DIM:(g + 1) * HEAD_DIM]))
    un = jnp.concatenate(parts, axis=-1)
    o_ref[...] = _silu(un * ng_ref[...] + nb_ref[...])


def conformer(p3, cv_w, cv_b, ng, nb, l, tt):
    bsz, seq, _ = p3.shape
    tt = min(tt, seq)
    assert seq % tt == 0 and tt >= CV_HALO
    c = D_GROUP
    vec = lambda: pl.BlockSpec((None, 1, c), lambda b, t: (l, 0, 0))
    return pl.pallas_call(
        _conformer_kernel,
        grid=(bsz, seq // tt),
        in_specs=[
            pl.BlockSpec((None, tt, c), lambda b, t: (b, t, 5)),
            pl.BlockSpec((None, tt, c), lambda b, t: (b, t, 6)),
            pl.BlockSpec((None, CV_WIDTH, c), lambda b, t: (l, 0, 0)),
            vec(), vec(), vec(),
        ],
        out_specs=pl.BlockSpec((None, tt, c), lambda b, t: (b, t, 0)),
        out_shape=jax.ShapeDtypeStruct((bsz, seq, c), F32),
        scratch_shapes=[pltpu.VMEM((CV_HALO, c), F32)],
        compiler_params=_params("parallel", "arbitrary"),
        name="conformer",
    )(p3, p3, cv_w, cv_b, ng, nb)


def _hgrn2_chunk(q, kk, v, lf2, state_t):
    c = HG_CHUNK
    sub = HG_SUB
    row = lax.broadcasted_iota(jnp.int32, (c, HEAD_DIM), 0)
    b = lf2
    k = 1
    while k < c:
        b = b + jnp.where(row >= k, pltpu.roll(b, k, axis=0), 0.0)
        k *= 2
    vb = v.astype(BF16)

    o = lax.dot_general((q * jnp.exp2(b)).astype(BF16), state_t.astype(BF16), NT_DIMS,
                        preferred_element_type=F32)

    lane_s = lax.broadcasted_iota(jnp.int32, (sub, c), 1)
    row_s = lax.broadcasted_iota(jnp.int32, (sub, c), 0)
    att_rows = []
    for blk in range(c // sub):
        lo = blk * sub
        q_i = q[lo:lo + sub]
        b_i = b[lo:lo + sub]
        k_i = kk[lo:lo + sub]
        att = jnp.zeros((sub, c), F32)
        if blk > 0:
            ref = b[lo - 1:lo]
            qs = (q_i * jnp.exp2(b_i - ref)).astype(BF16)
            ks = kk[:lo] * jnp.exp2(ref - b[:lo])
            ks = jnp.concatenate([ks, jnp.zeros((c - lo, HEAD_DIM), F32)], axis=0).astype(BF16)
            att = lax.dot_general(qs, ks, NT_DIMS, preferred_element_type=F32)
        for s in range(sub):
            decay = jnp.exp2(jnp.minimum(b_i - b_i[s:s + 1], 0.0))
            w = jnp.sum(q_i * k_i[s:s + 1] * decay, axis=-1, keepdims=True)
            att = att + jnp.where((lane_s == lo + s) & (row_s >= s), w, 0.0)
        att_rows.append(att)
    att = jnp.concatenate(att_rows, axis=0)
    o = o + jnp.dot(att.astype(BF16), vb, preferred_element_type=F32)

    b_last = b[c - 1:c]
    ks = (kk * jnp.exp2(b_last - b)).astype(BF16)
    new_state_t = state_t * jnp.exp2(b_last) + lax.dot_general(vb, ks, TN_DIMS, preferred_element_type=F32)
    return o, new_state_t


def _hgrn2_kernel(q_ref, f_ref, v_ref, g_ref, lbp_ref, ng_ref, o_ref, state_ref, *, layer):
    t = pl.program_id(1)
    rows = q_ref.shape[0]

    @pl.when(t == 0)
    def _():
        state_ref[...] = jnp.zeros(state_ref.shape, F32)

    lbp = lbp_ref[...]
    e = jnp.exp(lbp - jnp.max(lbp, axis=0, keepdims=True))
    sm = e / jnp.sum(e, axis=0, keepdims=True)
    lb = jnp.sum(sm[0:layer + 1], axis=0, keepdims=True) - sm[0:1]

    for h in range(N_HEADS):
        cs = slice(h * HEAD_DIM, (h + 1) * HEAD_DIM)
        lb_h = lb[:, cs]
        state_t = state_ref[h]
        for ci in range(rows // HG_CHUNK):
            rs = slice(ci * HG_CHUNK, (ci + 1) * HG_CHUNK)
            sig = jax.nn.sigmoid(f_ref[rs, cs])
            lf2 = jnp.log(lb_h + (1.0 - lb_h) * sig) * LOG2_E
            kk = (1.0 - lb_h) * (1.0 - sig)
            o, state_t = _hgrn2_chunk(q_ref[rs, cs], kk, v_ref[rs, cs], lf2, state_t)
            o = o * lax.rsqrt(jnp.mean(o * o, axis=-1, keepdims=True) + LN_EPS)
            o_ref[rs, cs] = o * ng_ref[:, cs] * _silu(g_ref[rs, cs])
        state_ref[h] = state_t


def hgrn2(p3, lower_bound_params, norm_g, l, tt):
    bsz, seq, _ = p3.shape
    tt = min(tt, seq)
    assert seq % tt == 0 and tt % HG_CHUNK == 0
    c = D_GROUP
    depth = lower_bound_params.shape[0]
    tok = lambda col: pl.BlockSpec((None, tt, c), lambda b, t: (b, t, col))
    return pl.pallas_call(
        functools.partial(_hgrn2_kernel, layer=l),
        grid=(bsz, seq // tt),
        in_specs=[
            tok(7), tok(8), tok(9), tok(10),
            pl.BlockSpec((depth, c), lambda b, t: (0, 0)),
            pl.BlockSpec((None, 1, c), lambda b, t: (l, 0, 0)),
        ],
        out_specs=pl.BlockSpec((None, tt, c), lambda b, t: (b, t, 0)),
        out_shape=jax.ShapeDtypeStruct((bsz, seq, c), F32),
        scratch_shapes=[pltpu.VMEM((N_HEADS, HEAD_DIM, HEAD_DIM), F32)],
        compiler_params=_params("parallel", "arbitrary"),
        name="hgrn2",
    )(p3, p3, p3, p3, lower_bound_params, norm_g)


def _out_proj_ln_kernel(x_ref, ya_ref, yb_ref, yc_ref, yd_ref, w_ref, g_ref, b_ref, o_ref, *, alpha):
    c = D_GROUP
    rows = x_ref.shape[0]
    step = min(rows, OUT_PROJ_ROWS)
    for r0 in range(0, rows, step):
        rs = slice(r0, r0 + step)
        y = jnp.dot(ya_ref[rs, :].astype(BF16), w_ref[0:c, :], preferred_element_type=F32)
        y = y + jnp.dot(yb_ref[rs, :].astype(BF16), w_ref[c:2 * c, :], preferred_element_type=F32)
        y = y + jnp.dot(yc_ref[rs, :].astype(BF16), w_ref[2 * c:3 * c, :], preferred_element_type=F32)
        y = y + jnp.dot(yd_ref[rs, :].astype(BF16), w_ref[3 * c:4 * c, :], preferred_element_type=F32)
        z = alpha * x_ref[rs, :] + y
        o_ref[rs, :] = _standardize(z) * g_ref[...] + b_ref[...]


def out_proj_ln(x, ya, yb, yc, yd, w_out, ln_g, ln_b, l, jn, alpha, tm):
    m, d = x.shape
    c = D_GROUP
    tm = min(tm, m)
    assert m % tm == 0
    mix = lambda: pl.BlockSpec((tm, c), lambda i: (i, 0))
    return pl.pallas_call(
        functools.partial(_out_proj_ln_kernel, alpha=alpha),
        grid=(m // tm,),
        in_specs=[
            pl.BlockSpec((tm, d), lambda i: (i, 0)),
            mix(), mix(), mix(), mix(),
            pl.BlockSpec((None, 4 * c, d), lambda i: (l, 0, 0)),
            pl.BlockSpec((None, 1, d), lambda i: (jn, 0, 0)),
            pl.BlockSpec((None, 1, d), lambda i: (jn, 0, 0)),
        ],
        out_specs=pl.BlockSpec((tm, d), lambda i: (i, 0)),
        out_shape=jax.ShapeDtypeStruct((m, d), F32),
        compiler_params=_params("parallel"),
        name="out_proj_ln",
    )(x, ya, yb, yc, yd, w_out, ln_g, ln_b)


def kernel(x, ln_g, ln_b, ffn_w_gate, ffn_w_up, ffn_w_down, w_in, w_out, rg_conv_w, rg_conv_b,
           rg_w_a, rg_b_a, rg_w_x, rg_b_x, rg_lambda, cv_w, cv_b, cv_ln_g, cv_ln_b,
           hg_lower_bounds, hg_norm_g):
    bsz, seq, d = x.shape
    depth = w_in.shape[0]
    alpha = (2 * depth) ** 0.25
    m = bsz * seq

    wg = block_cols(ffn_w_gate, FFN_COLS).astype(BF16)
    wu = block_cols(ffn_w_up, FFN_COLS).astype(BF16)
    wd = ffn_w_down.astype(BF16)
    w_in_b = block_cols(w_in, IN_PROJ_COLS).astype(BF16)
    w_out_b = w_out.astype(BF16)
    ln_g3 = ln_g.reshape(depth * 3, 1, d)
    ln_b3 = ln_b.reshape(depth * 3, 1, d)
    row = lambda a: a.reshape(depth, 1, a.shape[-1])

    h = x.reshape(m, d)
    for l in range(depth):
        h = ffn_ln(h, wg, wu, wd, ln_g3, ln_b3, l, 0, 3 * l, alpha, tm=1024)
        p3 = in_proj(h, w_in_b, l, tm=1024).reshape(bsz, seq, -1)
        y_a = moba(p3)
        y_b = rglru(p3, rg_conv_w, row(rg_conv_b), rg_w_a, row(rg_b_a), rg_w_x, row(rg_b_x),
                    row(rg_lambda), l, tt=256)
        y_c = conformer(p3, cv_w, row(cv_b), row(cv_ln_g), row(cv_ln_b), l, tt=512)
        y_d = hgrn2(p3, hg_lower_bounds, row(hg_norm_g), l, tt=256)
        flat = lambda y: y.reshape(m, D_GROUP)
        h = out_proj_ln(h, flat(y_a), flat(y_b), flat(y_c), flat(y_d), w_out_b, ln_g3, ln_b3,
                        l, 3 * l + 1, alpha, tm=512)
        h = ffn_ln(h, wg, wu, wd, ln_g3, ln_b3, l, 1, 3 * l + 2, alpha, tm=1024)
    return h.reshape(bsz, seq, d)
```

```python
import functools

import jax
import jax.numpy as jnp
from jax import lax
from jax.experimental import pallas as pl
from jax.experimental.pallas import tpu as pltpu

F32 = jnp.float32
BF16 = jnp.bfloat16

LANES = 128
SUBLANES = 8
VMEM_LIMIT_BYTES = 56 * 1024 * 1024

D_GROUP = 512
HEAD_DIM = 128
N_HEADS = D_GROUP // HEAD_DIM
MOBA_BLOCK = 256
MOBA_TOPK = 3
MOBA_GROUP = 4
MOBA_Q_BLOCKS = 4
LOG2_E = 1.4426950408889634
MASK_VALUE = -1e30
RG_C = 8.0
RG_CONV = 4
CV_WIDTH = 31
CV_HALO = 32
FFN_COLS = 512
IN_PROJ_COLS = 1408
OUT_PROJ_ROWS = 256
FFN_DOWN_COLS = 512
HG_CHUNK = 64
HG_SUB = 8
LN_EPS = 1e-5

NT_DIMS = (((1,), (1,)), ((), ()))
TN_DIMS = (((0,), (0,)), ((), ()))


def _params(*sem):
    return pltpu.CompilerParams(dimension_semantics=sem, vmem_limit_bytes=VMEM_LIMIT_BYTES)


def _standardize(z):
    mu = jnp.mean(z, axis=-1, keepdims=True)
    zc = z - mu
    var = jnp.mean(zc * zc, axis=-1, keepdims=True)
    return zc * lax.rsqrt(var + LN_EPS)


def _silu(z):
    return z * jax.nn.sigmoid(z)


def _shift_rows(ext, d, halo, rows):
    if d % SUBLANES == 0:
        return ext[halo - d:halo - d + rows]
    r = d % SUBLANES
    base = d - r
    rolled = pltpu.roll(ext, r, axis=0)
    return rolled[halo - base:halo - base + rows]


def _ffn_ln_kernel(x_ref, wg_ref, wu_ref, wd_ref, g_ref, b_ref, o_ref, xb_ref, *, alpha):
    f = pl.program_id(1)

    @pl.when(f == 0)
    def _():
        xb_ref[...] = x_ref[...].astype(BF16)
        o_ref[...] = jnp.zeros(o_ref.shape, F32)

    xb = xb_ref[...]
    hg = jnp.dot(xb, wg_ref[...], preferred_element_type=F32)
    hu = jnp.dot(xb, wu_ref[...], preferred_element_type=F32)
    h = (_silu(hg) * hu).astype(BF16)
    d = o_ref.shape[1]
    for c0 in range(0, d, FFN_DOWN_COLS):
        cs = slice(c0, min(c0 + FFN_DOWN_COLS, d))
        o_ref[:, cs] += jnp.dot(h, wd_ref[:, cs], preferred_element_type=F32)

    @pl.when(f == pl.num_programs(1) - 1)
    def _():
        z = alpha * x_ref[...] + 0.5 * o_ref[...]
        o_ref[...] = _standardize(z) * g_ref[...] + b_ref[...]


def ffn_ln(x, wg, wu, wd, ln_g, ln_b, l, j, jn, alpha, tm, tf):
    m, d = x.shape
    dff = wg.shape[-1]
    tm = min(tm, m)
    tf = min(tf, dff)
    assert m % tm == 0 and dff % tf == 0
    grid = (m // tm, dff // tf)
    return pl.pallas_call(
        functools.partial(_ffn_ln_kernel, alpha=alpha),
        grid=grid,
        in_specs=[
            pl.BlockSpec((tm, d), lambda i, f: (i, 0), pipeline_mode=pl.Buffered(1)),
            pl.BlockSpec((None, None, d, tf), lambda i, f: (l, j, 0, f)),
            pl.BlockSpec((None, None, d, tf), lambda i, f: (l, j, 0, f)),
            pl.BlockSpec((None, None, tf, d), lambda i, f: (l, j, f, 0)),
            pl.BlockSpec((None, 1, d), lambda i, f: (jn, 0, 0)),
            pl.BlockSpec((None, 1, d), lambda i, f: (jn, 0, 0)),
        ],
        out_specs=pl.BlockSpec((tm, d), lambda i, f: (i, 0)),
        out_shape=jax.ShapeDtypeStruct((m, d), F32),
        scratch_shapes=[pltpu.VMEM((tm, d), BF16)],
        compiler_params=_params("parallel", "arbitrary"),
        name="ffn_ln",
    )(x, wg, wu, wd, ln_g, ln_b)


def _in_proj_kernel(x_ref, w_ref, o_ref, xb_ref):
    @pl.when(pl.program_id(1) == 0)
    def _():
        xb_ref[...] = x_ref[...].astype(BF16)

    o_ref[...] = jnp.dot(xb_ref[...], w_ref[...], preferred_element_type=F32)


def in_proj(x, w_in, l, tm, tn):
    m, d = x.shape
    n = w_in.shape[-1]
    tm = min(tm, m)
    assert m % tm == 0 and n % tn == 0
    return pl.pallas_call(
        _in_proj_kernel,
        grid=(m // tm, n // tn),
        in_specs=[
            pl.BlockSpec((tm, d), lambda i, c: (i, 0)),
            pl.BlockSpec((None, d, tn), lambda i, c: (l, 0, c)),
        ],
        out_specs=pl.BlockSpec((tm, tn), lambda i, c: (i, c)),
        out_shape=jax.ShapeDtypeStruct((m, n), F32),
        scratch_shapes=[pltpu.VMEM((tm, d), BF16)],
        compiler_params=_params("parallel", "arbitrary"),
        name="in_proj",
    )(x, w_in)


def _fold_rows_max(a):
    rows, cols = a.shape
    return jnp.max(a.reshape(rows // SUBLANES, SUBLANES, cols), axis=0)


def _moba_kernel(q_ref, k_ref, v_ref, o_ref, qat_ref, ka_ref, vat_ref, vat3_ref, s_ref, so_ref, *, nb, scale):
    j = pl.program_id(2)
    blk = MOBA_BLOCK
    grp = MOBA_GROUP
    dh = HEAD_DIM
    seq = nb * blk
    tq = MOBA_Q_BLOCKS * blk
    span = grp * blk
    nbp = -(-nb // SUBLANES) * SUBLANES

    @pl.when(j == 0)
    def _():
        k = k_ref[...]
        lane = lax.broadcasted_iota(jnp.int32, (seq, LANES), 1)
        tile = lax.broadcasted_iota(jnp.int32, (nb, blk, LANES), 0).reshape(seq, LANES)
        ka_ref[:, 0:dh] = k.astype(BF16)
        ka_ref[:, dh:2 * dh] = jnp.where(tile == lane, 1.0, 0.0).astype(BF16)
        vat = jnp.concatenate([v_ref[...].T, jnp.ones((dh, seq), F32)], axis=0).astype(BF16)
        vat_ref[...] = vat
        for n in range(nb):
            vat3_ref[n] = vat[:, n * blk:(n + 1) * blk]
        k_mean = jnp.mean(k.reshape(nb, blk, dh), axis=1)
        k_mean = jnp.concatenate([k_mean, jnp.zeros((LANES - nb, dh), F32)], axis=0)

        q = q_ref[...]
        gate = lax.dot_general(k_mean.astype(BF16), q.astype(BF16), NT_DIMS, preferred_element_type=F32)
        g = gate[0:nbp]
        blk_row = lax.broadcasted_iota(jnp.int32, (nbp, seq), 0)
        q_blk = lax.shift_right_logical(lax.broadcasted_iota(jnp.int32, (nbp, seq), 1), blk.bit_length() - 1)
        row_f = blk_row.astype(F32)
        past = blk_row < q_blk
        g = jnp.where(past, g, MASK_VALUE)
        sel = jnp.zeros((nbp, seq), F32)
        for _ in range(MOBA_TOPK):
            top = jnp.max(g, axis=0, keepdims=True)
            first = jnp.min(jnp.where(g == top, row_f, float(LANES)), axis=0, keepdims=True)
            pick = row_f == first
            sel = jnp.where(pick, 1.0, sel)
            g = jnp.where(pick, -jnp.inf, g)
        bias = jnp.where(past & (sel > 0.5), 0.0, MASK_VALUE)
        qt = (q * (scale * LOG2_E)).T
        qat = jnp.concatenate([qt, bias, jnp.zeros((dh - nbp, seq), F32)], axis=0).astype(BF16)
        for t in range(seq // tq):
            qat_ref[t] = qat[:, t * tq:(t + 1) * tq]

    qat = qat_ref[j]
    first_blk = j * MOBA_Q_BLOCKS

    row = lax.broadcasted_iota(jnp.int32, (blk, blk), 0)
    col = lax.broadcasted_iota(jnp.int32, (blk, blk), 1)
    own_max = []
    for c in range(MOBA_Q_BLOCKS):
        start = pl.multiple_of((first_blk + c) * blk, blk)
        s = jnp.dot(ka_ref[pl.ds(start, blk), 0:dh], qat[0:dh, c * blk:(c + 1) * blk],
                    preferred_element_type=F32)
        s = jnp.where(row <= col, s, MASK_VALUE)
        so_ref[c] = s
        own_max.append(_fold_rows_max(s))
    mrun0 = jnp.concatenate(own_max, axis=1)

    dense_blocks = first_blk + MOBA_Q_BLOCKS - 1
    n_groups = (dense_blocks + grp - 1) // grp

    def attend(count):
        mrun = mrun0
        for g in range(count):
            s = jnp.dot(ka_ref[g * span:(g + 1) * span, :], qat, preferred_element_type=F32)
            s_ref[g] = s
            mrun = jnp.maximum(mrun, _fold_rows_max(s))
        m = jnp.max(mrun, axis=0, keepdims=True)
        cols = []
        for c in range(MOBA_Q_BLOCKS):
            p = jnp.exp2((so_ref[c] - m[:, c * blk:(c + 1) * blk]).astype(BF16))
            cols.append(jnp.dot(vat3_ref[first_blk + c], p, preferred_element_type=F32))
        acc = jnp.concatenate(cols, axis=1)
        for g in range(count):
            p = jnp.exp2((s_ref[g] - m).astype(BF16))
            acc = acc + jnp.dot(vat_ref[:, g * span:(g + 1) * span], p, preferred_element_type=F32)
        o_ref[...] = (acc[0:dh] / acc[dh:2 * dh]).T

    min_count = (MOBA_Q_BLOCKS - 1 + grp - 1) // grp
    max_count = (nb - 1 + grp - 1) // grp
    for count in range(min_count, max_count + 1):
        pl.when(n_groups == count)(functools.partial(attend, count))


def moba(p3):
    bsz, seq, _ = p3.shape
    blk = MOBA_BLOCK
    grp = MOBA_GROUP
    tq = MOBA_Q_BLOCKS * blk
    assert seq % (blk * grp) == 0 and seq % tq == 0
    nb = seq // blk
    assert nb <= HEAD_DIM
    max_count = (nb - 1 + grp - 1) // grp
    kernel = functools.partial(_moba_kernel, nb=nb, scale=HEAD_DIM ** -0.5)
    col = lambda c: pl.BlockSpec((None, seq, HEAD_DIM), lambda b, h, j: (b, 0, c * N_HEADS + h))
    return pl.pallas_call(
        kernel,
        grid=(bsz, N_HEADS, seq // tq),
        in_specs=[col(0), col(1), col(2)],
        out_specs=pl.BlockSpec((None, tq, HEAD_DIM), lambda b, h, j: (b, j, h)),
        out_shape=jax.ShapeDtypeStruct((bsz, seq, D_GROUP), F32),
        scratch_shapes=[
            pltpu.VMEM((seq // tq, 2 * HEAD_DIM, tq), BF16),
            pltpu.VMEM((seq, 2 * HEAD_DIM), BF16),
            pltpu.VMEM((2 * HEAD_DIM, seq), BF16),
            pltpu.VMEM((nb, 2 * HEAD_DIM, blk), BF16),
            pltpu.VMEM((max_count, grp * blk, tq), F32),
            pltpu.VMEM((MOBA_Q_BLOCKS, blk, blk), F32),
        ],
        compiler_params=_params("parallel", "parallel", "arbitrary"),
        name="moba",
    )(p3, p3, p3)


def _rglru_kernel(gate_ref, x_ref, cw_ref, cb_ref, wa_ref, ba_ref, wx_ref, bx_ref, lam_ref,
                  o_ref, halo_ref, h_ref):
    t = pl.program_id(1)
    rows = x_ref.shape[0]

    @pl.when(t == 0)
    def _():
        halo_ref[...] = jnp.zeros(halo_ref.shape, F32)
        h_ref[...] = jnp.zeros(h_ref.shape, F32)

    x = x_ref[...]
    ext = jnp.concatenate([halo_ref[...], x], axis=0)
    halo_ref[...] = x[rows - SUBLANES:]
    xc = jnp.zeros_like(x) + cb_ref[...]
    for j in range(RG_CONV):
        xc = xc + cw_ref[j:j + 1, :] * _shift_rows(ext, RG_CONV - 1 - j, SUBLANES, rows)

    xcb = xc.astype(BF16)
    r_parts, i_parts = [], []
    for g in range(D_GROUP // HEAD_DIM):
        blk = xcb[:, g * HEAD_DIM:(g + 1) * HEAD_DIM]
        r_parts.append(jnp.dot(blk, wa_ref[g].astype(BF16), preferred_element_type=F32))
        i_parts.append(jnp.dot(blk, wx_ref[g].astype(BF16), preferred_element_type=F32))
    r = jax.nn.sigmoid(jnp.concatenate(r_parts, axis=-1) + ba_ref[...])
    ig = jax.nn.sigmoid(jnp.concatenate(i_parts, axis=-1) + bx_ref[...])
    neg_lam = -lam_ref[...]
    softplus = jnp.maximum(neg_lam, 0.0) + jnp.log1p(jnp.exp(-jnp.abs(neg_lam)))
    log_a = -RG_C * r * softplus
    a = jnp.exp(log_a)
    u = jnp.sqrt(jnp.maximum(1.0 - jnp.exp(2.0 * log_a), 0.0)) * (ig * xc)

    in_group = lax.broadcasted_iota(jnp.int32, a.shape, 0) & (SUBLANES - 1)
    big_a, big_b = a, u
    k = 1
    while k < SUBLANES:
        a_sh = jnp.where(in_group >= k, pltpu.roll(big_a, k, axis=0), 1.0)
        b_sh = jnp.where(in_group >= k, pltpu.roll(big_b, k, axis=0), 0.0)
        big_b = big_a * b_sh + big_b
        big_a = big_a * a_sh
        k *= 2
    carry = h_ref[0:1, :]
    groups = []
    for r0 in range(0, rows, SUBLANES):
        h_grp = big_a[r0:r0 + SUBLANES] * carry + big_b[r0:r0 + SUBLANES]
        groups.append(h_grp)
        carry = h_grp[SUBLANES - 1:SUBLANES]
    h = jnp.concatenate(groups, axis=0)
    h_ref[0:1, :] = carry
    o_ref[...] = h * jax.nn.gelu(gate_ref[...], approximate=True)


def rglru(p3, conv_w, conv_b, w_a, b_a, w_x, b_x, lam, l, tt):
    bsz, seq, _ = p3.shape
    tt = min(tt, seq)
    assert seq % tt == 0
    c = D_GROUP
    vec = lambda: pl.BlockSpec((None, 1, c), lambda b, t: (l, 0, 0))
    return pl.pallas_call(
        _rglru_kernel,
        grid=(bsz, seq // tt),
        in_specs=[
            pl.BlockSpec((None, tt, c), lambda b, t: (b, t, 3)),
            pl.BlockSpec((None, tt, c), lambda b, t: (b, t, 4)),
            pl.BlockSpec((None, RG_CONV, c), lambda b, t: (l, 0, 0)),
            vec(),
            pl.BlockSpec((None, c // HEAD_DIM, HEAD_DIM, HEAD_DIM), lambda b, t: (l, 0, 0, 0)),
            vec(),
            pl.BlockSpec((None, c // HEAD_DIM, HEAD_DIM, HEAD_DIM), lambda b, t: (l, 0, 0, 0)),
            vec(),
            vec(),
        ],
        out_specs=pl.BlockSpec((None, tt, c), lambda b, t: (b, t, 0)),
        out_shape=jax.ShapeDtypeStruct((bsz, seq, c), F32),
        scratch_shapes=[pltpu.VMEM((SUBLANES, c), F32), pltpu.VMEM((SUBLANES, c), F32)],
        compiler_params=_params("parallel", "arbitrary"),
        name="rglru",
    )(p3, p3, conv_w, conv_b, w_a, b_a, w_x, b_x, lam)


def _conformer_kernel(val_ref, gate_ref, cw_ref, cb_ref, ng_ref, nb_ref, o_ref, halo_ref):
    t = pl.program_id(1)
    rows = val_ref.shape[0]

    @pl.when(t == 0)
    def _():
        halo_ref[...] = jnp.zeros(halo_ref.shape, F32)

    glu = val_ref[...] * jax.nn.sigmoid(gate_ref[...])
    ext = jnp.concatenate([halo_ref[...], glu], axis=0)
    halo_ref[...] = glu[rows - CV_HALO:]
    rolled = [ext] + [pltpu.roll(ext, r, axis=0) for r in range(1, SUBLANES)]
    u = jnp.zeros_like(glu) + cb_ref[...]
    for j in range(CV_WIDTH):
        d = CV_WIDTH - 1 - j
        r = d % SUBLANES
        start = CV_HALO - (d - r)
        u = u + cw_ref[j:j + 1, :] * rolled[r][start:start + rows]
    parts = []
    for g in range(D_GROUP // HEAD_DIM):
        parts.append(_standardize(u[:, g * HEAD_DIM:(g + 1) * HEAD_DIM]))
    un = jnp.concatenate(parts, axis=-1)
    o_ref[...] = _silu(un * ng_ref[...] + nb_ref[...])


def conformer(p3, cv_w, cv_b, ng, nb, l, tt):
    bsz, seq, _ = p3.shape
    tt = min(tt, seq)
    assert seq % tt == 0 and tt >= CV_HALO
    c = D_GROUP
    vec = lambda: pl.BlockSpec((None, 1, c), lambda b, t: (l, 0, 0))
    return pl.pallas_call(
        _conformer_kernel,
        grid=(bsz, seq // tt),
        in_specs=[
            pl.BlockSpec((None, tt, c), lambda b, t: (b, t, 5)),
            pl.BlockSpec((None, tt, c), lambda b, t: (b, t, 6)),
            pl.BlockSpec((None, CV_WIDTH, c), lambda b, t: (l, 0, 0)),
            vec(), vec(), vec(),
        ],
        out_specs=pl.BlockSpec((None, tt, c), lambda b, t: (b, t, 0)),
        out_shape=jax.ShapeDtypeStruct((bsz, seq, c), F32),
        scratch_shapes=[pltpu.VMEM((CV_HALO, c), F32)],
        compiler_params=_params("parallel", "arbitrary"),
        name="conformer",
    )(p3, p3, cv_w, cv_b, ng, nb)


def _hgrn2_chunk(q, kk, v, lf2, state_t):
    c = HG_CHUNK
    sub = HG_SUB
    row = lax.broadcasted_iota(jnp.int32, (c, HEAD_DIM), 0)
    b = lf2
    k = 1
    while k < c:
        b = b + jnp.where(row >= k, pltpu.roll(b, k, axis=0), 0.0)
        k *= 2
    vb = v.astype(BF16)

    o = lax.dot_general((q * jnp.exp2(b)).astype(BF16), state_t.astype(BF16), NT_DIMS,
                        preferred_element_type=F32)

    lane_s = lax.broadcasted_iota(jnp.int32, (sub, c), 1)
    row_s = lax.broadcasted_iota(jnp.int32, (sub, c), 0)
    att_rows = []
    for blk in range(c // sub):
        lo = blk * sub
        q_i = q[lo:lo + sub]
        b_i = b[lo:lo + sub]
        k_i = kk[lo:lo + sub]
        att = jnp.zeros((sub, c), F32)
        if blk > 0:
            ref = b[lo - 1:lo]
            qs = (q_i * jnp.exp2(b_i - ref)).astype(BF16)
            ks = kk[:lo] * jnp.exp2(ref - b[:lo])
            ks = jnp.concatenate([ks, jnp.zeros((c - lo, HEAD_DIM), F32)], axis=0).astype(BF16)
            att = lax.dot_general(qs, ks, NT_DIMS, preferred_element_type=F32)
        for s in range(sub):
            decay = jnp.exp2(jnp.minimum(b_i - b_i[s:s + 1], 0.0))
            w = jnp.sum(q_i * k_i[s:s + 1] * decay, axis=-1, keepdims=True)
            att = att + jnp.where((lane_s == lo + s) & (row_s >= s), w, 0.0)
        att_rows.append(att)
    att = jnp.concatenate(att_rows, axis=0)
    o = o + jnp.dot(att.astype(BF16), vb, preferred_element_type=F32)

    b_last = b[c - 1:c]
    ks = (kk * jnp.exp2(b_last - b)).astype(BF16)
    new_state_t = state_t * jnp.exp2(b_last) + lax.dot_general(vb, ks, TN_DIMS, preferred_element_type=F32)
    return o, new_state_t


def _hgrn2_kernel(q_ref, f_ref, v_ref, g_ref, lbp_ref, ng_ref, o_ref, state_ref, *, layer):
    t = pl.program_id(1)
    rows = q_ref.shape[0]

    @pl.when(t == 0)
    def _():
        state_ref[...] = jnp.zeros(state_ref.shape, F32)

    lbp = lbp_ref[...]
    e = jnp.exp(lbp - jnp.max(lbp, axis=0, keepdims=True))
    sm = e / jnp.sum(e, axis=0, keepdims=True)
    lb = jnp.sum(sm[0:layer + 1], axis=0, keepdims=True) - sm[0:1]

    for h in range(N_HEADS):
        cs = slice(h * HEAD_DIM, (h + 1) * HEAD_DIM)
        lb_h = lb[:, cs]
        state_t = state_ref[h]
        for ci in range(rows // HG_CHUNK):
            rs = slice(ci * HG_CHUNK, (ci + 1) * HG_CHUNK)
            sig = jax.nn.sigmoid(f_ref[rs, cs])
            lf2 = jnp.log(lb_h + (1.0 - lb_h) * sig) * LOG2_E
            kk = (1.0 - lb_h) * (1.0 - sig)
            o, state_t = _hgrn2_chunk(q_ref[rs, cs], kk, v_ref[rs, cs], lf2, state_t)
            o = o * lax.rsqrt(jnp.mean(o * o, axis=-1, keepdims=True) + LN_EPS)
            o_ref[rs, cs] = o * ng_ref[:, cs] * _silu(g_ref[rs, cs])
        state_ref[h] = state_t


def hgrn2(p3, lower_bound_params, norm_g, l, tt):
    bsz, seq, _ = p3.shape
    tt = min(tt, seq)
    assert seq % tt == 0 and tt % HG_CHUNK == 0
    c = D_GROUP
    depth = lower_bound_params.shape[0]
    tok = lambda col: pl.BlockSpec((None, tt, c), lambda b, t: (b, t, col))
    return pl.pallas_call(
        functools.partial(_hgrn2_kernel, layer=l),
        grid=(bsz, seq // tt),
        in_specs=[
            tok(7), tok(8), tok(9), tok(10),
            pl.BlockSpec((depth, c), lambda b, t: (0, 0)),
            pl.BlockSpec((None, 1, c), lambda b, t: (l, 0, 0)),
        ],
        out_specs=pl.BlockSpec((None, tt, c), lambda b, t: (b, t, 0)),
        out_shape=jax.ShapeDtypeStruct((bsz, seq, c), F32),
        scratch_shapes=[pltpu.VMEM((N_HEADS, HEAD_DIM, HEAD_DIM), F32)],
        compiler_params=_params("parallel", "arbitrary"),
        name="hgrn2",
    )(p3, p3, p3, p3, lower_bound_params, norm_g)


def _out_proj_ln_kernel(x_ref, ya_ref, yb_ref, yc_ref, yd_ref, w_ref, g_ref, b_ref, o_ref, *, alpha):
    c = D_GROUP
    rows = x_ref.shape[0]
    step = min(rows, OUT_PROJ_ROWS)
    for r0 in range(0, rows, step):
        rs = slice(r0, r0 + step)
        y = jnp.dot(ya_ref[rs, :].astype(BF16), w_ref[0:c, :], preferred_element_type=F32)
        y = y + jnp.dot(yb_ref[rs, :].astype(BF16), w_ref[c:2 * c, :], preferred_element_type=F32)
        y = y + jnp.dot(yc_ref[rs, :].astype(BF16), w_ref[2 * c:3 * c, :], preferred_element_type=F32)
        y = y + jnp.dot(yd_ref[rs, :].astype(BF16), w_ref[3 * c:4 * c, :], preferred_element_type=F32)
        z = alpha * x_ref[rs, :] + y
        o_ref[rs, :] = _standardize(z) * g_ref[...] + b_ref[...]


def out_proj_ln(x, ya, yb, yc, yd, w_out, ln_g, ln_b, l, jn, alpha, tm):
    m, d = x.shape
    c = D_GROUP
    tm = min(tm, m)
    assert m % tm == 0
    mix = lambda: pl.BlockSpec((tm, c), lambda i: (i, 0))
    return pl.pallas_call(
        functools.partial(_out_proj_ln_kernel, alpha=alpha),
        grid=(m // tm,),
        in_specs=[
            pl.BlockSpec((tm, d), lambda i: (i, 0)),
            mix(), mix(), mix(), mix(),
            pl.BlockSpec((None, 4 * c, d), lambda i: (l, 0, 0)),
            pl.BlockSpec((None, 1, d), lambda i: (jn, 0, 0)),
            pl.BlockSpec((None, 1, d), lambda i: (jn, 0, 0)),
        ],
        out_specs=pl.BlockSpec((tm, d), lambda i: (i, 0)),
        out_shape=jax.ShapeDtypeStruct((m, d), F32),
        compiler_params=_params("parallel"),
        name="out_proj_ln",
    )(x, ya, yb, yc, yd, w_out, ln_g, ln_b)


def kernel(x, ln_g, ln_b, ffn_w_gate, ffn_w_up, ffn_w_down, w_in, w_out, rg_conv_w, rg_conv_b,
           rg_w_a, rg_b_a, rg_w_x, rg_b_x, rg_lambda, cv_w, cv_b, cv_ln_g, cv_ln_b,
           hg_lower_bounds, hg_norm_g):
    bsz, seq, d = x.shape
    depth = w_in.shape[0]
    alpha = (2 * depth) ** 0.25
    m = bsz * seq

    wg = ffn_w_gate.astype(BF16)
    wu = ffn_w_up.astype(BF16)
    wd = ffn_w_down.astype(BF16)
    w_in_b = w_in.astype(BF16)
    w_out_b = w_out.astype(BF16)
    ln_g3 = ln_g.reshape(depth * 3, 1, d)
    ln_b3 = ln_b.reshape(depth * 3, 1, d)
    row = lambda a: a.reshape(depth, 1, a.shape[-1])

    h = x.reshape(m, d)
    for l in range(depth):
        h = ffn_ln(h, wg, wu, wd, ln_g3, ln_b3, l, 0, 3 * l, alpha, tm=1024, tf=FFN_COLS)
        p3 = in_proj(h, w_in_b, l, tm=1024, tn=IN_PROJ_COLS).reshape(bsz, seq, -1)
        y_a = moba(p3)
        y_b = rglru(p3, rg_conv_w, row(rg_conv_b), rg_w_a, row(rg_b_a), rg_w_x, row(rg_b_x),
                    row(rg_lambda), l, tt=1024)
        y_c = conformer(p3, cv_w, row(cv_b), row(cv_ln_g), row(cv_ln_b), l, tt=1024)
        y_d = hgrn2(p3, hg_lower_bounds, row(hg_norm_g), l, tt=512)
        flat = lambda y: y.reshape(m, D_GROUP)
        h = out_proj_ln(h, flat(y_a), flat(y_b), flat(y_c), flat(y_d), w_out_b, ln_g3, ln_b3,
                        l, 3 * l + 1, alpha, tm=512)
        h = ffn_ln(h, wg, wu, wd, ln_g3, ln_b3, l, 1, 3 * l + 2, alpha, tm=1024, tf=FFN_COLS)
    return h.reshape(bsz, seq, d)
```

```python
import functools

import jax
import jax.numpy as jnp
from jax import lax
from jax.experimental import pallas as pl
from jax.experimental.pallas import tpu as pltpu

F32 = jnp.float32
BF16 = jnp.bfloat16

LANES = 128
SUBLANES = 8
VMEM_LIMIT_BYTES = 56 * 1024 * 1024

D_GROUP = 512
HEAD_DIM = 128
N_HEADS = D_GROUP // HEAD_DIM
MOBA_BLOCK = 256
MOBA_TOPK = 3
MOBA_GROUP = 4
MOBA_Q_BLOCKS = 4
MOBA_Q_PARTS = 2
LOG2_E = 1.4426950408889634
MASK_VALUE = -1e30
RG_C = 8.0
RG_CONV = 4
CV_WIDTH = 31
CV_HALO = 32
FFN_COLS = 512
IN_PROJ_COLS = 1408
OUT_PROJ_ROWS = 256
FFN_DOWN_COLS = 512
HG_CHUNK = 64
HG_SUB = 8
LN_EPS = 1e-5

NT_DIMS = (((1,), (1,)), ((), ()))
TN_DIMS = (((0,), (0,)), ((), ()))


def _params(*sem):
    return pltpu.CompilerParams(dimension_semantics=sem, vmem_limit_bytes=VMEM_LIMIT_BYTES)


def _standardize(z):
    mu = jnp.mean(z, axis=-1, keepdims=True)
    zc = z - mu
    var = jnp.mean(zc * zc, axis=-1, keepdims=True)
    return zc * lax.rsqrt(var + LN_EPS)


def _silu(z):
    return z * jax.nn.sigmoid(z)


def _shift_rows(ext, d, halo, rows):
    if d % SUBLANES == 0:
        return ext[halo - d:halo - d + rows]
    r = d % SUBLANES
    base = d - r
    rolled = pltpu.roll(ext, r, axis=0)
    return rolled[halo - base:halo - base + rows]


def _ffn_ln_kernel(x_ref, wg_ref, wu_ref, wd_ref, g_ref, b_ref, o_ref, xb_ref, *, alpha):
    f = pl.program_id(1)

    @pl.when(f == 0)
    def _():
        xb_ref[...] = x_ref[...].astype(BF16)
        o_ref[...] = jnp.zeros(o_ref.shape, F32)

    xb = xb_ref[...]
    hg = jnp.dot(xb, wg_ref[...], preferred_element_type=F32)
    hu = jnp.dot(xb, wu_ref[...], preferred_element_type=F32)
    h = (_silu(hg) * hu).astype(BF16)
    d = o_ref.shape[1]
    for c0 in range(0, d, FFN_DOWN_COLS):
        cs = slice(c0, min(c0 + FFN_DOWN_COLS, d))
        o_ref[:, cs] += jnp.dot(h, wd_ref[:, cs], preferred_element_type=F32)

    @pl.when(f == pl.num_programs(1) - 1)
    def _():
        z = alpha * x_ref[...] + 0.5 * o_ref[...]
        o_ref[...] = _standardize(z) * g_ref[...] + b_ref[...]


def ffn_ln(x, wg, wu, wd, ln_g, ln_b, l, j, jn, alpha, tm, tf):
    m, d = x.shape
    dff = wg.shape[-1]
    tm = min(tm, m)
    tf = min(tf, dff)
    assert m % tm == 0 and dff % tf == 0
    grid = (m // tm, dff // tf)
    return pl.pallas_call(
        functools.partial(_ffn_ln_kernel, alpha=alpha),
        grid=grid,
        in_specs=[
            pl.BlockSpec((tm, d), lambda i, f: (i, 0), pipeline_mode=pl.Buffered(1)),
            pl.BlockSpec((None, None, d, tf), lambda i, f: (l, j, 0, f)),
            pl.BlockSpec((None, None, d, tf), lambda i, f: (l, j, 0, f)),
            pl.BlockSpec((None, None, tf, d), lambda i, f: (l, j, f, 0)),
            pl.BlockSpec((None, 1, d), lambda i, f: (jn, 0, 0)),
            pl.BlockSpec((None, 1, d), lambda i, f: (jn, 0, 0)),
        ],
        out_specs=pl.BlockSpec((tm, d), lambda i, f: (i, 0)),
        out_shape=jax.ShapeDtypeStruct((m, d), F32),
        scratch_shapes=[pltpu.VMEM((tm, d), BF16)],
        compiler_params=_params("parallel", "arbitrary"),
        name="ffn_ln",
    )(x, wg, wu, wd, ln_g, ln_b)


def _in_proj_kernel(x_ref, w_ref, o_ref, xb_ref):
    @pl.when(pl.program_id(1) == 0)
    def _():
        xb_ref[...] = x_ref[...].astype(BF16)

    o_ref[...] = jnp.dot(xb_ref[...], w_ref[...], preferred_element_type=F32)


def in_proj(x, w_in, l, tm, tn):
    m, d = x.shape
    n = w_in.shape[-1]
    tm = min(tm, m)
    assert m % tm == 0 and n % tn == 0
    return pl.pallas_call(
        _in_proj_kernel,
        grid=(m // tm, n // tn),
        in_specs=[
            pl.BlockSpec((tm, d), lambda i, c: (i, 0)),
            pl.BlockSpec((None, d, tn), lambda i, c: (l, 0, c)),
        ],
        out_specs=pl.BlockSpec((tm, tn), lambda i, c: (i, c)),
        out_shape=jax.ShapeDtypeStruct((m, n), F32),
        scratch_shapes=[pltpu.VMEM((tm, d), BF16)],
        compiler_params=_params("parallel", "arbitrary"),
        name="in_proj",
    )(x, w_in)


def _fold_rows_max(a):
    rows, cols = a.shape
    return jnp.max(a.reshape(rows // SUBLANES, SUBLANES, cols), axis=0)


def _moba_kernel(q_ref, k_ref, v_ref, o_ref, qat_ref, ka_ref, vat_ref, vat3_ref, s_ref, so_ref, *, nb, scale):
    j = pl.program_id(2)
    blk = MOBA_BLOCK
    grp = MOBA_GROUP
    dh = HEAD_DIM
    seq = nb * blk
    tq = MOBA_Q_BLOCKS * blk
    span = grp * blk
    nbp = -(-nb // SUBLANES) * SUBLANES

    @pl.when(j == 0)
    def _():
        k = k_ref[...]
        lane = lax.broadcasted_iota(jnp.int32, (seq, LANES), 1)
        tile = lax.broadcasted_iota(jnp.int32, (nb, blk, LANES), 0).reshape(seq, LANES)
        ka_ref[:, 0:dh] = k.astype(BF16)
        ka_ref[:, dh:2 * dh] = jnp.where(tile == lane, 1.0, 0.0).astype(BF16)
        vat = jnp.concatenate([v_ref[...].T, jnp.ones((dh, seq), F32)], axis=0).astype(BF16)
        vat_ref[...] = vat
        for n in range(nb):
            vat3_ref[n] = vat[:, n * blk:(n + 1) * blk]
        k_mean = jnp.mean(k.reshape(nb, blk, dh), axis=1)
        k_mean = jnp.concatenate([k_mean, jnp.zeros((LANES - nb, dh), F32)], axis=0)

        q = q_ref[...]
        gate = lax.dot_general(k_mean.astype(BF16), q.astype(BF16), NT_DIMS, preferred_element_type=F32)
        g = gate[0:nbp]
        blk_row = lax.broadcasted_iota(jnp.int32, (nbp, seq), 0)
        q_blk = lax.shift_right_logical(lax.broadcasted_iota(jnp.int32, (nbp, seq), 1), blk.bit_length() - 1)
        row_f = blk_row.astype(F32)
        past = blk_row < q_blk
        g = jnp.where(past, g, MASK_VALUE)
        sel = jnp.zeros((nbp, seq), F32)
        for _ in range(MOBA_TOPK):
            top = jnp.max(g, axis=0, keepdims=True)
            first = jnp.min(jnp.where(g == top, row_f, float(LANES)), axis=0, keepdims=True)
            pick = row_f == first
            sel = jnp.where(pick, 1.0, sel)
            g = jnp.where(pick, -jnp.inf, g)
        bias = jnp.where(past & (sel > 0.5), 0.0, MASK_VALUE)
        qt = (q * (scale * LOG2_E)).T
        qat = jnp.concatenate([qt, bias, jnp.zeros((dh - nbp, seq), F32)], axis=0).astype(BF16)
        for t in range(seq // tq):
            qat_ref[t] = qat[:, t * tq:(t + 1) * tq]

    qat = qat_ref[j]
    first_blk = j * MOBA_Q_BLOCKS

    row = lax.broadcasted_iota(jnp.int32, (blk, blk), 0)
    col = lax.broadcasted_iota(jnp.int32, (blk, blk), 1)
    own_max = []
    for c in range(MOBA_Q_BLOCKS):
        start = pl.multiple_of((first_blk + c) * blk, blk)
        s = jnp.dot(ka_ref[pl.ds(start, blk), 0:dh], qat[0:dh, c * blk:(c + 1) * blk],
                    preferred_element_type=F32)
        s = jnp.where(row <= col, s, MASK_VALUE)
        so_ref[c] = s
        own_max.append(_fold_rows_max(s))
    mrun0 = jnp.concatenate(own_max, axis=1)

    dense_blocks = first_blk + MOBA_Q_BLOCKS - 1
    n_groups = (dense_blocks + grp - 1) // grp

    def attend(count):
        width = tq // MOBA_Q_PARTS
        own_per_part = MOBA_Q_BLOCKS // MOBA_Q_PARTS
        maxes = []
        for part in range(MOBA_Q_PARTS):
            qs = slice(part * width, (part + 1) * width)
            mrun = mrun0[:, qs]
            for g in range(count):
                s = jnp.dot(ka_ref[g * span:(g + 1) * span, :], qat[:, qs], preferred_element_type=F32)
                s_ref[g, :, qs] = s
                mrun = jnp.maximum(mrun, _fold_rows_max(s))
            maxes.append(jnp.max(mrun, axis=0, keepdims=True))
        outs = []
        for part in range(MOBA_Q_PARTS):
            qs = slice(part * width, (part + 1) * width)
            m = maxes[part]
            cols = []
            for c in range(own_per_part):
                p = jnp.exp2((so_ref[part * own_per_part + c] - m[:, c * blk:(c + 1) * blk]).astype(BF16))
                cols.append(jnp.dot(vat3_ref[first_blk + part * own_per_part + c], p,
                                    preferred_element_type=F32))
            acc = jnp.concatenate(cols, axis=1)
            for g in range(count):
                p = jnp.exp2((s_ref[g, :, qs] - m).astype(BF16))
                acc = acc + jnp.dot(vat_ref[:, g * span:(g + 1) * span], p, preferred_element_type=F32)
            outs.append(acc[0:dh] / acc[dh:2 * dh])
        o_ref[...] = jnp.concatenate(outs, axis=1).T

    min_count = (MOBA_Q_BLOCKS - 1 + grp - 1) // grp
    max_count = (nb - 1 + grp - 1) // grp
    for count in range(min_count, max_count + 1):
        pl.when(n_groups == count)(functools.partial(attend, count))


def moba(p3):
    bsz, seq, _ = p3.shape
    blk = MOBA_BLOCK
    grp = MOBA_GROUP
    tq = MOBA_Q_BLOCKS * blk
    assert seq % (blk * grp) == 0 and seq % tq == 0
    nb = seq // blk
    assert nb <= HEAD_DIM
    max_count = (nb - 1 + grp - 1) // grp
    kernel = functools.partial(_moba_kernel, nb=nb, scale=HEAD_DIM ** -0.5)
    col = lambda c: pl.BlockSpec((None, seq, HEAD_DIM), lambda b, h, j: (b, 0, c * N_HEADS + h))
    return pl.pallas_call(
        kernel,
        grid=(bsz, N_HEADS, seq // tq),
        in_specs=[col(0), col(1), col(2)],
        out_specs=pl.BlockSpec((None, tq, HEAD_DIM), lambda b, h, j: (b, j, h)),
        out_shape=jax.ShapeDtypeStruct((bsz, seq, D_GROUP), F32),
        scratch_shapes=[
            pltpu.VMEM((seq // tq, 2 * HEAD_DIM, tq), BF16),
            pltpu.VMEM((seq, 2 * HEAD_DIM), BF16),
            pltpu.VMEM((2 * HEAD_DIM, seq), BF16),
            pltpu.VMEM((nb, 2 * HEAD_DIM, blk), BF16),
            pltpu.VMEM((max_count, grp * blk, tq), F32),
            pltpu.VMEM((MOBA_Q_BLOCKS, blk, blk), F32),
        ],
        compiler_params=_params("parallel", "parallel", "arbitrary"),
        name="moba",
    )(p3, p3, p3)


def _rglru_kernel(gate_ref, x_ref, cw_ref, cb_ref, wa_ref, ba_ref, wx_ref, bx_ref, lam_ref,
                  o_ref, halo_ref, h_ref):
    t = pl.program_id(1)
    rows = x_ref.shape[0]

    @pl.when(t == 0)
    def _():
        halo_ref[...] = jnp.zeros(halo_ref.shape, F32)
        h_ref[...] = jnp.zeros(h_ref.shape, F32)

    x = x_ref[...]
    ext = jnp.concatenate([halo_ref[...], x], axis=0)
    halo_ref[...] = x[rows - SUBLANES:]
    xc = jnp.zeros_like(x) + cb_ref[...]
    for j in range(RG_CONV):
        xc = xc + cw_ref[j:j + 1, :] * _shift_rows(ext, RG_CONV - 1 - j, SUBLANES, rows)

    xcb = xc.astype(BF16)
    r_parts, i_parts = [], []
    for g in range(D_GROUP // HEAD_DIM):
        blk = xcb[:, g * HEAD_DIM:(g + 1) * HEAD_DIM]
        r_parts.append(jnp.dot(blk, wa_ref[g].astype(BF16), preferred_element_type=F32))
        i_parts.append(jnp.dot(blk, wx_ref[g].astype(BF16), preferred_element_type=F32))
    r = jax.nn.sigmoid(jnp.concatenate(r_parts, axis=-1) + ba_ref[...])
    ig = jax.nn.sigmoid(jnp.concatenate(i_parts, axis=-1) + bx_ref[...])
    neg_lam = -lam_ref[...]
    softplus = jnp.maximum(neg_lam, 0.0) + jnp.log1p(jnp.exp(-jnp.abs(neg_lam)))
    log_a = -RG_C * r * softplus
    a = jnp.exp(log_a)
    u = jnp.sqrt(jnp.maximum(1.0 - jnp.exp(2.0 * log_a), 0.0)) * (ig * xc)

    in_group = lax.broadcasted_iota(jnp.int32, a.shape, 0) & (SUBLANES - 1)
    big_a, big_b = a, u
    k = 1
    while k < SUBLANES:
        a_sh = jnp.where(in_group >= k, pltpu.roll(big_a, k, axis=0), 1.0)
        b_sh = jnp.where(in_group >= k, pltpu.roll(big_b, k, axis=0), 0.0)
        big_b = big_a * b_sh + big_b
        big_a = big_a * a_sh
        k *= 2
    carry = h_ref[0:1, :]
    groups = []
    for r0 in range(0, rows, SUBLANES):
        h_grp = big_a[r0:r0 + SUBLANES] * carry + big_b[r0:r0 + SUBLANES]
        groups.append(h_grp)
        carry = h_grp[SUBLANES - 1:SUBLANES]
    h = jnp.concatenate(groups, axis=0)
    h_ref[0:1, :] = carry
    o_ref[...] = h * jax.nn.gelu(gate_ref[...], approximate=True)


def rglru(p3, conv_w, conv_b, w_a, b_a, w_x, b_x, lam, l, tt):
    bsz, seq, _ = p3.shape
    tt = min(tt, seq)
    assert seq % tt == 0
    c = D_GROUP
    vec = lambda: pl.BlockSpec((None, 1, c), lambda b, t: (l, 0, 0))
    return pl.pallas_call(
        _rglru_kernel,
        grid=(bsz, seq // tt),
        in_specs=[
            pl.BlockSpec((None, tt, c), lambda b, t: (b, t, 3)),
            pl.BlockSpec((None, tt, c), lambda b, t: (b, t, 4)),
            pl.BlockSpec((None, RG_CONV, c), lambda b, t: (l, 0, 0)),
            vec(),
            pl.BlockSpec((None, c // HEAD_DIM, HEAD_DIM, HEAD_DIM), lambda b, t: (l, 0, 0, 0)),
            vec(),
            pl.BlockSpec((None, c // HEAD_DIM, HEAD_DIM, HEAD_DIM), lambda b, t: (l, 0, 0, 0)),
            vec(),
            vec(),
        ],
        out_specs=pl.BlockSpec((None, tt, c), lambda b, t: (b, t, 0)),
        out_shape=jax.ShapeDtypeStruct((bsz, seq, c), F32),
        scratch_shapes=[pltpu.VMEM((SUBLANES, c), F32), pltpu.VMEM((SUBLANES, c), F32)],
        compiler_params=_params("parallel", "arbitrary"),
        name="rglru",
    )(p3, p3, conv_w, conv_b, w_a, b_a, w_x, b_x, lam)


def _conformer_units(val_ref, gate_ref, cw_ref, cb_ref, ng_ref, nb_ref, o_ref, halo_ref):
    rows = val_ref.shape[0]

    def unit(g):
        cs = slice(g * HEAD_DIM, (g + 1) * HEAD_DIM)
        glu = val_ref[:, cs] * jax.nn.sigmoid(gate_ref[:, cs])
        ext = jnp.concatenate([halo_ref[:, cs], glu], axis=0)
        halo_ref[:, cs] = glu[rows - CV_HALO:]
        rolled = [ext] + [pltpu.roll(ext, r, axis=0) for r in range(1, SUBLANES)]
        u = jnp.zeros_like(glu) + cb_ref[:, cs]
        for j in range(CV_WIDTH):
            d = CV_WIDTH - 1 - j
            r = d % SUBLANES
            start = CV_HALO - (d - r)
            u = u + cw_ref[j:j + 1, cs] * rolled[r][start:start + rows]
        o_ref[:, cs] = _silu(_standardize(u) * ng_ref[:, cs] + nb_ref[:, cs]).astype(o_ref.dtype)

    return [functools.partial(unit, g) for g in range(D_GROUP // HEAD_DIM)]


def _conformer_kernel(val_ref, gate_ref, cw_ref, cb_ref, ng_ref, nb_ref, o_ref, halo_ref):
    @pl.when(pl.program_id(1) == 0)
    def _():
        halo_ref[...] = jnp.zeros(halo_ref.shape, F32)

    for unit in _conformer_units(val_ref, gate_ref, cw_ref, cb_ref, ng_ref, nb_ref, o_ref, halo_ref):
        unit()


def conformer(p3, cv_w, cv_b, ng, nb, l, tt):
    bsz, seq, _ = p3.shape
    tt = min(tt, seq)
    assert seq % tt == 0 and tt >= CV_HALO
    c = D_GROUP
    vec = lambda: pl.BlockSpec((None, 1, c), lambda b, t: (l, 0, 0))
    return pl.pallas_call(
        _conformer_kernel,
        grid=(bsz, seq // tt),
        in_specs=[
            pl.BlockSpec((None, tt, c), lambda b, t: (b, t, 5)),
            pl.BlockSpec((None, tt, c), lambda b, t: (b, t, 6)),
            pl.BlockSpec((None, CV_WIDTH, c), lambda b, t: (l, 0, 0)),
            vec(), vec(), vec(),
        ],
        out_specs=pl.BlockSpec((None, tt, c), lambda b, t: (b, t, 0)),
        out_shape=jax.ShapeDtypeStruct((bsz, seq, c), F32),
        scratch_shapes=[pltpu.VMEM((CV_HALO, c), F32)],
        compiler_params=_params("parallel", "arbitrary"),
        name="conformer",
    )(p3, p3, cv_w, cv_b, ng, nb)


def _hgrn2_chunk(q, kk, v, lf2, state_t):
    c = HG_CHUNK
    sub = HG_SUB
    row = lax.broadcasted_iota(jnp.int32, (c, HEAD_DIM), 0)
    b = lf2
    k = 1
    while k < c:
        b = b + jnp.where(row >= k, pltpu.roll(b, k, axis=0), 0.0)
        k *= 2
    vb = v.astype(BF16)

    o = lax.dot_general((q * jnp.exp2(b)).astype(BF16), state_t.astype(BF16), NT_DIMS,
                        preferred_element_type=F32)

    lane_s = lax.broadcasted_iota(jnp.int32, (sub, c), 1)
    row_s = lax.broadcasted_iota(jnp.int32, (sub, c), 0)
    att_rows = []
    for blk in range(c // sub):
        lo = blk * sub
        q_i = q[lo:lo + sub]
        b_i = b[lo:lo + sub]
        k_i = kk[lo:lo + sub]
        att = jnp.zeros((sub, c), F32)
        if blk > 0:
            ref = b[lo - 1:lo]
            qs = (q_i * jnp.exp2(b_i - ref)).astype(BF16)
            ks = kk[:lo] * jnp.exp2(ref - b[:lo])
            ks = jnp.concatenate([ks, jnp.zeros((c - lo, HEAD_DIM), F32)], axis=0).astype(BF16)
            att = lax.dot_general(qs, ks, NT_DIMS, preferred_element_type=F32)
        for s in range(sub):
            decay = jnp.exp2(jnp.minimum(b_i - b_i[s:s + 1], 0.0))
            w = jnp.sum(q_i * k_i[s:s + 1] * decay, axis=-1, keepdims=True)
            att = att + jnp.where((lane_s == lo + s) & (row_s >= s), w, 0.0)
        att_rows.append(att)
    att = jnp.concatenate(att_rows, axis=0)
    o = o + jnp.dot(att.astype(BF16), vb, preferred_element_type=F32)

    b_last = b[c - 1:c]
    ks = (kk * jnp.exp2(b_last - b)).astype(BF16)
    new_state_t = state_t * jnp.exp2(b_last) + lax.dot_general(vb, ks, TN_DIMS, preferred_element_type=F32)
    return o, new_state_t


def _hgrn2_units(q_ref, f_ref, v_ref, g_ref, lbp_ref, ng_ref, o_ref, state_ref, layer):
    rows = q_ref.shape[0]
    n_chunks = rows // HG_CHUNK

    lbp = lbp_ref[...]
    e = jnp.exp(lbp - jnp.max(lbp, axis=0, keepdims=True))
    sm = e / jnp.sum(e, axis=0, keepdims=True)
    lb = jnp.sum(sm[0:layer + 1], axis=0, keepdims=True) - sm[0:1]
    states = [state_ref[h] for h in range(N_HEADS)]

    def unit(ci, h):
        cs = slice(h * HEAD_DIM, (h + 1) * HEAD_DIM)
        rs = slice(ci * HG_CHUNK, (ci + 1) * HG_CHUNK)
        lb_h = lb[:, cs]
        sig = jax.nn.sigmoid(f_ref[rs, cs])
        lf2 = jnp.log(lb_h + (1.0 - lb_h) * sig) * LOG2_E
        kk = (1.0 - lb_h) * (1.0 - sig)
        o, states[h] = _hgrn2_chunk(q_ref[rs, cs], kk, v_ref[rs, cs], lf2, states[h])
        o = o * lax.rsqrt(jnp.mean(o * o, axis=-1, keepdims=True) + LN_EPS)
        o_ref[rs, cs] = (o * ng_ref[:, cs] * _silu(g_ref[rs, cs])).astype(o_ref.dtype)
        if ci == n_chunks - 1:
            state_ref[h] = states[h]

    return [functools.partial(unit, ci, h) for ci in range(n_chunks) for h in range(N_HEADS)]


def _hgrn2_kernel(q_ref, f_ref, v_ref, g_ref, lbp_ref, ng_ref, o_ref, state_ref, *, layer):
    @pl.when(pl.program_id(1) == 0)
    def _():
        state_ref[...] = jnp.zeros(state_ref.shape, F32)

    for unit in _hgrn2_units(q_ref, f_ref, v_ref, g_ref, lbp_ref, ng_ref, o_ref, state_ref, layer):
        unit()


def hgrn2(p3, lower_bound_params, norm_g, l, tt):
    bsz, seq, _ = p3.shape
    tt = min(tt, seq)
    assert seq % tt == 0 and tt % HG_CHUNK == 0
    c = D_GROUP
    depth = lower_bound_params.shape[0]
    tok = lambda col: pl.BlockSpec((None, tt, c), lambda b, t: (b, t, col))
    return pl.pallas_call(
        functools.partial(_hgrn2_kernel, layer=l),
        grid=(bsz, seq // tt),
        in_specs=[
            tok(7), tok(8), tok(9), tok(10),
            pl.BlockSpec((depth, c), lambda b, t: (0, 0)),
            pl.BlockSpec((None, 1, c), lambda b, t: (l, 0, 0)),
        ],
        out_specs=pl.BlockSpec((None, tt, c), lambda b, t: (b, t, 0)),
        out_shape=jax.ShapeDtypeStruct((bsz, seq, c), F32),
        scratch_shapes=[pltpu.VMEM((N_HEADS, HEAD_DIM, HEAD_DIM), F32)],
        compiler_params=_params("parallel", "arbitrary"),
        name="hgrn2",
    )(p3, p3, p3, p3, lower_bound_params, norm_g)


def _out_proj_ln_units(x_ref, y_refs, w_ref, g_ref, b_ref, o_ref, alpha):
    c = D_GROUP
    rows = x_ref.shape[0]
    step = min(rows, OUT_PROJ_ROWS)
    units = []
    for r0 in range(0, rows, step):
        rs = slice(r0, r0 + step)
        acc = []

        def matmul(y_ref, k, rs=rs, acc=acc):
            part = jnp.dot(y_ref[rs, :].astype(BF16), w_ref[k * c:(k + 1) * c, :], preferred_element_type=F32)
            acc[:] = [part if not acc else acc[0] + part]

        def norm(rs=rs, acc=acc):
            z = alpha * x_ref[rs, :] + acc[0]
            o_ref[rs, :] = _standardize(z) * g_ref[...] + b_ref[...]

        units += [functools.partial(matmul, y_ref, k) for k, y_ref in enumerate(y_refs)]
        units.append(norm)
    return units


def _out_proj_ln_kernel(x_ref, ya_ref, yb_ref, yc_ref, yd_ref, w_ref, g_ref, b_ref, o_ref, *, alpha):
    for unit in _out_proj_ln_units(x_ref, (ya_ref, yb_ref, yc_ref, yd_ref), w_ref, g_ref, b_ref, o_ref, alpha):
        unit()


def out_proj_ln(x, ya, yb, yc, yd, w_out, ln_g, ln_b, l, jn, alpha, tm):
    m, d = x.shape
    c = D_GROUP
    tm = min(tm, m)
    assert m % tm == 0
    mix = lambda: pl.BlockSpec((tm, c), lambda i: (i, 0))
    return pl.pallas_call(
        functools.partial(_out_proj_ln_kernel, alpha=alpha),
        grid=(m // tm,),
        in_specs=[
            pl.BlockSpec((tm, d), lambda i: (i, 0)),
            mix(), mix(), mix(), mix(),
            pl.BlockSpec((None, 4 * c, d), lambda i: (l, 0, 0)),
            pl.BlockSpec((None, 1, d), lambda i: (jn, 0, 0)),
            pl.BlockSpec((None, 1, d), lambda i: (jn, 0, 0)),
        ],
        out_specs=pl.BlockSpec((tm, d), lambda i: (i, 0)),
        out_shape=jax.ShapeDtypeStruct((m, d), F32),
        compiler_params=_params("parallel"),
        name="out_proj_ln",
    )(x, ya, yb, yc, yd, w_out, ln_g, ln_b)


def kernel(x, ln_g, ln_b, ffn_w_gate, ffn_w_up, ffn_w_down, w_in, w_out, rg_conv_w, rg_conv_b,
           rg_w_a, rg_b_a, rg_w_x, rg_b_x, rg_lambda, cv_w, cv_b, cv_ln_g, cv_ln_b,
           hg_lower_bounds, hg_norm_g):
    bsz, seq, d = x.shape
    depth = w_in.shape[0]
    alpha = (2 * depth) ** 0.25
    m = bsz * seq

    wg = ffn_w_gate.astype(BF16)
    wu = ffn_w_up.astype(BF16)
    wd = ffn_w_down.astype(BF16)
    w_in_b = w_in.astype(BF16)
    w_out_b = w_out.astype(BF16)
    ln_g3 = ln_g.reshape(depth * 3, 1, d)
    ln_b3 = ln_b.reshape(depth * 3, 1, d)
    row = lambda a: a.reshape(depth, 1, a.shape[-1])

    h = x.reshape(m, d)
    for l in range(depth):
        h = ffn_ln(h, wg, wu, wd, ln_g3, ln_b3, l, 0, 3 * l, alpha, tm=1024, tf=FFN_COLS)
        p3 = in_proj(h, w_in_b, l, tm=1024, tn=IN_PROJ_COLS).reshape(bsz, seq, -1)
        y_a = moba(p3)
        y_b = rglru(p3, rg_conv_w, row(rg_conv_b), rg_w_a, row(rg_b_a), rg_w_x, row(rg_b_x),
                    row(rg_lambda), l, tt=1024)
        y_c = conformer(p3, cv_w, row(cv_b), row(cv_ln_g), row(cv_ln_b), l, tt=1024)
        y_d = hgrn2(p3, hg_lower_bounds, row(hg_norm_g), l, tt=512)
        flat = lambda y: y.reshape(m, D_GROUP)
        h = out_proj_ln(h, flat(y_a), flat(y_b), flat(y_c), flat(y_d), w_out_b, ln_g3, ln_b3,
                        l, 3 * l + 1, alpha, tm=512)
        h = ffn_ln(h, wg, wu, wd, ln_g3, ln_b3, l, 1, 3 * l + 2, alpha, tm=1024, tf=FFN_COLS)
    return h.reshape(bsz, seq, d)
```

```python
import functools

import jax
import jax.numpy as jnp
from jax import lax
from jax.experimental import pallas as pl
from jax.experimental.pallas import tpu as pltpu

F32 = jnp.float32
BF16 = jnp.bfloat16

LANES = 128
SUBLANES = 8
VMEM_LIMIT_BYTES = 56 * 1024 * 1024

D_GROUP = 512
HEAD_DIM = 128
N_HEADS = D_GROUP // HEAD_DIM
MOBA_BLOCK = 256
MOBA_TOPK = 3
MOBA_GROUP = 4
MOBA_Q_BLOCKS = 4
MOBA_Q_PARTS = 2
LOG2_E = 1.4426950408889634
MASK_VALUE = -1e30
RG_C = 8.0
RG_CONV = 4
CV_WIDTH = 31
CV_HALO = 32
FFN_COLS = 512
IN_PROJ_COLS = 1408
OUT_PROJ_ROWS = 256
FFN_DOWN_COLS = 512
FFN_NORM_ROWS = 256
HG_CHUNK = 64
HG_SUB = 8
LN_EPS = 1e-5

NT_DIMS = (((1,), (1,)), ((), ()))
TN_DIMS = (((0,), (0,)), ((), ()))


def _params(*sem):
    return pltpu.CompilerParams(dimension_semantics=sem, vmem_limit_bytes=VMEM_LIMIT_BYTES)


def _standardize(z):
    mu = jnp.mean(z, axis=-1, keepdims=True)
    zc = z - mu
    var = jnp.mean(zc * zc, axis=-1, keepdims=True)
    return zc * lax.rsqrt(var + LN_EPS)


def _silu(z):
    return z * jax.nn.sigmoid(z)


def _shift_rows(ext, d, halo, rows):
    if d % SUBLANES == 0:
        return ext[halo - d:halo - d + rows]
    r = d % SUBLANES
    base = d - r
    rolled = pltpu.roll(ext, r, axis=0)
    return rolled[halo - base:halo - base + rows]


def _ffn_ln_kernel(x_ref, wg_ref, wu_ref, wd_ref, g_ref, b_ref, o_ref, xb_ref, *, alpha):
    f = pl.program_id(1)
    last = pl.num_programs(1) - 1
    rows, d = o_ref.shape

    def hidden(xb):
        hg = jnp.dot(xb, wg_ref[...], preferred_element_type=F32)
        hu = jnp.dot(xb, wu_ref[...], preferred_element_type=F32)
        return (_silu(hg) * hu).astype(BF16)

    def down(h, accumulate):
        for c0 in range(0, d, FFN_DOWN_COLS):
            cs = slice(c0, min(c0 + FFN_DOWN_COLS, d))
            part = jnp.dot(h, wd_ref[:, cs], preferred_element_type=F32)
            o_ref[:, cs] = o_ref[:, cs] + part if accumulate else part

    @pl.when(f == 0)
    def _():
        xb = x_ref[...].astype(BF16)
        xb_ref[...] = xb
        down(hidden(xb), accumulate=False)

    @pl.when((f > 0) & (f < last))
    def _():
        down(hidden(xb_ref[...]), accumulate=True)

    @pl.when(f == last)
    def _():
        h = hidden(xb_ref[...])
        for r0 in range(0, rows, FFN_NORM_ROWS):
            rs = slice(r0, min(r0 + FFN_NORM_ROWS, rows))
            acc = o_ref[rs, :] + jnp.dot(h[rs, :], wd_ref[...], preferred_element_type=F32)
            z = alpha * x_ref[rs, :] + 0.5 * acc
            o_ref[rs, :] = _standardize(z) * g_ref[...] + b_ref[...]


def ffn_ln(x, wg, wu, wd, ln_g, ln_b, l, j, jn, alpha, tm, tf):
    m, d = x.shape
    dff = wg.shape[-1]
    tm = min(tm, m)
    tf = min(tf, dff)
    assert m % tm == 0 and dff % tf == 0
    grid = (m // tm, dff // tf)
    return pl.pallas_call(
        functools.partial(_ffn_ln_kernel, alpha=alpha),
        grid=grid,
        in_specs=[
            pl.BlockSpec((tm, d), lambda i, f: (i, 0), pipeline_mode=pl.Buffered(1)),
            pl.BlockSpec((None, None, d, tf), lambda i, f: (l, j, 0, f)),
            pl.BlockSpec((None, None, d, tf), lambda i, f: (l, j, 0, f)),
            pl.BlockSpec((None, None, tf, d), lambda i, f: (l, j, f, 0)),
            pl.BlockSpec((None, 1, d), lambda i, f: (jn, 0, 0)),
            pl.BlockSpec((None, 1, d), lambda i, f: (jn, 0, 0)),
        ],
        out_specs=pl.BlockSpec((tm, d), lambda i, f: (i, 0)),
        out_shape=jax.ShapeDtypeStruct((m, d), F32),
        scratch_shapes=[pltpu.VMEM((tm, d), BF16)],
        compiler_params=_params("parallel", "arbitrary"),
        name="ffn_ln",
    )(x, wg, wu, wd, ln_g, ln_b)


def _in_proj_kernel(x_ref, w_ref, o_ref, xb_ref):
    @pl.when(pl.program_id(1) == 0)
    def _():
        xb_ref[...] = x_ref[...].astype(BF16)

    o_ref[...] = jnp.dot(xb_ref[...], w_ref[...], preferred_element_type=F32)


def in_proj(x, w_in, l, tm, tn):
    m, d = x.shape
    n = w_in.shape[-1]
    tm = min(tm, m)
    assert m % tm == 0 and n % tn == 0
    return pl.pallas_call(
        _in_proj_kernel,
        grid=(m // tm, n // tn),
        in_specs=[
            pl.BlockSpec((tm, d), lambda i, c: (i, 0)),
            pl.BlockSpec((None, d, tn), lambda i, c: (l, 0, c)),
        ],
        out_specs=pl.BlockSpec((tm, tn), lambda i, c: (i, c)),
        out_shape=jax.ShapeDtypeStruct((m, n), F32),
        scratch_shapes=[pltpu.VMEM((tm, d), BF16)],
        compiler_params=_params("parallel", "arbitrary"),
        name="in_proj",
    )(x, w_in)


def _fold_rows_max(a):
    rows, cols = a.shape
    return jnp.max(a.reshape(rows // SUBLANES, SUBLANES, cols), axis=0)


def _moba_kernel(q_ref, k_ref, v_ref, o_ref, qat_ref, ka_ref, vat_ref, vat3_ref, s_ref, so_ref, *, nb, scale):
    j = pl.program_id(2)
    blk = MOBA_BLOCK
    grp = MOBA_GROUP
    dh = HEAD_DIM
    seq = nb * blk
    tq = MOBA_Q_BLOCKS * blk
    span = grp * blk
    nbp = -(-nb // SUBLANES) * SUBLANES

    @pl.when(j == 0)
    def _():
        k = k_ref[...]
        lane = lax.broadcasted_iota(jnp.int32, (seq, LANES), 1)
        tile = lax.broadcasted_iota(jnp.int32, (nb, blk, LANES), 0).reshape(seq, LANES)
        ka_ref[:, 0:dh] = k.astype(BF16)
        ka_ref[:, dh:2 * dh] = jnp.where(tile == lane, 1.0, 0.0).astype(BF16)
        vat = jnp.concatenate([v_ref[...].T, jnp.ones((dh, seq), F32)], axis=0).astype(BF16)
        vat_ref[...] = vat
        for n in range(nb):
            vat3_ref[n] = vat[:, n * blk:(n + 1) * blk]
        k_mean = jnp.mean(k.reshape(nb, blk, dh), axis=1)
        k_mean = jnp.concatenate([k_mean, jnp.zeros((LANES - nb, dh), F32)], axis=0)

        q = q_ref[...]
        gate = lax.dot_general(k_mean.astype(BF16), q.astype(BF16), NT_DIMS, preferred_element_type=F32)
        g = gate[0:nbp]
        blk_row = lax.broadcasted_iota(jnp.int32, (nbp, seq), 0)
        q_blk = lax.shift_right_logical(lax.broadcasted_iota(jnp.int32, (nbp, seq), 1), blk.bit_length() - 1)
        row_f = blk_row.astype(F32)
        past = blk_row < q_blk
        g = jnp.where(past, g, MASK_VALUE)
        sel = jnp.zeros((nbp, seq), F32)
        for _ in range(MOBA_TOPK):
            top = jnp.max(g, axis=0, keepdims=True)
            first = jnp.min(jnp.where(g == top, row_f, float(LANES)), axis=0, keepdims=True)
            pick = row_f == first
            sel = jnp.where(pick, 1.0, sel)
            g = jnp.where(pick, -jnp.inf, g)
        bias = jnp.where(past & (sel > 0.5), 0.0, MASK_VALUE)
        qt = (q * (scale * LOG2_E)).T
        qat = jnp.concatenate([qt, bias, jnp.zeros((dh - nbp, seq), F32)], axis=0).astype(BF16)
        for t in range(seq // tq):
            qat_ref[t] = qat[:, t * tq:(t + 1) * tq]

    qat = qat_ref[j]
    first_blk = j * MOBA_Q_BLOCKS

    row = lax.broadcasted_iota(jnp.int32, (blk, blk), 0)
    col = lax.broadcasted_iota(jnp.int32, (blk, blk), 1)
    own_max = []
    for c in range(MOBA_Q_BLOCKS):
        start = pl.multiple_of((first_blk + c) * blk, blk)
        s = jnp.dot(ka_ref[pl.ds(start, blk), 0:dh], qat[0:dh, c * blk:(c + 1) * blk],
                    preferred_element_type=F32)
        s = jnp.where(row <= col, s, MASK_VALUE)
        so_ref[c] = s
        own_max.append(_fold_rows_max(s))
    mrun0 = jnp.concatenate(own_max, axis=1)

    dense_blocks = first_blk + MOBA_Q_BLOCKS - 1
    n_groups = (dense_blocks + grp - 1) // grp

    def attend(count):
        width = tq // MOBA_Q_PARTS
        own_per_part = MOBA_Q_BLOCKS // MOBA_Q_PARTS
        maxes = []
        for part in range(MOBA_Q_PARTS):
            qs = slice(part * width, (part + 1) * width)
            mrun = mrun0[:, qs]
            for g in range(count):
                s = jnp.dot(ka_ref[g * span:(g + 1) * span, :], qat[:, qs], preferred_element_type=F32)
                s_ref[g, :, qs] = s
                mrun = jnp.maximum(mrun, _fold_rows_max(s))
            maxes.append(jnp.max(mrun, axis=0, keepdims=True))
        outs = []
        for part in range(MOBA_Q_PARTS):
            qs = slice(part * width, (part + 1) * width)
            m = maxes[part]
            cols = []
            for c in range(own_per_part):
                p = jnp.exp2((so_ref[part * own_per_part + c] - m[:, c * blk:(c + 1) * blk]).astype(BF16))
                cols.append(jnp.dot(vat3_ref[first_blk + part * own_per_part + c], p,
                                    preferred_element_type=F32))
            acc = jnp.concatenate(cols, axis=1)
            for g in range(count):
                p = jnp.exp2((s_ref[g, :, qs] - m).astype(BF16))
                acc = acc + jnp.dot(vat_ref[:, g * span:(g + 1) * span], p, preferred_element_type=F32)
            outs.append(acc[0:dh] / acc[dh:2 * dh])
        o_ref[...] = jnp.concatenate(outs, axis=1).T

    min_count = (MOBA_Q_BLOCKS - 1 + grp - 1) // grp
    max_count = (nb - 1 + grp - 1) // grp
    for count in range(min_count, max_count + 1):
        pl.when(n_groups == count)(functools.partial(attend, count))


def moba(p3):
    bsz, seq, _ = p3.shape
    blk = MOBA_BLOCK
    grp = MOBA_GROUP
    tq = MOBA_Q_BLOCKS * blk
    assert seq % (blk * grp) == 0 and seq % tq == 0
    nb = seq // blk
    assert nb <= HEAD_DIM
    max_count = (nb - 1 + grp - 1) // grp
    kernel = functools.partial(_moba_kernel, nb=nb, scale=HEAD_DIM ** -0.5)
    col = lambda c: pl.BlockSpec((None, seq, HEAD_DIM), lambda b, h, j: (b, 0, c * N_HEADS + h))
    return pl.pallas_call(
        kernel,
        grid=(bsz, N_HEADS, seq // tq),
        in_specs=[col(0), col(1), col(2)],
        out_specs=pl.BlockSpec((None, tq, HEAD_DIM), lambda b, h, j: (b, j, h)),
        out_shape=jax.ShapeDtypeStruct((bsz, seq, D_GROUP), F32),
        scratch_shapes=[
            pltpu.VMEM((seq // tq, 2 * HEAD_DIM, tq), BF16),
            pltpu.VMEM((seq, 2 * HEAD_DIM), BF16),
            pltpu.VMEM((2 * HEAD_DIM, seq), BF16),
            pltpu.VMEM((nb, 2 * HEAD_DIM, blk), BF16),
            pltpu.VMEM((max_count, grp * blk, tq), F32),
            pltpu.VMEM((MOBA_Q_BLOCKS, blk, blk), F32),
        ],
        compiler_params=_params("parallel", "parallel", "arbitrary"),
        name="moba",
    )(p3, p3, p3)


def _rglru_kernel(gate_ref, x_ref, cw_ref, cb_ref, wa_ref, ba_ref, wx_ref, bx_ref, lam_ref,
                  o_ref, halo_ref, h_ref):
    t = pl.program_id(1)
    rows = x_ref.shape[0]

    @pl.when(t == 0)
    def _():
        halo_ref[...] = jnp.zeros(halo_ref.shape, F32)
        h_ref[...] = jnp.zeros(h_ref.shape, F32)

    x = x_ref[...]
    ext = jnp.concatenate([halo_ref[...], x], axis=0)
    halo_ref[...] = x[rows - SUBLANES:]
    xc = jnp.zeros_like(x) + cb_ref[...]
    for j in range(RG_CONV):
        xc = xc + cw_ref[j:j + 1, :] * _shift_rows(ext, RG_CONV - 1 - j, SUBLANES, rows)

    xcb = xc.astype(BF16)
    r_parts, i_parts = [], []
    for g in range(D_GROUP // HEAD_DIM):
        blk = xcb[:, g * HEAD_DIM:(g + 1) * HEAD_DIM]
        r_parts.append(jnp.dot(blk, wa_ref[g].astype(BF16), preferred_element_type=F32))
        i_parts.append(jnp.dot(blk, wx_ref[g].astype(BF16), preferred_element_type=F32))
    r = jax.nn.sigmoid(jnp.concatenate(r_parts, axis=-1) + ba_ref[...])
    ig = jax.nn.sigmoid(jnp.concatenate(i_parts, axis=-1) + bx_ref[...])
    neg_lam = -lam_ref[...]
    softplus = jnp.maximum(neg_lam, 0.0) + jnp.log1p(jnp.exp(-jnp.abs(neg_lam)))
    log_a = -RG_C * r * softplus
    a = jnp.exp(log_a)
    u = jnp.sqrt(jnp.maximum(1.0 - jnp.exp(2.0 * log_a), 0.0)) * (ig * xc)

    in_group = lax.broadcasted_iota(jnp.int32, a.shape, 0) & (SUBLANES - 1)
    big_a, big_b = a, u
    k = 1
    while k < SUBLANES:
        a_sh = jnp.where(in_group >= k, pltpu.roll(big_a, k, axis=0), 1.0)
        b_sh = jnp.where(in_group >= k, pltpu.roll(big_b, k, axis=0), 0.0)
        big_b = big_a * b_sh + big_b
        big_a = big_a * a_sh
        k *= 2
    carry = h_ref[0:1, :]
    groups = []
    for r0 in range(0, rows, SUBLANES):
        h_grp = big_a[r0:r0 + SUBLANES] * carry + big_b[r0:r0 + SUBLANES]
        groups.append(h_grp)
        carry = h_grp[SUBLANES - 1:SUBLANES]
    h = jnp.concatenate(groups, axis=0)
    h_ref[0:1, :] = carry
    o_ref[...] = h * jax.nn.gelu(gate_ref[...], approximate=True)


def rglru(p3, conv_w, conv_b, w_a, b_a, w_x, b_x, lam, l, tt):
    bsz, seq, _ = p3.shape
    tt = min(tt, seq)
    assert seq % tt == 0
    c = D_GROUP
    vec = lambda: pl.BlockSpec((None, 1, c), lambda b, t: (l, 0, 0))
    return pl.pallas_call(
        _rglru_kernel,
        grid=(bsz, seq // tt),
        in_specs=[
            pl.BlockSpec((None, tt, c), lambda b, t: (b, t, 3)),
            pl.BlockSpec((None, tt, c), lambda b, t: (b, t, 4)),
            pl.BlockSpec((None, RG_CONV, c), lambda b, t: (l, 0, 0)),
            vec(),
            pl.BlockSpec((None, c // HEAD_DIM, HEAD_DIM, HEAD_DIM), lambda b, t: (l, 0, 0, 0)),
            vec(),
            pl.BlockSpec((None, c // HEAD_DIM, HEAD_DIM, HEAD_DIM), lambda b, t: (l, 0, 0, 0)),
            vec(),
            vec(),
        ],
        out_specs=pl.BlockSpec((None, tt, c), lambda b, t: (b, t, 0)),
        out_shape=jax.ShapeDtypeStruct((bsz, seq, c), F32),
        scratch_shapes=[pltpu.VMEM((SUBLANES, c), F32), pltpu.VMEM((SUBLANES, c), F32)],
        compiler_params=_params("parallel", "arbitrary"),
        name="rglru",
    )(p3, p3, conv_w, conv_b, w_a, b_a, w_x, b_x, lam)


def _conformer_units(val_ref, gate_ref, cw_ref, cb_ref, ng_ref, nb_ref, o_ref, halo_ref):
    rows = val_ref.shape[0]

    def unit(g):
        cs = slice(g * HEAD_DIM, (g + 1) * HEAD_DIM)
        glu = val_ref[:, cs] * jax.nn.sigmoid(gate_ref[:, cs])
        ext = jnp.concatenate([halo_ref[:, cs], glu], axis=0)
        halo_ref[:, cs] = glu[rows - CV_HALO:]
        rolled = [ext] + [pltpu.roll(ext, r, axis=0) for r in range(1, SUBLANES)]
        u = jnp.zeros_like(glu) + cb_ref[:, cs]
        for j in range(CV_WIDTH):
            d = CV_WIDTH - 1 - j
            r = d % SUBLANES
            start = CV_HALO - (d - r)
            u = u + cw_ref[j:j + 1, cs] * rolled[r][start:start + rows]
        o_ref[:, cs] = _silu(_standardize(u) * ng_ref[:, cs] + nb_ref[:, cs]).astype(o_ref.dtype)

    return [functools.partial(unit, g) for g in range(D_GROUP // HEAD_DIM)]


def _conformer_kernel(val_ref, gate_ref, cw_ref, cb_ref, ng_ref, nb_ref, o_ref, halo_ref):
    @pl.when(pl.program_id(1) == 0)
    def _():
        halo_ref[...] = jnp.zeros(halo_ref.shape, F32)

    for unit in _conformer_units(val_ref, gate_ref, cw_ref, cb_ref, ng_ref, nb_ref, o_ref, halo_ref):
        unit()


def conformer(p3, cv_w, cv_b, ng, nb, l, tt):
    bsz, seq, _ = p3.shape
    tt = min(tt, seq)
    assert seq % tt == 0 and tt >= CV_HALO
    c = D_GROUP
    vec = lambda: pl.BlockSpec((None, 1, c), lambda b, t: (l, 0, 0))
    return pl.pallas_call(
        _conformer_kernel,
        grid=(bsz, seq // tt),
        in_specs=[
            pl.BlockSpec((None, tt, c), lambda b, t: (b, t, 5)),
            pl.BlockSpec((None, tt, c), lambda b, t: (b, t, 6)),
            pl.BlockSpec((None, CV_WIDTH, c), lambda b, t: (l, 0, 0)),
            vec(), vec(), vec(),
        ],
        out_specs=pl.BlockSpec((None, tt, c), lambda b, t: (b, t, 0)),
        out_shape=jax.ShapeDtypeStruct((bsz, seq, c), F32),
        scratch_shapes=[pltpu.VMEM((CV_HALO, c), F32)],
        compiler_params=_params("parallel", "arbitrary"),
        name="conformer",
    )(p3, p3, cv_w, cv_b, ng, nb)


def _hgrn2_chunk(q, kk, v, lf2, state_t):
    c = HG_CHUNK
    sub = HG_SUB
    row = lax.broadcasted_iota(jnp.int32, (c, HEAD_DIM), 0)
    b = lf2
    k = 1
    while k < c:
        b = b + jnp.where(row >= k, pltpu.roll(b, k, axis=0), 0.0)
        k *= 2
    vb = v.astype(BF16)

    o = lax.dot_general((q * jnp.exp2(b)).astype(BF16), state_t.astype(BF16), NT_DIMS,
                        preferred_element_type=F32)

    lane_s = lax.broadcasted_iota(jnp.int32, (sub, c), 1)
    row_s = lax.broadcasted_iota(jnp.int32, (sub, c), 0)
    att_rows = []
    for blk in range(c // sub):
        lo = blk * sub
        q_i = q[lo:lo + sub]
        b_i = b[lo:lo + sub]
        k_i = kk[lo:lo + sub]
        att = jnp.zeros((sub, c), F32)
        if blk > 0:
            ref = b[lo - 1:lo]
            qs = (q_i * jnp.exp2(b_i - ref)).astype(BF16)
            ks = kk[:lo] * jnp.exp2(ref - b[:lo])
            ks = jnp.concatenate([ks, jnp.zeros((c - lo, HEAD_DIM), F32)], axis=0).astype(BF16)
            att = lax.dot_general(qs, ks, NT_DIMS, preferred_element_type=F32)
        for s in range(sub):
            decay = jnp.exp2(jnp.minimum(b_i - b_i[s:s + 1], 0.0))
            w = jnp.sum(q_i * k_i[s:s + 1] * decay, axis=-1, keepdims=True)
            att = att + jnp.where((lane_s == lo + s) & (row_s >= s), w, 0.0)
        att_rows.append(att)
    att = jnp.concatenate(att_rows, axis=0)
    o = o + jnp.dot(att.astype(BF16), vb, preferred_element_type=F32)

    b_last = b[c - 1:c]
    ks = (kk * jnp.exp2(b_last - b)).astype(BF16)
    new_state_t = state_t * jnp.exp2(b_last) + lax.dot_general(vb, ks, TN_DIMS, preferred_element_type=F32)
    return o, new_state_t


def _hgrn2_units(q_ref, f_ref, v_ref, g_ref, lbp_ref, ng_ref, o_ref, state_ref, layer):
    rows = q_ref.shape[0]
    n_chunks = rows // HG_CHUNK

    lbp = lbp_ref[...]
    e = jnp.exp(lbp - jnp.max(lbp, axis=0, keepdims=True))
    sm = e / jnp.sum(e, axis=0, keepdims=True)
    lb = jnp.sum(sm[0:layer + 1], axis=0, keepdims=True) - sm[0:1]
    states = [state_ref[h] for h in range(N_HEADS)]

    def unit(ci, h):
        cs = slice(h * HEAD_DIM, (h + 1) * HEAD_DIM)
        rs = slice(ci * HG_CHUNK, (ci + 1) * HG_CHUNK)
        lb_h = lb[:, cs]
        sig = jax.nn.sigmoid(f_ref[rs, cs])
        lf2 = jnp.log(lb_h + (1.0 - lb_h) * sig) * LOG2_E
        kk = (1.0 - lb_h) * (1.0 - sig)
        o, states[h] = _hgrn2_chunk(q_ref[rs, cs], kk, v_ref[rs, cs], lf2, states[h])
        o = o * lax.rsqrt(jnp.mean(o * o, axis=-1, keepdims=True) + LN_EPS)
        o_ref[rs, cs] = (o * ng_ref[:, cs] * _silu(g_ref[rs, cs])).astype(o_ref.dtype)
        if ci == n_chunks - 1:
            state_ref[h] = states[h]

    return [functools.partial(unit, ci, h) for ci in range(n_chunks) for h in range(N_HEADS)]


def _hgrn2_kernel(q_ref, f_ref, v_ref, g_ref, lbp_ref, ng_ref, o_ref, state_ref, *, layer):
    @pl.when(pl.program_id(1) == 0)
    def _():
        state_ref[...] = jnp.zeros(state_ref.shape, F32)

    for unit in _hgrn2_units(q_ref, f_ref, v_ref, g_ref, lbp_ref, ng_ref, o_ref, state_ref, layer):
        unit()


def hgrn2(p3, lower_bound_params, norm_g, l, tt):
    bsz, seq, _ = p3.shape
    tt = min(tt, seq)
    assert seq % tt == 0 and tt % HG_CHUNK == 0
    c = D_GROUP
    depth = lower_bound_params.shape[0]
    tok = lambda col: pl.BlockSpec((None, tt, c), lambda b, t: (b, t, col))
    return pl.pallas_call(
        functools.partial(_hgrn2_kernel, layer=l),
        grid=(bsz, seq // tt),
        in_specs=[
            tok(7), tok(8), tok(9), tok(10),
            pl.BlockSpec((depth, c), lambda b, t: (0, 0)),
            pl.BlockSpec((None, 1, c), lambda b, t: (l, 0, 0)),
        ],
        out_specs=pl.BlockSpec((None, tt, c), lambda b, t: (b, t, 0)),
        out_shape=jax.ShapeDtypeStruct((bsz, seq, c), F32),
        scratch_shapes=[pltpu.VMEM((N_HEADS, HEAD_DIM, HEAD_DIM), F32)],
        compiler_params=_params("parallel", "arbitrary"),
        name="hgrn2",
    )(p3, p3, p3, p3, lower_bound_params, norm_g)


def _out_proj_ln_units(x_ref, y_refs, w_ref, g_ref, b_ref, o_ref, alpha):
    c = D_GROUP
    rows = x_ref.shape[0]
    step = min(rows, OUT_PROJ_ROWS)
    units = []
    for r0 in range(0, rows, step):
        rs = slice(r0, r0 + step)
        acc = []

        def matmul(y_ref, k, rs=rs, acc=acc):
            part = jnp.dot(y_ref[rs, :].astype(BF16), w_ref[k * c:(k + 1) * c, :], preferred_element_type=F32)
            acc[:] = [part if not acc else acc[0] + part]

        def norm(rs=rs, acc=acc):
            z = alpha * x_ref[rs, :] + acc[0]
            o_ref[rs, :] = _standardize(z) * g_ref[...] + b_ref[...]

        units += [functools.partial(matmul, y_ref, k) for k, y_ref in enumerate(y_refs)]
        units.append(norm)
    return units


def _out_proj_ln_kernel(x_ref, ya_ref, yb_ref, yc_ref, yd_ref, w_ref, g_ref, b_ref, o_ref, *, alpha):
    for unit in _out_proj_ln_units(x_ref, (ya_ref, yb_ref, yc_ref, yd_ref), w_ref, g_ref, b_ref, o_ref, alpha):
        unit()


def out_proj_ln(x, ya, yb, yc, yd, w_out, ln_g, ln_b, l, jn, alpha, tm):
    m, d = x.shape
    c = D_GROUP
    tm = min(tm, m)
    assert m % tm == 0
    mix = lambda: pl.BlockSpec((tm, c), lambda i: (i, 0))
    return pl.pallas_call(
        functools.partial(_out_proj_ln_kernel, alpha=alpha),
        grid=(m // tm,),
        in_specs=[
            pl.BlockSpec((tm, d), lambda i: (i, 0)),
            mix(), mix(), mix(), mix(),
            pl.BlockSpec((None, 4 * c, d), lambda i: (l, 0, 0)),
            pl.BlockSpec((None, 1, d), lambda i: (jn, 0, 0)),
            pl.BlockSpec((None, 1, d), lambda i: (jn, 0, 0)),
        ],
        out_specs=pl.BlockSpec((tm, d), lambda i: (i, 0)),
        out_shape=jax.ShapeDtypeStruct((m, d), F32),
        compiler_params=_params("parallel"),
        name="out_proj_ln",
    )(x, ya, yb, yc, yd, w_out, ln_g, ln_b)


def kernel(x, ln_g, ln_b, ffn_w_gate, ffn_w_up, ffn_w_down, w_in, w_out, rg_conv_w, rg_conv_b,
           rg_w_a, rg_b_a, rg_w_x, rg_b_x, rg_lambda, cv_w, cv_b, cv_ln_g, cv_ln_b,
           hg_lower_bounds, hg_norm_g):
    bsz, seq, d = x.shape
    depth = w_in.shape[0]
    alpha = (2 * depth) ** 0.25
    m = bsz * seq

    wg = ffn_w_gate.astype(BF16)
    wu = ffn_w_up.astype(BF16)
    wd = ffn_w_down.astype(BF16)
    w_in_b = w_in.astype(BF16)
    w_out_b = w_out.astype(BF16)
    ln_g3 = ln_g.reshape(depth * 3, 1, d)
    ln_b3 = ln_b.reshape(depth * 3, 1, d)
    row = lambda a: a.reshape(depth, 1, a.shape[-1])

    h = x.reshape(m, d)
    for l in range(depth):
        h = ffn_ln(h, wg, wu, wd, ln_g3, ln_b3, l, 0, 3 * l, alpha, tm=1024, tf=FFN_COLS)
        p3 = in_proj(h, w_in_b, l, tm=1024, tn=IN_PROJ_COLS).reshape(bsz, seq, -1)
        y_a = moba(p3)
        y_b = rglru(p3, rg_conv_w, row(rg_conv_b), rg_w_a, row(rg_b_a), rg_w_x, row(rg_b_x),
                    row(rg_lambda), l, tt=1024)
        y_c = conformer(p3, cv_w, row(cv_b), row(cv_ln_g), row(cv_ln_b), l, tt=1024)
        y_d = hgrn2(p3, hg_lower_bounds, row(hg_norm_g), l, tt=512)
        flat = lambda y: y.reshape(m, D_GROUP)
        h = out_proj_ln(h, flat(y_a), flat(y_b), flat(y_c), flat(y_d), w_out_b, ln_g3, ln_b3,
                        l, 3 * l + 1, alpha, tm=512)
        h = ffn_ln(h, wg, wu, wd, ln_g3, ln_b3, l, 1, 3 * l + 2, alpha, tm=1024, tf=FFN_COLS)
    return h.reshape(bsz, seq, d)
```

```python
import functools

import jax
import jax.numpy as jnp
from jax import lax
from jax.experimental import pallas as pl
from jax.experimental.pallas import tpu as pltpu

F32 = jnp.float32
BF16 = jnp.bfloat16

LANES = 128
SUBLANES = 8
VMEM_LIMIT_BYTES = 56 * 1024 * 1024

D_GROUP = 512
HEAD_DIM = 128
N_HEADS = D_GROUP // HEAD_DIM
MOBA_BLOCK = 256
MOBA_TOPK = 3
MOBA_GROUP = 4
MOBA_Q_BLOCKS = 4
MOBA_Q_PARTS = 2
LOG2_E = 1.4426950408889634
MASK_VALUE = -1e30
RG_C = 8.0
RG_CONV = 4
CV_WIDTH = 31
CV_HALO = 32
FFN_COLS = 512
IN_PROJ_COLS = 1408
OUT_PROJ_ROWS = 256
FFN_DOWN_COLS = 512
FFN_NORM_ROWS = 256
HG_CHUNK = 64
HG_SUB = 8
LN_EPS = 1e-5

NT_DIMS = (((1,), (1,)), ((), ()))
TN_DIMS = (((0,), (0,)), ((), ()))


def _params(*sem):
    return pltpu.CompilerParams(dimension_semantics=sem, vmem_limit_bytes=VMEM_LIMIT_BYTES)


def _standardize(z):
    mu = jnp.mean(z, axis=-1, keepdims=True)
    zc = z - mu
    var = jnp.mean(zc * zc, axis=-1, keepdims=True)
    return zc * lax.rsqrt(var + LN_EPS)


def _silu(z):
    return z * jax.nn.sigmoid(z)


def _shift_rows(ext, d, halo, rows):
    if d % SUBLANES == 0:
        return ext[halo - d:halo - d + rows]
    r = d % SUBLANES
    base = d - r
    rolled = pltpu.roll(ext, r, axis=0)
    return rolled[halo - base:halo - base + rows]


def _ffn_ln_kernel(x_ref, wg_ref, wu_ref, wd_ref, g_ref, b_ref, o_ref, xb_ref, *, alpha):
    f = pl.program_id(1)
    last = pl.num_programs(1) - 1
    rows, d = o_ref.shape

    def hidden(xb):
        hg = jnp.dot(xb, wg_ref[...], preferred_element_type=F32)
        hu = jnp.dot(xb, wu_ref[...], preferred_element_type=F32)
        return (_silu(hg) * hu).astype(BF16)

    def down(h, accumulate):
        for c0 in range(0, d, FFN_DOWN_COLS):
            cs = slice(c0, min(c0 + FFN_DOWN_COLS, d))
            part = jnp.dot(h, wd_ref[:, cs], preferred_element_type=F32)
            o_ref[:, cs] = o_ref[:, cs] + part if accumulate else part

    @pl.when(f == 0)
    def _():
        xb = x_ref[...].astype(BF16)
        xb_ref[...] = xb
        down(hidden(xb), accumulate=False)

    @pl.when((f > 0) & (f < last))
    def _():
        down(hidden(xb_ref[...]), accumulate=True)

    @pl.when(f == last)
    def _():
        h = hidden(xb_ref[...])
        for r0 in range(0, rows, FFN_NORM_ROWS):
            rs = slice(r0, min(r0 + FFN_NORM_ROWS, rows))
            acc = o_ref[rs, :] + jnp.dot(h[rs, :], wd_ref[...], preferred_element_type=F32)
            z = alpha * x_ref[rs, :] + 0.5 * acc
            o_ref[rs, :] = _standardize(z) * g_ref[...] + b_ref[...]


def ffn_ln(x, wg, wu, wd, ln_g, ln_b, l, j, jn, alpha, tm, tf):
    m, d = x.shape
    dff = wg.shape[-1]
    tm = min(tm, m)
    tf = min(tf, dff)
    assert m % tm == 0 and dff % tf == 0
    grid = (m // tm, dff // tf)
    return pl.pallas_call(
        functools.partial(_ffn_ln_kernel, alpha=alpha),
        grid=grid,
        in_specs=[
            pl.BlockSpec((tm, d), lambda i, f: (i, 0)),
            pl.BlockSpec((None, None, d, tf), lambda i, f: (l, j, 0, f)),
            pl.BlockSpec((None, None, d, tf), lambda i, f: (l, j, 0, f)),
            pl.BlockSpec((None, None, tf, d), lambda i, f: (l, j, f, 0)),
            pl.BlockSpec((None, 1, d), lambda i, f: (jn, 0, 0)),
            pl.BlockSpec((None, 1, d), lambda i, f: (jn, 0, 0)),
        ],
        out_specs=pl.BlockSpec((tm, d), lambda i, f: (i, 0)),
        out_shape=jax.ShapeDtypeStruct((m, d), F32),
        scratch_shapes=[pltpu.VMEM((tm, d), BF16)],
        compiler_params=_params("parallel", "arbitrary"),
        name="ffn_ln",
    )(x, wg, wu, wd, ln_g, ln_b)


def _in_proj_kernel(x_ref, w_ref, o_ref, xb_ref):
    @pl.when(pl.program_id(1) == 0)
    def _():
        xb_ref[...] = x_ref[...].astype(BF16)

    o_ref[...] = jnp.dot(xb_ref[...], w_ref[...], preferred_element_type=F32)


def in_proj(x, w_in, l, tm, tn):
    m, d = x.shape
    n = w_in.shape[-1]
    tm = min(tm, m)
    assert m % tm == 0 and n % tn == 0
    return pl.pallas_call(
        _in_proj_kernel,
        grid=(m // tm, n // tn),
        in_specs=[
            pl.BlockSpec((tm, d), lambda i, c: (i, 0)),
            pl.BlockSpec((None, d, tn), lambda i, c: (l, 0, c)),
        ],
        out_specs=pl.BlockSpec((tm, tn), lambda i, c: (i, c)),
        out_shape=jax.ShapeDtypeStruct((m, n), F32),
        scratch_shapes=[pltpu.VMEM((tm, d), BF16)],
        compiler_params=_params("parallel", "arbitrary"),
        name="in_proj",
    )(x, w_in)


def _fold_rows_max(a):
    rows, cols = a.shape
    return jnp.max(a.reshape(rows // SUBLANES, SUBLANES, cols), axis=0)


def _moba_kernel(q_ref, k_ref, v_ref, o_ref, qat_ref, ka_ref, vat_ref, vat3_ref, s_ref, so_ref, *, nb, scale):
    j = pl.program_id(2)
    blk = MOBA_BLOCK
    grp = MOBA_GROUP
    dh = HEAD_DIM
    seq = nb * blk
    tq = MOBA_Q_BLOCKS * blk
    span = grp * blk
    nbp = -(-nb // SUBLANES) * SUBLANES

    @pl.when(j == 0)
    def _():
        k = k_ref[...]
        lane = lax.broadcasted_iota(jnp.int32, (seq, LANES), 1)
        tile = lax.broadcasted_iota(jnp.int32, (nb, blk, LANES), 0).reshape(seq, LANES)
        ka_ref[:, 0:dh] = k.astype(BF16)
        ka_ref[:, dh:2 * dh] = jnp.where(tile == lane, 1.0, 0.0).astype(BF16)
        vat = jnp.concatenate([v_ref[...].T, jnp.ones((dh, seq), F32)], axis=0).astype(BF16)
        vat_ref[...] = vat
        for n in range(nb):
            vat3_ref[n] = vat[:, n * blk:(n + 1) * blk]
        k_mean = jnp.mean(k.reshape(nb, blk, dh), axis=1)
        k_mean = jnp.concatenate([k_mean, jnp.zeros((LANES - nb, dh), F32)], axis=0)

        q = q_ref[...]
        gate = lax.dot_general(k_mean.astype(BF16), q.astype(BF16), NT_DIMS, preferred_element_type=F32)
        g = gate[0:nbp]
        blk_row = lax.broadcasted_iota(jnp.int32, (nbp, seq), 0)
        q_blk = lax.shift_right_logical(lax.broadcasted_iota(jnp.int32, (nbp, seq), 1), blk.bit_length() - 1)
        row_f = blk_row.astype(F32)
        past = blk_row < q_blk
        g = jnp.where(past, g, MASK_VALUE)
        sel = jnp.zeros((nbp, seq), F32)
        for _ in range(MOBA_TOPK):
            top = jnp.max(g, axis=0, keepdims=True)
            first = jnp.min(jnp.where(g == top, row_f, float(LANES)), axis=0, keepdims=True)
            pick = row_f == first
            sel = jnp.where(pick, 1.0, sel)
            g = jnp.where(pick, -jnp.inf, g)
        bias = jnp.where(past & (sel > 0.5), 0.0, MASK_VALUE)
        qt = (q * (scale * LOG2_E)).T
        qat = jnp.concatenate([qt, bias, jnp.zeros((dh - nbp, seq), F32)], axis=0).astype(BF16)
        for t in range(seq // tq):
            qat_ref[t] = qat[:, t * tq:(t + 1) * tq]

    qat = qat_ref[j]
    first_blk = j * MOBA_Q_BLOCKS

    row = lax.broadcasted_iota(jnp.int32, (blk, blk), 0)
    col = lax.broadcasted_iota(jnp.int32, (blk, blk), 1)
    own_max = []
    for c in range(MOBA_Q_BLOCKS):
        start = pl.multiple_of((first_blk + c) * blk, blk)
        s = jnp.dot(ka_ref[pl.ds(start, blk), 0:dh], qat[0:dh, c * blk:(c + 1) * blk],
                    preferred_element_type=F32)
        s = jnp.where(row <= col, s, MASK_VALUE)
        so_ref[c] = s
        own_max.append(_fold_rows_max(s))
    mrun0 = jnp.concatenate(own_max, axis=1)

    dense_blocks = first_blk + MOBA_Q_BLOCKS - 1
    n_groups = (dense_blocks + grp - 1) // grp

    def attend(count):
        width = tq // MOBA_Q_PARTS
        own_per_part = MOBA_Q_BLOCKS // MOBA_Q_PARTS
        maxes = []
        for part in range(MOBA_Q_PARTS):
            qs = slice(part * width, (part + 1) * width)
            mrun = mrun0[:, qs]
            for g in range(count):
                s = jnp.dot(ka_ref[g * span:(g + 1) * span, :], qat[:, qs], preferred_element_type=F32)
                s_ref[g, :, qs] = s
                mrun = jnp.maximum(mrun, _fold_rows_max(s))
            maxes.append(jnp.max(mrun, axis=0, keepdims=True))
        outs = []
        for part in range(MOBA_Q_PARTS):
            qs = slice(part * width, (part + 1) * width)
            m = maxes[part]
            cols = []
            for c in range(own_per_part):
                p = jnp.exp2((so_ref[part * own_per_part + c] - m[:, c * blk:(c + 1) * blk]).astype(BF16))
                cols.append(jnp.dot(vat3_ref[first_blk + part * own_per_part + c], p,
                                    preferred_element_type=F32))
            acc = jnp.concatenate(cols, axis=1)
            for g in range(count):
                p = jnp.exp2((s_ref[g, :, qs] - m).astype(BF16))
                acc = acc + jnp.dot(vat_ref[:, g * span:(g + 1) * span], p, preferred_element_type=F32)
            outs.append(acc[0:dh] / acc[dh:2 * dh])
        o_ref[...] = jnp.concatenate(outs, axis=1).T

    min_count = (MOBA_Q_BLOCKS - 1 + grp - 1) // grp
    max_count = (nb - 1 + grp - 1) // grp
    for count in range(min_count, max_count + 1):
        pl.when(n_groups == count)(functools.partial(attend, count))


def moba(p3):
    bsz, seq, _ = p3.shape
    blk = MOBA_BLOCK
    grp = MOBA_GROUP
    tq = MOBA_Q_BLOCKS * blk
    assert seq % (blk * grp) == 0 and seq % tq == 0
    nb = seq // blk
    assert nb <= HEAD_DIM
    max_count = (nb - 1 + grp - 1) // grp
    kernel = functools.partial(_moba_kernel, nb=nb, scale=HEAD_DIM ** -0.5)
    col = lambda c: pl.BlockSpec((None, seq, HEAD_DIM), lambda b, h, j: (b, 0, c * N_HEADS + h))
    return pl.pallas_call(
        kernel,
        grid=(bsz, N_HEADS, seq // tq),
        in_specs=[col(0), col(1), col(2)],
        out_specs=pl.BlockSpec((None, tq, HEAD_DIM), lambda b, h, j: (b, j, h)),
        out_shape=jax.ShapeDtypeStruct((bsz, seq, D_GROUP), F32),
        scratch_shapes=[
            pltpu.VMEM((seq // tq, 2 * HEAD_DIM, tq), BF16),
            pltpu.VMEM((seq, 2 * HEAD_DIM), BF16),
            pltpu.VMEM((2 * HEAD_DIM, seq), BF16),
            pltpu.VMEM((nb, 2 * HEAD_DIM, blk), BF16),
            pltpu.VMEM((max_count, grp * blk, tq), F32),
            pltpu.VMEM((MOBA_Q_BLOCKS, blk, blk), F32),
        ],
        compiler_params=_params("parallel", "parallel", "arbitrary"),
        name="moba",
    )(p3, p3, p3)


def _rglru_kernel(gate_ref, x_ref, cw_ref, cb_ref, wa_ref, ba_ref, wx_ref, bx_ref, lam_ref,
                  o_ref, halo_ref, h_ref):
    t = pl.program_id(1)
    rows = x_ref.shape[0]

    @pl.when(t == 0)
    def _():
        halo_ref[...] = jnp.zeros(halo_ref.shape, F32)
        h_ref[...] = jnp.zeros(h_ref.shape, F32)

    x = x_ref[...]
    ext = jnp.concatenate([halo_ref[...], x], axis=0)
    halo_ref[...] = x[rows - SUBLANES:]
    xc = jnp.zeros_like(x) + cb_ref[...]
    for j in range(RG_CONV):
        xc = xc + cw_ref[j:j + 1, :] * _shift_rows(ext, RG_CONV - 1 - j, SUBLANES, rows)

    xcb = xc.astype(BF16)
    r_parts, i_parts = [], []
    for g in range(D_GROUP // HEAD_DIM):
        blk = xcb[:, g * HEAD_DIM:(g + 1) * HEAD_DIM]
        r_parts.append(jnp.dot(blk, wa_ref[g].astype(BF16), preferred_element_type=F32))
        i_parts.append(jnp.dot(blk, wx_ref[g].astype(BF16), preferred_element_type=F32))
    r = jax.nn.sigmoid(jnp.concatenate(r_parts, axis=-1) + ba_ref[...])
    ig = jax.nn.sigmoid(jnp.concatenate(i_parts, axis=-1) + bx_ref[...])
    neg_lam = -lam_ref[...]
    softplus = jnp.maximum(neg_lam, 0.0) + jnp.log1p(jnp.exp(-jnp.abs(neg_lam)))
    log_a = -RG_C * r * softplus
    a = jnp.exp(log_a)
    u = jnp.sqrt(jnp.maximum(1.0 - jnp.exp(2.0 * log_a), 0.0)) * (ig * xc)

    in_group = lax.broadcasted_iota(jnp.int32, a.shape, 0) & (SUBLANES - 1)
    big_a, big_b = a, u
    k = 1
    while k < SUBLANES:
        a_sh = jnp.where(in_group >= k, pltpu.roll(big_a, k, axis=0), 1.0)
        b_sh = jnp.where(in_group >= k, pltpu.roll(big_b, k, axis=0), 0.0)
        big_b = big_a * b_sh + big_b
        big_a = big_a * a_sh
        k *= 2
    carry = h_ref[0:1, :]
    groups = []
    for r0 in range(0, rows, SUBLANES):
        h_grp = big_a[r0:r0 + SUBLANES] * carry + big_b[r0:r0 + SUBLANES]
        groups.append(h_grp)
        carry = h_grp[SUBLANES - 1:SUBLANES]
    h = jnp.concatenate(groups, axis=0)
    h_ref[0:1, :] = carry
    o_ref[...] = h * jax.nn.gelu(gate_ref[...], approximate=True)


def rglru(p3, conv_w, conv_b, w_a, b_a, w_x, b_x, lam, l, tt):
    bsz, seq, _ = p3.shape
    tt = min(tt, seq)
    assert seq % tt == 0
    c = D_GROUP
    vec = lambda: pl.BlockSpec((None, 1, c), lambda b, t: (l, 0, 0))
    return pl.pallas_call(
        _rglru_kernel,
        grid=(bsz, seq // tt),
        in_specs=[
            pl.BlockSpec((None, tt, c), lambda b, t: (b, t, 3)),
            pl.BlockSpec((None, tt, c), lambda b, t: (b, t, 4)),
            pl.BlockSpec((None, RG_CONV, c), lambda b, t: (l, 0, 0)),
            vec(),
            pl.BlockSpec((None, c // HEAD_DIM, HEAD_DIM, HEAD_DIM), lambda b, t: (l, 0, 0, 0)),
            vec(),
            pl.BlockSpec((None, c // HEAD_DIM, HEAD_DIM, HEAD_DIM), lambda b, t: (l, 0, 0, 0)),
            vec(),
            vec(),
        ],
        out_specs=pl.BlockSpec((None, tt, c), lambda b, t: (b, t, 0)),
        out_shape=jax.ShapeDtypeStruct((bsz, seq, c), F32),
        scratch_shapes=[pltpu.VMEM((SUBLANES, c), F32), pltpu.VMEM((SUBLANES, c), F32)],
        compiler_params=_params("parallel", "arbitrary"),
        name="rglru",
    )(p3, p3, conv_w, conv_b, w_a, b_a, w_x, b_x, lam)


def _conformer_units(val_ref, gate_ref, cw_ref, cb_ref, ng_ref, nb_ref, o_ref, halo_ref):
    rows = val_ref.shape[0]

    def unit(g):
        cs = slice(g * HEAD_DIM, (g + 1) * HEAD_DIM)
        glu = val_ref[:, cs] * jax.nn.sigmoid(gate_ref[:, cs])
        ext = jnp.concatenate([halo_ref[:, cs], glu], axis=0)
        halo_ref[:, cs] = glu[rows - CV_HALO:]
        rolled = [ext] + [pltpu.roll(ext, r, axis=0) for r in range(1, SUBLANES)]
        u = jnp.zeros_like(glu) + cb_ref[:, cs]
        for j in range(CV_WIDTH):
            d = CV_WIDTH - 1 - j
            r = d % SUBLANES
            start = CV_HALO - (d - r)
            u = u + cw_ref[j:j + 1, cs] * rolled[r][start:start + rows]
        o_ref[:, cs] = _silu(_standardize(u) * ng_ref[:, cs] + nb_ref[:, cs]).astype(o_ref.dtype)

    return [functools.partial(unit, g) for g in range(D_GROUP // HEAD_DIM)]


def _conformer_kernel(val_ref, gate_ref, cw_ref, cb_ref, ng_ref, nb_ref, o_ref, halo_ref):
    @pl.when(pl.program_id(1) == 0)
    def _():
        halo_ref[...] = jnp.zeros(halo_ref.shape, F32)

    for unit in _conformer_units(val_ref, gate_ref, cw_ref, cb_ref, ng_ref, nb_ref, o_ref, halo_ref):
        unit()


def conformer(p3, cv_w, cv_b, ng, nb, l, tt):
    bsz, seq, _ = p3.shape
    tt = min(tt, seq)
    assert seq % tt == 0 and tt >= CV_HALO
    c = D_GROUP
    vec = lambda: pl.BlockSpec((None, 1, c), lambda b, t: (l, 0, 0))
    return pl.pallas_call(
        _conformer_kernel,
        grid=(bsz, seq // tt),
        in_specs=[
            pl.BlockSpec((None, tt, c), lambda b, t: (b, t, 5)),
            pl.BlockSpec((None, tt, c), lambda b, t: (b, t, 6)),
            pl.BlockSpec((None, CV_WIDTH, c), lambda b, t: (l, 0, 0)),
            vec(), vec(), vec(),
        ],
        out_specs=pl.BlockSpec((None, tt, c), lambda b, t: (b, t, 0)),
        out_shape=jax.ShapeDtypeStruct((bsz, seq, c), F32),
        scratch_shapes=[pltpu.VMEM((CV_HALO, c), F32)],
        compiler_params=_params("parallel", "arbitrary"),
        name="conformer",
    )(p3, p3, cv_w, cv_b, ng, nb)


def _hgrn2_chunk(q, kk, v, lf2, state_t):
    c = HG_CHUNK
    sub = HG_SUB
    row = lax.broadcasted_iota(jnp.int32, (c, HEAD_DIM), 0)
    b = lf2
    k = 1
    while k < c:
        b = b + jnp.where(row >= k, pltpu.roll(b, k, axis=0), 0.0)
        k *= 2
    vb = v.astype(BF16)

    o = lax.dot_general((q * jnp.exp2(b)).astype(BF16), state_t.astype(BF16), NT_DIMS,
                        preferred_element_type=F32)

    lane_s = lax.broadcasted_iota(jnp.int32, (sub, c), 1)
    row_s = lax.broadcasted_iota(jnp.int32, (sub, c), 0)
    att_rows = []
    for blk in range(c // sub):
        lo = blk * sub
        q_i = q[lo:lo + sub]
        b_i = b[lo:lo + sub]
        k_i = kk[lo:lo + sub]
        att = jnp.zeros((sub, c), F32)
        if blk > 0:
            ref = b[lo - 1:lo]
            qs = (q_i * jnp.exp2(b_i - ref)).astype(BF16)
            ks = kk[:lo] * jnp.exp2(ref - b[:lo])
            ks = jnp.concatenate([ks, jnp.zeros((c - lo, HEAD_DIM), F32)], axis=0).astype(BF16)
            att = lax.dot_general(qs, ks, NT_DIMS, preferred_element_type=F32)
        for s in range(sub):
            decay = jnp.exp2(jnp.minimum(b_i - b_i[s:s + 1], 0.0))
            w = jnp.sum(q_i * k_i[s:s + 1] * decay, axis=-1, keepdims=True)
            att = att + jnp.where((lane_s == lo + s) & (row_s >= s), w, 0.0)
        att_rows.append(att)
    att = jnp.concatenate(att_rows, axis=0)
    o = o + jnp.dot(att.astype(BF16), vb, preferred_element_type=F32)

    b_last = b[c - 1:c]
    ks = (kk * jnp.exp2(b_last - b)).astype(BF16)
    new_state_t = state_t * jnp.exp2(b_last) + lax.dot_general(vb, ks, TN_DIMS, preferred_element_type=F32)
    return o, new_state_t


def _hgrn2_units(q_ref, f_ref, v_ref, g_ref, lbp_ref, ng_ref, o_ref, state_ref, layer):
    rows = q_ref.shape[0]
    n_chunks = rows // HG_CHUNK

    lbp = lbp_ref[...]
    e = jnp.exp(lbp - jnp.max(lbp, axis=0, keepdims=True))
    sm = e / jnp.sum(e, axis=0, keepdims=True)
    lb = jnp.sum(sm[0:layer + 1], axis=0, keepdims=True) - sm[0:1]
    states = [state_ref[h] for h in range(N_HEADS)]

    def unit(ci, h):
        cs = slice(h * HEAD_DIM, (h + 1) * HEAD_DIM)
        rs = slice(ci * HG_CHUNK, (ci + 1) * HG_CHUNK)
        lb_h = lb[:, cs]
        sig = jax.nn.sigmoid(f_ref[rs, cs])
        lf2 = jnp.log(lb_h + (1.0 - lb_h) * sig) * LOG2_E
        kk = (1.0 - lb_h) * (1.0 - sig)
        o, states[h] = _hgrn2_chunk(q_ref[rs, cs], kk, v_ref[rs, cs], lf2, states[h])
        o = o * lax.rsqrt(jnp.mean(o * o, axis=-1, keepdims=True) + LN_EPS)
        o_ref[rs, cs] = (o * ng_ref[:, cs] * _silu(g_ref[rs, cs])).astype(o_ref.dtype)
        if ci == n_chunks - 1:
            state_ref[h] = states[h]

    return [functools.partial(unit, ci, h) for ci in range(n_chunks) for h in range(N_HEADS)]


def _hgrn2_kernel(q_ref, f_ref, v_ref, g_ref, lbp_ref, ng_ref, o_ref, state_ref, *, layer):
    @pl.when(pl.program_id(1) == 0)
    def _():
        state_ref[...] = jnp.zeros(state_ref.shape, F32)

    for unit in _hgrn2_units(q_ref, f_ref, v_ref, g_ref, lbp_ref, ng_ref, o_ref, state_ref, layer):
        unit()


def hgrn2(p3, lower_bound_params, norm_g, l, tt):
    bsz, seq, _ = p3.shape
    tt = min(tt, seq)
    assert seq % tt == 0 and tt % HG_CHUNK == 0
    c = D_GROUP
    depth = lower_bound_params.shape[0]
    tok = lambda col: pl.BlockSpec((None, tt, c), lambda b, t: (b, t, col))
    return pl.pallas_call(
        functools.partial(_hgrn2_kernel, layer=l),
        grid=(bsz, seq // tt),
        in_specs=[
            tok(7), tok(8), tok(9), tok(10),
            pl.BlockSpec((depth, c), lambda b, t: (0, 0)),
            pl.BlockSpec((None, 1, c), lambda b, t: (l, 0, 0)),
        ],
        out_specs=pl.BlockSpec((None, tt, c), lambda b, t: (b, t, 0)),
        out_shape=jax.ShapeDtypeStruct((bsz, seq, c), F32),
        scratch_shapes=[pltpu.VMEM((N_HEADS, HEAD_DIM, HEAD_DIM), F32)],
        compiler_params=_params("parallel", "arbitrary"),
        name="hgrn2",
    )(p3, p3, p3, p3, lower_bound_params, norm_g)


def _out_proj_ln_units(x_ref, y_refs, w_ref, g_ref, b_ref, o_ref, alpha):
    c = D_GROUP
    rows = x_ref.shape[0]
    step = min(rows, OUT_PROJ_ROWS)
    units = []
    for r0 in range(0, rows, step):
        rs = slice(r0, r0 + step)
        acc = []

        def matmul(y_ref, k, rs=rs, acc=acc):
            part = jnp.dot(y_ref[rs, :].astype(BF16), w_ref[k * c:(k + 1) * c, :], preferred_element_type=F32)
            acc[:] = [part if not acc else acc[0] + part]

        def norm(rs=rs, acc=acc):
            z = alpha * x_ref[rs, :] + acc[0]
            o_ref[rs, :] = _standardize(z) * g_ref[...] + b_ref[...]

        units += [functools.partial(matmul, y_ref, k) for k, y_ref in enumerate(y_refs)]
        units.append(norm)
    return units


def _out_proj_ln_kernel(x_ref, ya_ref, yb_ref, yc_ref, yd_ref, w_ref, g_ref, b_ref, o_ref, *, alpha):
    for unit in _out_proj_ln_units(x_ref, (ya_ref, yb_ref, yc_ref, yd_ref), w_ref, g_ref, b_ref, o_ref, alpha):
        unit()


def out_proj_ln(x, ya, yb, yc, yd, w_out, ln_g, ln_b, l, jn, alpha, tm):
    m, d = x.shape
    c = D_GROUP
    tm = min(tm, m)
    assert m % tm == 0
    mix = lambda: pl.BlockSpec((tm, c), lambda i: (i, 0))
    return pl.pallas_call(
        functools.partial(_out_proj_ln_kernel, alpha=alpha),
        grid=(m // tm,),
        in_specs=[
            pl.BlockSpec((tm, d), lambda i: (i, 0)),
            mix(), mix(), mix(), mix(),
            pl.BlockSpec((None, 4 * c, d), lambda i: (l, 0, 0)),
            pl.BlockSpec((None, 1, d), lambda i: (jn, 0, 0)),
            pl.BlockSpec((None, 1, d), lambda i: (jn, 0, 0)),
        ],
        out_specs=pl.BlockSpec((tm, d), lambda i: (i, 0)),
        out_shape=jax.ShapeDtypeStruct((m, d), F32),
        compiler_params=_params("parallel"),
        name="out_proj_ln",
    )(x, ya, yb, yc, yd, w_out, ln_g, ln_b)


def kernel(x, ln_g, ln_b, ffn_w_gate, ffn_w_up, ffn_w_down, w_in, w_out, rg_conv_w, rg_conv_b,
           rg_w_a, rg_b_a, rg_w_x, rg_b_x, rg_lambda, cv_w, cv_b, cv_ln_g, cv_ln_b,
           hg_lower_bounds, hg_norm_g):
    bsz, seq, d = x.shape
    depth = w_in.shape[0]
    alpha = (2 * depth) ** 0.25
    m = bsz * seq

    wg = ffn_w_gate.astype(BF16)
    wu = ffn_w_up.astype(BF16)
    wd = ffn_w_down.astype(BF16)
    w_in_b = w_in.astype(BF16)
    w_out_b = w_out.astype(BF16)
    ln_g3 = ln_g.reshape(depth * 3, 1, d)
    ln_b3 = ln_b.reshape(depth * 3, 1, d)
    row = lambda a: a.reshape(depth, 1, a.shape[-1])

    h = x.reshape(m, d)
    for l in range(depth):
        h = ffn_ln(h, wg, wu, wd, ln_g3, ln_b3, l, 0, 3 * l, alpha, tm=1024, tf=FFN_COLS)
        p3 = in_proj(h, w_in_b, l, tm=1024, tn=IN_PROJ_COLS).reshape(bsz, seq, -1)
        y_a = moba(p3)
        y_b = rglru(p3, rg_conv_w, row(rg_conv_b), rg_w_a, row(rg_b_a), rg_w_x, row(rg_b_x),
                    row(rg_lambda), l, tt=1024)
        y_c = conformer(p3, cv_w, row(cv_b), row(cv_ln_g), row(cv_ln_b), l, tt=1024)
        y_d = hgrn2(p3, hg_lower_bounds, row(hg_norm_g), l, tt=512)
        flat = lambda y: y.reshape(m, D_GROUP)
        h = out_proj_ln(h, flat(y_a), flat(y_b), flat(y_c), flat(y_d), w_out_b, ln_g3, ln_b3,
                        l, 3 * l + 1, alpha, tm=512)
        h = ffn_ln(h, wg, wu, wd, ln_g3, ln_b3, l, 1, 3 * l + 2, alpha, tm=1024, tf=FFN_COLS)
    return h.reshape(bsz, seq, d)
```

```python
import functools

import jax
import jax.numpy as jnp
from jax import lax
from jax.experimental import pallas as pl
from jax.experimental.pallas import tpu as pltpu

F32 = jnp.float32
BF16 = jnp.bfloat16

LANES = 128
SUBLANES = 8
VMEM_LIMIT_BYTES = 56 * 1024 * 1024

D_GROUP = 512
HEAD_DIM = 128
N_HEADS = D_GROUP // HEAD_DIM
MOBA_BLOCK = 256
MOBA_TOPK = 3
MOBA_GROUP = 4
MOBA_Q_BLOCKS = 4
MOBA_Q_PARTS = 2
LOG2_E = 1.4426950408889634
MASK_VALUE = -1e30
RG_C = 8.0
RG_CONV = 4
CV_WIDTH = 31
CV_HALO = 32
FFN_COLS = 512
IN_PROJ_COLS = 1408
OUT_PROJ_ROWS = 256
FFN_DOWN_COLS = 512
FFN_NORM_ROWS = 256
HG_CHUNK = 64
HG_SUB = 8
LN_EPS = 1e-5

NT_DIMS = (((1,), (1,)), ((), ()))
TN_DIMS = (((0,), (0,)), ((), ()))


def _params(*sem):
    return pltpu.CompilerParams(dimension_semantics=sem, vmem_limit_bytes=VMEM_LIMIT_BYTES)


def _standardize(z):
    mu = jnp.mean(z, axis=-1, keepdims=True)
    zc = z - mu
    var = jnp.mean(zc * zc, axis=-1, keepdims=True)
    return zc * lax.rsqrt(var + LN_EPS)


def _silu(z):
    return z * jax.nn.sigmoid(z)


def _shift_rows(ext, d, halo, rows):
    if d % SUBLANES == 0:
        return ext[halo - d:halo - d + rows]
    r = d % SUBLANES
    base = d - r
    rolled = pltpu.roll(ext, r, axis=0)
    return rolled[halo - base:halo - base + rows]


def _ffn_ln_kernel(x_ref, wg_ref, wu_ref, wd_ref, g_ref, b_ref, o_ref, xb_ref, *, alpha):
    f = pl.program_id(1)
    last = pl.num_programs(1) - 1
    rows, d = o_ref.shape

    def hidden(xb):
        hg = jnp.dot(xb, wg_ref[...], preferred_element_type=F32)
        hu = jnp.dot(xb, wu_ref[...], preferred_element_type=F32)
        return (_silu(hg) * hu).astype(BF16)

    def down(h, accumulate):
        for c0 in range(0, d, FFN_DOWN_COLS):
            cs = slice(c0, min(c0 + FFN_DOWN_COLS, d))
            part = jnp.dot(h, wd_ref[:, cs], preferred_element_type=F32)
            o_ref[:, cs] = o_ref[:, cs] + part if accumulate else part

    @pl.when(f == 0)
    def _():
        xb = x_ref[...].astype(BF16)
        xb_ref[...] = xb
        down(hidden(xb), accumulate=False)

    @pl.when((f > 0) & (f < last))
    def _():
        down(hidden(xb_ref[...]), accumulate=True)

    @pl.when(f == last)
    def _():
        h = hidden(xb_ref[...])
        for r0 in range(0, rows, FFN_NORM_ROWS):
            rs = slice(r0, min(r0 + FFN_NORM_ROWS, rows))
            acc = o_ref[rs, :] + jnp.dot(h[rs, :], wd_ref[...], preferred_element_type=F32)
            z = alpha * x_ref[rs, :] + 0.5 * acc
            o_ref[rs, :] = _standardize(z) * g_ref[...] + b_ref[...]


def ffn_ln(x, wg, wu, wd, ln_g, ln_b, l, j, jn, alpha, tm, tf):
    m, d = x.shape
    dff = wg.shape[-1]
    tm = min(tm, m)
    tf = min(tf, dff)
    assert m % tm == 0 and dff % tf == 0
    grid = (m // tm, dff // tf)
    return pl.pallas_call(
        functools.partial(_ffn_ln_kernel, alpha=alpha),
        grid=grid,
        in_specs=[
            pl.BlockSpec((tm, d), lambda i, f: (i, 0)),
            pl.BlockSpec((None, None, d, tf), lambda i, f: (l, j, 0, f)),
            pl.BlockSpec((None, None, d, tf), lambda i, f: (l, j, 0, f)),
            pl.BlockSpec((None, None, tf, d), lambda i, f: (l, j, f, 0)),
            pl.BlockSpec((None, 1, d), lambda i, f: (jn, 0, 0)),
            pl.BlockSpec((None, 1, d), lambda i, f: (jn, 0, 0)),
        ],
        out_specs=pl.BlockSpec((tm, d), lambda i, f: (i, 0)),
        out_shape=jax.ShapeDtypeStruct((m, d), F32),
        scratch_shapes=[pltpu.VMEM((tm, d), BF16)],
        compiler_params=_params("parallel", "arbitrary"),
        name="ffn_ln",
    )(x, wg, wu, wd, ln_g, ln_b)


def _in_proj_kernel(x_ref, w_ref, o_ref, xb_ref):
    c = pl.program_id(1)

    @pl.when(c == 0)
    def _():
        xb = x_ref[...].astype(BF16)
        xb_ref[...] = xb
        o_ref[...] = jnp.dot(xb, w_ref[...], preferred_element_type=F32)

    @pl.when(c > 0)
    def _():
        o_ref[...] = jnp.dot(xb_ref[...], w_ref[...], preferred_element_type=F32)


def in_proj(x, w_in, l, tm, tn):
    m, d = x.shape
    n = w_in.shape[-1]
    tm = min(tm, m)
    assert m % tm == 0 and n % tn == 0
    return pl.pallas_call(
        _in_proj_kernel,
        grid=(m // tm, n // tn),
        in_specs=[
            pl.BlockSpec((tm, d), lambda i, c: (i, 0)),
            pl.BlockSpec((None, d, tn), lambda i, c: (l, 0, c)),
        ],
        out_specs=pl.BlockSpec((tm, tn), lambda i, c: (i, c)),
        out_shape=jax.ShapeDtypeStruct((m, n), F32),
        scratch_shapes=[pltpu.VMEM((tm, d), BF16)],
        compiler_params=_params("parallel", "arbitrary"),
        name="in_proj",
    )(x, w_in)


def _fold_rows_max(a):
    rows, cols = a.shape
    return jnp.max(a.reshape(rows // SUBLANES, SUBLANES, cols), axis=0)


def _moba_kernel(q_ref, k_ref, v_ref, o_ref, qat_ref, ka_ref, vat_ref, vat3_ref, s_ref, so_ref, *, nb, scale):
    j = pl.program_id(2)
    blk = MOBA_BLOCK
    grp = MOBA_GROUP
    dh = HEAD_DIM
    seq = nb * blk
    tq = MOBA_Q_BLOCKS * blk
    span = grp * blk
    nbp = -(-nb // SUBLANES) * SUBLANES

    @pl.when(j == 0)
    def _():
        k = k_ref[...]
        lane = lax.broadcasted_iota(jnp.int32, (seq, LANES), 1)
        tile = lax.broadcasted_iota(jnp.int32, (nb, blk, LANES), 0).reshape(seq, LANES)
        ka_ref[:, 0:dh] = k.astype(BF16)
        ka_ref[:, dh:2 * dh] = jnp.where(tile == lane, 1.0, 0.0).astype(BF16)
        vat = jnp.concatenate([v_ref[...].T, jnp.ones((dh, seq), F32)], axis=0).astype(BF16)
        vat_ref[...] = vat
        for n in range(nb):
            vat3_ref[n] = vat[:, n * blk:(n + 1) * blk]
        k_mean = jnp.mean(k.reshape(nb, blk, dh), axis=1)
        k_mean = jnp.concatenate([k_mean, jnp.zeros((LANES - nb, dh), F32)], axis=0)

        q = q_ref[...]
        gate = lax.dot_general(k_mean.astype(BF16), q.astype(BF16), NT_DIMS, preferred_element_type=F32)
        g = gate[0:nbp]
        blk_row = lax.broadcasted_iota(jnp.int32, (nbp, seq), 0)
        q_blk = lax.shift_right_logical(lax.broadcasted_iota(jnp.int32, (nbp, seq), 1), blk.bit_length() - 1)
        row_f = blk_row.astype(F32)
        past = blk_row < q_blk
        g = jnp.where(past, g, MASK_VALUE)
        sel = jnp.zeros((nbp, seq), F32)
        for _ in range(MOBA_TOPK):
            top = jnp.max(g, axis=0, keepdims=True)
            first = jnp.min(jnp.where(g == top, row_f, float(LANES)), axis=0, keepdims=True)
            pick = row_f == first
            sel = jnp.where(pick, 1.0, sel)
            g = jnp.where(pick, -jnp.inf, g)
        bias = jnp.where(past & (sel > 0.5), 0.0, MASK_VALUE)
        qt = (q * (scale * LOG2_E)).T
        qat = jnp.concatenate([qt, bias, jnp.zeros((dh - nbp, seq), F32)], axis=0).astype(BF16)
        for t in range(seq // tq):
            qat_ref[t] = qat[:, t * tq:(t + 1) * tq]

    qat = qat_ref[j]
    first_blk = j * MOBA_Q_BLOCKS

    row = lax.broadcasted_iota(jnp.int32, (blk, blk), 0)
    col = lax.broadcasted_iota(jnp.int32, (blk, blk), 1)
    own_max = []
    for c in range(MOBA_Q_BLOCKS):
        start = pl.multiple_of((first_blk + c) * blk, blk)
        s = jnp.dot(ka_ref[pl.ds(start, blk), 0:dh], qat[0:dh, c * blk:(c + 1) * blk],
                    preferred_element_type=F32)
        s = jnp.where(row <= col, s, MASK_VALUE)
        so_ref[c] = s
        own_max.append(_fold_rows_max(s))
    mrun0 = jnp.concatenate(own_max, axis=1)

    def attend(tile):
        width = tq // MOBA_Q_PARTS
        own_per_part = MOBA_Q_BLOCKS // MOBA_Q_PARTS
        extents = [(tile * MOBA_Q_BLOCKS + (part + 1) * own_per_part - 1) * blk for part in range(MOBA_Q_PARTS)]
        pieces = [[(k0, min(k0 + span, ext)) for k0 in range(0, ext, span)] for ext in extents]
        maxes = []
        for part in range(MOBA_Q_PARTS):
            qs = slice(part * width, (part + 1) * width)
            mrun = mrun0[:, qs]
            for g, (k0, k1) in enumerate(pieces[part]):
                s = jnp.dot(ka_ref[k0:k1, :], qat[:, qs], preferred_element_type=F32)
                s_ref[g, 0:k1 - k0, qs] = s
                mrun = jnp.maximum(mrun, _fold_rows_max(s))
            maxes.append(jnp.max(mrun, axis=0, keepdims=True))
        outs = []
        for part in range(MOBA_Q_PARTS):
            qs = slice(part * width, (part + 1) * width)
            m = maxes[part]
            cols = []
            for c in range(own_per_part):
                p = jnp.exp2((so_ref[part * own_per_part + c] - m[:, c * blk:(c + 1) * blk]).astype(BF16))
                cols.append(jnp.dot(vat3_ref[first_blk + part * own_per_part + c], p,
                                    preferred_element_type=F32))
            acc = jnp.concatenate(cols, axis=1)
            for g, (k0, k1) in enumerate(pieces[part]):
                p = jnp.exp2((s_ref[g, 0:k1 - k0, qs] - m).astype(BF16))
                acc = acc + jnp.dot(vat_ref[:, k0:k1], p, preferred_element_type=F32)
            outs.append(acc[0:dh] / acc[dh:2 * dh])
        o_ref[...] = jnp.concatenate(outs, axis=1).T

    for tile in range(seq // tq):
        pl.when(j == tile)(functools.partial(attend, tile))


def moba(p3):
    bsz, seq, _ = p3.shape
    blk = MOBA_BLOCK
    grp = MOBA_GROUP
    tq = MOBA_Q_BLOCKS * blk
    assert seq % (blk * grp) == 0 and seq % tq == 0
    nb = seq // blk
    assert nb <= HEAD_DIM
    max_count = (nb - 1 + grp - 1) // grp
    kernel = functools.partial(_moba_kernel, nb=nb, scale=HEAD_DIM ** -0.5)
    col = lambda c: pl.BlockSpec((None, seq, HEAD_DIM), lambda b, h, j: (b, 0, c * N_HEADS + h))
    return pl.pallas_call(
        kernel,
        grid=(bsz, N_HEADS, seq // tq),
        in_specs=[col(0), col(1), col(2)],
        out_specs=pl.BlockSpec((None, tq, HEAD_DIM), lambda b, h, j: (b, j, h)),
        out_shape=jax.ShapeDtypeStruct((bsz, seq, D_GROUP), F32),
        scratch_shapes=[
            pltpu.VMEM((seq // tq, 2 * HEAD_DIM, tq), BF16),
            pltpu.VMEM((seq, 2 * HEAD_DIM), BF16),
            pltpu.VMEM((2 * HEAD_DIM, seq), BF16),
            pltpu.VMEM((nb, 2 * HEAD_DIM, blk), BF16),
            pltpu.VMEM((max_count, grp * blk, tq), F32),
            pltpu.VMEM((MOBA_Q_BLOCKS, blk, blk), F32),
        ],
        compiler_params=_params("parallel", "parallel", "arbitrary"),
        name="moba",
    )(p3, p3, p3)


def _rglru_kernel(gate_ref, x_ref, cw_ref, cb_ref, wa_ref, ba_ref, wx_ref, bx_ref, lam_ref,
                  o_ref, halo_ref, h_ref):
    t = pl.program_id(1)
    rows = x_ref.shape[0]

    @pl.when(t == 0)
    def _():
        halo_ref[...] = jnp.zeros(halo_ref.shape, F32)
        h_ref[...] = jnp.zeros(h_ref.shape, F32)

    x = x_ref[...]
    ext = jnp.concatenate([halo_ref[...], x], axis=0)
    halo_ref[...] = x[rows - SUBLANES:]
    xc = jnp.zeros_like(x) + cb_ref[...]
    for j in range(RG_CONV):
        xc = xc + cw_ref[j:j + 1, :] * _shift_rows(ext, RG_CONV - 1 - j, SUBLANES, rows)

    xcb = xc.astype(BF16)
    r_parts, i_parts = [], []
    for g in range(D_GROUP // HEAD_DIM):
        blk = xcb[:, g * HEAD_DIM:(g + 1) * HEAD_DIM]
        r_parts.append(jnp.dot(blk, wa_ref[g].astype(BF16), preferred_element_type=F32))
        i_parts.append(jnp.dot(blk, wx_ref[g].astype(BF16), preferred_element_type=F32))
    r = jax.nn.sigmoid(jnp.concatenate(r_parts, axis=-1) + ba_ref[...])
    ig = jax.nn.sigmoid(jnp.concatenate(i_parts, axis=-1) + bx_ref[...])
    neg_lam = -lam_ref[...]
    softplus = jnp.maximum(neg_lam, 0.0) + jnp.log1p(jnp.exp(-jnp.abs(neg_lam)))
    log_a = -RG_C * r * softplus
    a = jnp.exp(log_a)
    u = jnp.sqrt(jnp.maximum(1.0 - jnp.exp(2.0 * log_a), 0.0)) * (ig * xc)

    in_group = lax.broadcasted_iota(jnp.int32, a.shape, 0) & (SUBLANES - 1)
    big_a, big_b = a, u
    k = 1
    while k < SUBLANES:
        a_sh = jnp.where(in_group >= k, pltpu.roll(big_a, k, axis=0), 1.0)
        b_sh = jnp.where(in_group >= k, pltpu.roll(big_b, k, axis=0), 0.0)
        big_b = big_a * b_sh + big_b
        big_a = big_a * a_sh
        k *= 2
    carry = h_ref[0:1, :]
    groups = []
    for r0 in range(0, rows, SUBLANES):
        h_grp = big_a[r0:r0 + SUBLANES] * carry + big_b[r0:r0 + SUBLANES]
        groups.append(h_grp)
        carry = h_grp[SUBLANES - 1:SUBLANES]
    h = jnp.concatenate(groups, axis=0)
    h_ref[0:1, :] = carry
    o_ref[...] = h * jax.nn.gelu(gate_ref[...], approximate=True)


def rglru(p3, conv_w, conv_b, w_a, b_a, w_x, b_x, lam, l, tt):
    bsz, seq, _ = p3.shape
    tt = min(tt, seq)
    assert seq % tt == 0
    c = D_GROUP
    vec = lambda: pl.BlockSpec((None, 1, c), lambda b, t: (l, 0, 0))
    return pl.pallas_call(
        _rglru_kernel,
        grid=(bsz, seq // tt),
        in_specs=[
            pl.BlockSpec((None, tt, c), lambda b, t: (b, t, 3)),
            pl.BlockSpec((None, tt, c), lambda b, t: (b, t, 4)),
            pl.BlockSpec((None, RG_CONV, c), lambda b, t: (l, 0, 0)),
            vec(),
            pl.BlockSpec((None, c // HEAD_DIM, HEAD_DIM, HEAD_DIM), lambda b, t: (l, 0, 0, 0)),
            vec(),
            pl.BlockSpec((None, c // HEAD_DIM, HEAD_DIM, HEAD_DIM), lambda b, t: (l, 0, 0, 0)),
            vec(),
            vec(),
        ],
        out_specs=pl.BlockSpec((None, tt, c), lambda b, t: (b, t, 0)),
        out_shape=jax.ShapeDtypeStruct((bsz, seq, c), F32),
        scratch_shapes=[pltpu.VMEM((SUBLANES, c), F32), pltpu.VMEM((SUBLANES, c), F32)],
        compiler_params=_params("parallel", "arbitrary"),
        name="rglru",
    )(p3, p3, conv_w, conv_b, w_a, b_a, w_x, b_x, lam)


def _conformer_units(val_ref, gate_ref, cw_ref, cb_ref, ng_ref, nb_ref, o_ref, halo_ref):
    rows = val_ref.shape[0]

    def unit(g):
        cs = slice(g * HEAD_DIM, (g + 1) * HEAD_DIM)
        glu = val_ref[:, cs] * jax.nn.sigmoid(gate_ref[:, cs])
        ext = jnp.concatenate([halo_ref[:, cs], glu], axis=0)
        halo_ref[:, cs] = glu[rows - CV_HALO:]
        rolled = [ext] + [pltpu.roll(ext, r, axis=0) for r in range(1, SUBLANES)]
        u = jnp.zeros_like(glu) + cb_ref[:, cs]
        for j in range(CV_WIDTH):
            d = CV_WIDTH - 1 - j
            r = d % SUBLANES
            start = CV_HALO - (d - r)
            u = u + cw_ref[j:j + 1, cs] * rolled[r][start:start + rows]
        o_ref[:, cs] = _silu(_standardize(u) * ng_ref[:, cs] + nb_ref[:, cs]).astype(o_ref.dtype)

    return [functools.partial(unit, g) for g in range(D_GROUP // HEAD_DIM)]


def _conformer_kernel(val_ref, gate_ref, cw_ref, cb_ref, ng_ref, nb_ref, o_ref, halo_ref):
    @pl.when(pl.program_id(1) == 0)
    def _():
        halo_ref[...] = jnp.zeros(halo_ref.shape, F32)

    for unit in _conformer_units(val_ref, gate_ref, cw_ref, cb_ref, ng_ref, nb_ref, o_ref, halo_ref):
        unit()


def conformer(p3, cv_w, cv_b, ng, nb, l, tt):
    bsz, seq, _ = p3.shape
    tt = min(tt, seq)
    assert seq % tt == 0 and tt >= CV_HALO
    c = D_GROUP
    vec = lambda: pl.BlockSpec((None, 1, c), lambda b, t: (l, 0, 0))
    return pl.pallas_call(
        _conformer_kernel,
        grid=(bsz, seq // tt),
        in_specs=[
            pl.BlockSpec((None, tt, c), lambda b, t: (b, t, 5)),
            pl.BlockSpec((None, tt, c), lambda b, t: (b, t, 6)),
            pl.BlockSpec((None, CV_WIDTH, c), lambda b, t: (l, 0, 0)),
            vec(), vec(), vec(),
        ],
        out_specs=pl.BlockSpec((None, tt, c), lambda b, t: (b, t, 0)),
        out_shape=jax.ShapeDtypeStruct((bsz, seq, c), F32),
        scratch_shapes=[pltpu.VMEM((CV_HALO, c), F32)],
        compiler_params=_params("parallel", "arbitrary"),
        name="conformer",
    )(p3, p3, cv_w, cv_b, ng, nb)


def _hgrn2_chunk(q, kk, v, lf2, state_t):
    c = HG_CHUNK
    sub = HG_SUB
    row = lax.broadcasted_iota(jnp.int32, (c, HEAD_DIM), 0)
    b = lf2
    k = 1
    while k < c:
        b = b + jnp.where(row >= k, pltpu.roll(b, k, axis=0), 0.0)
        k *= 2
    vb = v.astype(BF16)

    o = lax.dot_general((q * jnp.exp2(b)).astype(BF16), state_t.astype(BF16), NT_DIMS,
                        preferred_element_type=F32)

    lane_s = lax.broadcasted_iota(jnp.int32, (sub, c), 1)
    row_s = lax.broadcasted_iota(jnp.int32, (sub, c), 0)
    att_rows = []
    for blk in range(c // sub):
        lo = blk * sub
        q_i = q[lo:lo + sub]
        b_i = b[lo:lo + sub]
        k_i = kk[lo:lo + sub]
        att = jnp.zeros((sub, c), F32)
        if blk > 0:
            ref = b[lo - 1:lo]
            qs = (q_i * jnp.exp2(b_i - ref)).astype(BF16)
            ks = kk[:lo] * jnp.exp2(ref - b[:lo])
            ks = jnp.concatenate([ks, jnp.zeros((c - lo, HEAD_DIM), F32)], axis=0).astype(BF16)
            att = lax.dot_general(qs, ks, NT_DIMS, preferred_element_type=F32)
        for s in range(sub):
            decay = jnp.exp2(jnp.minimum(b_i - b_i[s:s + 1], 0.0))
            w = jnp.sum(q_i * k_i[s:s + 1] * decay, axis=-1, keepdims=True)
            att = att + jnp.where((lane_s == lo + s) & (row_s >= s), w, 0.0)
        att_rows.append(att)
    att = jnp.concatenate(att_rows, axis=0)
    o = o + jnp.dot(att.astype(BF16), vb, preferred_element_type=F32)

    b_last = b[c - 1:c]
    ks = (kk * jnp.exp2(b_last - b)).astype(BF16)
    new_state_t = state_t * jnp.exp2(b_last) + lax.dot_general(vb, ks, TN_DIMS, preferred_element_type=F32)
    return o, new_state_t


def _hgrn2_units(q_ref, f_ref, v_ref, g_ref, lbp_ref, ng_ref, o_ref, state_ref, layer):
    rows = q_ref.shape[0]
    n_chunks = rows // HG_CHUNK

    lbp = lbp_ref[...]
    e = jnp.exp(lbp - jnp.max(lbp, axis=0, keepdims=True))
    sm = e / jnp.sum(e, axis=0, keepdims=True)
    lb = jnp.sum(sm[0:layer + 1], axis=0, keepdims=True) - sm[0:1]
    states = [state_ref[h] for h in range(N_HEADS)]

    def unit(ci, h):
        cs = slice(h * HEAD_DIM, (h + 1) * HEAD_DIM)
        rs = slice(ci * HG_CHUNK, (ci + 1) * HG_CHUNK)
        lb_h = lb[:, cs]
        sig = jax.nn.sigmoid(f_ref[rs, cs])
        lf2 = jnp.log(lb_h + (1.0 - lb_h) * sig) * LOG2_E
        kk = (1.0 - lb_h) * (1.0 - sig)
        o, states[h] = _hgrn2_chunk(q_ref[rs, cs], kk, v_ref[rs, cs], lf2, states[h])
        o = o * lax.rsqrt(jnp.mean(o * o, axis=-1, keepdims=True) + LN_EPS)
        o_ref[rs, cs] = (o * ng_ref[:, cs] * _silu(g_ref[rs, cs])).astype(o_ref.dtype)
        if ci == n_chunks - 1:
            state_ref[h] = states[h]

    return [functools.partial(unit, ci, h) for ci in range(n_chunks) for h in range(N_HEADS)]


def _hgrn2_kernel(q_ref, f_ref, v_ref, g_ref, lbp_ref, ng_ref, o_ref, state_ref, *, layer):
    @pl.when(pl.program_id(1) == 0)
    def _():
        state_ref[...] = jnp.zeros(state_ref.shape, F32)

    for unit in _hgrn2_units(q_ref, f_ref, v_ref, g_ref, lbp_ref, ng_ref, o_ref, state_ref, layer):
        unit()


def hgrn2(p3, lower_bound_params, norm_g, l, tt):
    bsz, seq, _ = p3.shape
    tt = min(tt, seq)
    assert seq % tt == 0 and tt % HG_CHUNK == 0
    c = D_GROUP
    depth = lower_bound_params.shape[0]
    tok = lambda col: pl.BlockSpec((None, tt, c), lambda b, t: (b, t, col))
    return pl.pallas_call(
        functools.partial(_hgrn2_kernel, layer=l),
        grid=(bsz, seq // tt),
        in_specs=[
            tok(7), tok(8), tok(9), tok(10),
            pl.BlockSpec((depth, c), lambda b, t: (0, 0)),
            pl.BlockSpec((None, 1, c), lambda b, t: (l, 0, 0)),
        ],
        out_specs=pl.BlockSpec((None, tt, c), lambda b, t: (b, t, 0)),
        out_shape=jax.ShapeDtypeStruct((bsz, seq, c), F32),
        scratch_shapes=[pltpu.VMEM((N_HEADS, HEAD_DIM, HEAD_DIM), F32)],
        compiler_params=_params("parallel", "arbitrary"),
        name="hgrn2",
    )(p3, p3, p3, p3, lower_bound_params, norm_g)


def _out_proj_ln_units(x_ref, y_refs, w_ref, g_ref, b_ref, o_ref, alpha):
    c = D_GROUP
    rows = x_ref.shape[0]
    step = min(rows, OUT_PROJ_ROWS)
    units = []
    for r0 in range(0, rows, step):
        rs = slice(r0, r0 + step)
        acc = []

        def matmul(y_ref, k, rs=rs, acc=acc):
            part = jnp.dot(y_ref[rs, :].astype(BF16), w_ref[k * c:(k + 1) * c, :], preferred_element_type=F32)
            acc[:] = [part if not acc else acc[0] + part]

        def norm(rs=rs, acc=acc):
            z = alpha * x_ref[rs, :] + acc[0]
            o_ref[rs, :] = _standardize(z) * g_ref[...] + b_ref[...]

        units += [functools.partial(matmul, y_ref, k) for k, y_ref in enumerate(y_refs)]
        units.append(norm)
    return units


def _out_proj_ln_kernel(x_ref, ya_ref, yb_ref, yc_ref, yd_ref, w_ref, g_ref, b_ref, o_ref, *, alpha):
    for unit in _out_proj_ln_units(x_ref, (ya_ref, yb_ref, yc_ref, yd_ref), w_ref, g_ref, b_ref, o_ref, alpha):
        unit()


def out_proj_ln(x, ya, yb, yc, yd, w_out, ln_g, ln_b, l, jn, alpha, tm):
    m, d = x.shape
    c = D_GROUP
    tm = min(tm, m)
    assert m % tm == 0
    mix = lambda: pl.BlockSpec((tm, c), lambda i: (i, 0))
    return pl.pallas_call(
        functools.partial(_out_proj_ln_kernel, alpha=alpha),
        grid=(m // tm,),
        in_specs=[
            pl.BlockSpec((tm, d), lambda i: (i, 0)),
            mix(), mix(), mix(), mix(),
            pl.BlockSpec((None, 4 * c, d), lambda i: (l, 0, 0)),
            pl.BlockSpec((None, 1, d), lambda i: (jn, 0, 0)),
            pl.BlockSpec((None, 1, d), lambda i: (jn, 0, 0)),
        ],
        out_specs=pl.BlockSpec((tm, d), lambda i: (i, 0)),
        out_shape=jax.ShapeDtypeStruct((m, d), F32),
        compiler_params=_params("parallel"),
        name="out_proj_ln",
    )(x, ya, yb, yc, yd, w_out, ln_g, ln_b)


def kernel(x, ln_g, ln_b, ffn_w_gate, ffn_w_up, ffn_w_down, w_in, w_out, rg_conv_w, rg_conv_b,
           rg_w_a, rg_b_a, rg_w_x, rg_b_x, rg_lambda, cv_w, cv_b, cv_ln_g, cv_ln_b,
           hg_lower_bounds, hg_norm_g):
    bsz, seq, d = x.shape
    depth = w_in.shape[0]
    alpha = (2 * depth) ** 0.25
    m = bsz * seq

    wg = ffn_w_gate.astype(BF16)
    wu = ffn_w_up.astype(BF16)
    wd = ffn_w_down.astype(BF16)
    w_in_b = w_in.astype(BF16)
    w_out_b = w_out.astype(BF16)
    ln_g3 = ln_g.reshape(depth * 3, 1, d)
    ln_b3 = ln_b.reshape(depth * 3, 1, d)
    row = lambda a: a.reshape(depth, 1, a.shape[-1])

    h = x.reshape(m, d)
    for l in range(depth):
        h = ffn_ln(h, wg, wu, wd, ln_g3, ln_b3, l, 0, 3 * l, alpha, tm=1024, tf=FFN_COLS)
        p3 = in_proj(h, w_in_b, l, tm=1024, tn=IN_PROJ_COLS).reshape(bsz, seq, -1)
        y_a = moba(p3)
        y_b = rglru(p3, rg_conv_w, row(rg_conv_b), rg_w_a, row(rg_b_a), rg_w_x, row(rg_b_x),
                    row(rg_lambda), l, tt=1024)
        y_c = conformer(p3, cv_w, row(cv_b), row(cv_ln_g), row(cv_ln_b), l, tt=1024)
        y_d = hgrn2(p3, hg_lower_bounds, row(hg_norm_g), l, tt=512)
        flat = lambda y: y.reshape(m, D_GROUP)
        h = out_proj_ln(h, flat(y_a), flat(y_b), flat(y_c), flat(y_d), w_out_b, ln_g3, ln_b3,
                        l, 3 * l + 1, alpha, tm=512)
        h = ffn_ln(h, wg, wu, wd, ln_g3, ln_b3, l, 1, 3 * l + 2, alpha, tm=1024, tf=FFN_COLS)
    return h.reshape(bsz, seq, d)
```

```python
import functools

import jax
import jax.numpy as jnp
from jax import lax
from jax.experimental import pallas as pl
from jax.experimental.pallas import tpu as pltpu

F32 = jnp.float32
BF16 = jnp.bfloat16

LANES = 128
SUBLANES = 8
VMEM_LIMIT_BYTES = 56 * 1024 * 1024

D_GROUP = 512
HEAD_DIM = 128
N_HEADS = D_GROUP // HEAD_DIM
MOBA_BLOCK = 256
MOBA_TOPK = 3
MOBA_GROUP = 4
MOBA_Q_BLOCKS = 4
MOBA_Q_PARTS = 2
LOG2_E = 1.4426950408889634
MASK_VALUE = -1e30
RG_C = 8.0
RG_CONV = 4
CV_WIDTH = 31
CV_HALO = 32
FFN_COLS = 512
IN_PROJ_COLS = 1408
OUT_PROJ_ROWS = 256
FFN_DOWN_COLS = 512
FFN_NORM_ROWS = 256
HG_CHUNK = 64
HG_SUB = 8
LN_EPS = 1e-5

NT_DIMS = (((1,), (1,)), ((), ()))
TN_DIMS = (((0,), (0,)), ((), ()))


def _params(*sem):
    return pltpu.CompilerParams(dimension_semantics=sem, vmem_limit_bytes=VMEM_LIMIT_BYTES)


def _standardize(z):
    mu = jnp.mean(z, axis=-1, keepdims=True)
    zc = z - mu
    var = jnp.mean(zc * zc, axis=-1, keepdims=True)
    return zc * lax.rsqrt(var + LN_EPS)


def _silu(z):
    return z * jax.nn.sigmoid(z)


def _shift_rows(ext, d, halo, rows):
    if d % SUBLANES == 0:
        return ext[halo - d:halo - d + rows]
    r = d % SUBLANES
    base = d - r
    rolled = pltpu.roll(ext, r, axis=0)
    return rolled[halo - base:halo - base + rows]


def _ffn_ln_kernel(x_ref, wgu_ref, wd_ref, g_ref, b_ref, o_ref, xb_ref, *, alpha):
    f = pl.program_id(1)
    last = pl.num_programs(1) - 1
    rows, d = o_ref.shape
    tf = wd_ref.shape[0]

    def hidden(xb):
        hgu = jnp.dot(xb, wgu_ref[...], preferred_element_type=F32)
        return (_silu(hgu[:, 0:tf]) * hgu[:, tf:2 * tf]).astype(BF16)

    def down(h, accumulate):
        for c0 in range(0, d, FFN_DOWN_COLS):
            cs = slice(c0, min(c0 + FFN_DOWN_COLS, d))
            part = jnp.dot(h, wd_ref[:, cs], preferred_element_type=F32)
            o_ref[:, cs] = o_ref[:, cs] + part if accumulate else part

    @pl.when(f == 0)
    def _():
        xb = x_ref[...].astype(BF16)
        xb_ref[...] = xb
        down(hidden(xb), accumulate=False)

    @pl.when((f > 0) & (f < last))
    def _():
        down(hidden(xb_ref[...]), accumulate=True)

    @pl.when(f == last)
    def _():
        h = hidden(xb_ref[...])
        for r0 in range(0, rows, FFN_NORM_ROWS):
            rs = slice(r0, min(r0 + FFN_NORM_ROWS, rows))
            acc = o_ref[rs, :] + jnp.dot(h[rs, :], wd_ref[...], preferred_element_type=F32)
            z = alpha * x_ref[rs, :] + 0.5 * acc
            o_ref[rs, :] = _standardize(z) * g_ref[...] + b_ref[...]


def interleave_cols(a, b, t):
    *lead, d, f = a.shape
    assert f % t == 0 and b.shape == a.shape
    parts = [w.reshape(*lead, d, f // t, 1, t) for w in (a, b)]
    return jnp.concatenate(parts, axis=-2).reshape(*lead, d, 2 * f)


def ffn_ln(x, wgu, wd, ln_g, ln_b, l, j, jn, alpha, tm, tf):
    m, d = x.shape
    dff = wd.shape[-2]
    tm = min(tm, m)
    assert m % tm == 0 and dff % tf == 0 and dff // tf >= 2 and wgu.shape[-1] == 2 * dff
    grid = (m // tm, dff // tf)
    return pl.pallas_call(
        functools.partial(_ffn_ln_kernel, alpha=alpha),
        grid=grid,
        in_specs=[
            pl.BlockSpec((tm, d), lambda i, f: (i, 0)),
            pl.BlockSpec((None, None, d, 2 * tf), lambda i, f: (l, j, 0, f)),
            pl.BlockSpec((None, None, tf, d), lambda i, f: (l, j, f, 0)),
            pl.BlockSpec((None, 1, d), lambda i, f: (jn, 0, 0)),
            pl.BlockSpec((None, 1, d), lambda i, f: (jn, 0, 0)),
        ],
        out_specs=pl.BlockSpec((tm, d), lambda i, f: (i, 0)),
        out_shape=jax.ShapeDtypeStruct((m, d), F32),
        scratch_shapes=[pltpu.VMEM((tm, d), BF16)],
        compiler_params=_params("parallel", "arbitrary"),
        name="ffn_ln",
    )(x, wgu, wd, ln_g, ln_b)


def _in_proj_kernel(x_ref, w_ref, o_ref, xb_ref):
    c = pl.program_id(1)

    @pl.when(c == 0)
    def _():
        xb = x_ref[...].astype(BF16)
        xb_ref[...] = xb
        o_ref[...] = jnp.dot(xb, w_ref[...], preferred_element_type=F32)

    @pl.when(c > 0)
    def _():
        o_ref[...] = jnp.dot(xb_ref[...], w_ref[...], preferred_element_type=F32)


def in_proj(x, w_in, l, tm, tn):
    m, d = x.shape
    n = w_in.shape[-1]
    tm = min(tm, m)
    assert m % tm == 0 and n % tn == 0
    return pl.pallas_call(
        _in_proj_kernel,
        grid=(m // tm, n // tn),
        in_specs=[
            pl.BlockSpec((tm, d), lambda i, c: (i, 0)),
            pl.BlockSpec((None, d, tn), lambda i, c: (l, 0, c)),
        ],
        out_specs=pl.BlockSpec((tm, tn), lambda i, c: (i, c)),
        out_shape=jax.ShapeDtypeStruct((m, n), F32),
        scratch_shapes=[pltpu.VMEM((tm, d), BF16)],
        compiler_params=_params("parallel", "arbitrary"),
        name="in_proj",
    )(x, w_in)


def _fold_rows_max(a):
    rows, cols = a.shape
    return jnp.max(a.reshape(rows // SUBLANES, SUBLANES, cols), axis=0)


def _moba_kernel(q_ref, k_ref, v_ref, o_ref, qat_ref, ka_ref, vat_ref, vat3_ref, s_ref, so_ref, *, nb, scale):
    j = pl.program_id(2)
    blk = MOBA_BLOCK
    grp = MOBA_GROUP
    dh = HEAD_DIM
    seq = nb * blk
    tq = MOBA_Q_BLOCKS * blk
    span = grp * blk
    nbp = -(-nb // SUBLANES) * SUBLANES

    @pl.when(j == 0)
    def _():
        k = k_ref[...]
        lane = lax.broadcasted_iota(jnp.int32, (seq, LANES), 1)
        tile = lax.broadcasted_iota(jnp.int32, (nb, blk, LANES), 0).reshape(seq, LANES)
        ka_ref[:, 0:dh] = k.astype(BF16)
        ka_ref[:, dh:2 * dh] = jnp.where(tile == lane, 1.0, 0.0).astype(BF16)
        vat = jnp.concatenate([v_ref[...].T, jnp.ones((dh, seq), F32)], axis=0).astype(BF16)
        vat_ref[...] = vat
        for n in range(nb):
            vat3_ref[n] = vat[:, n * blk:(n + 1) * blk]
        k_mean = jnp.mean(k.reshape(nb, blk, dh), axis=1)
        k_mean = jnp.concatenate([k_mean, jnp.zeros((LANES - nb, dh), F32)], axis=0)

        q = q_ref[...]
        gate = lax.dot_general(k_mean.astype(BF16), q.astype(BF16), NT_DIMS, preferred_element_type=F32)
        g = gate[0:nbp]
        blk_row = lax.broadcasted_iota(jnp.int32, (nbp, seq), 0)
        q_blk = lax.shift_right_logical(lax.broadcasted_iota(jnp.int32, (nbp, seq), 1), blk.bit_length() - 1)
        row_f = blk_row.astype(F32)
        past = blk_row < q_blk
        g = jnp.where(past, g, MASK_VALUE)
        sel = jnp.zeros((nbp, seq), F32)
        for _ in range(MOBA_TOPK):
            top = jnp.max(g, axis=0, keepdims=True)
            first = jnp.min(jnp.where(g == top, row_f, float(LANES)), axis=0, keepdims=True)
            pick = row_f == first
            sel = jnp.where(pick, 1.0, sel)
            g = jnp.where(pick, -jnp.inf, g)
        bias = jnp.where(past & (sel > 0.5), 0.0, MASK_VALUE)
        qt = (q * (scale * LOG2_E)).T
        qat = jnp.concatenate([qt, bias, jnp.zeros((dh - nbp, seq), F32)], axis=0).astype(BF16)
        for t in range(seq // tq):
            qat_ref[t] = qat[:, t * tq:(t + 1) * tq]

    qat = qat_ref[j]
    first_blk = j * MOBA_Q_BLOCKS

    row = lax.broadcasted_iota(jnp.int32, (blk, blk), 0)
    col = lax.broadcasted_iota(jnp.int32, (blk, blk), 1)
    own_max = []
    for c in range(MOBA_Q_BLOCKS):
        start = pl.multiple_of((first_blk + c) * blk, blk)
        s = jnp.dot(ka_ref[pl.ds(start, blk), 0:dh], qat[0:dh, c * blk:(c + 1) * blk],
                    preferred_element_type=F32)
        s = jnp.where(row <= col, s, MASK_VALUE)
        so_ref[c] = s
        own_max.append(_fold_rows_max(s))
    mrun0 = jnp.concatenate(own_max, axis=1)

    def attend(tile):
        width = tq // MOBA_Q_PARTS
        own_per_part = MOBA_Q_BLOCKS // MOBA_Q_PARTS
        extents = [(tile * MOBA_Q_BLOCKS + (part + 1) * own_per_part - 1) * blk for part in range(MOBA_Q_PARTS)]
        pieces = [[(k0, min(k0 + span, ext)) for k0 in range(0, ext, span)] for ext in extents]
        maxes = []
        for part in range(MOBA_Q_PARTS):
            qs = slice(part * width, (part + 1) * width)
            mrun = mrun0[:, qs]
            for g, (k0, k1) in enumerate(pieces[part]):
                s = jnp.dot(ka_ref[k0:k1, :], qat[:, qs], preferred_element_type=F32)
                s_ref[g, 0:k1 - k0, qs] = s
                mrun = jnp.maximum(mrun, _fold_rows_max(s))
            maxes.append(jnp.max(mrun, axis=0, keepdims=True))
        outs = []
        for part in range(MOBA_Q_PARTS):
            qs = slice(part * width, (part + 1) * width)
            m = maxes[part]
            cols = []
            for c in range(own_per_part):
                p = jnp.exp2((so_ref[part * own_per_part + c] - m[:, c * blk:(c + 1) * blk]).astype(BF16))
                cols.append(jnp.dot(vat3_ref[first_blk + part * own_per_part + c], p,
                                    preferred_element_type=F32))
            acc = jnp.concatenate(cols, axis=1)
            for g, (k0, k1) in enumerate(pieces[part]):
                p = jnp.exp2((s_ref[g, 0:k1 - k0, qs] - m).astype(BF16))
                acc = acc + jnp.dot(vat_ref[:, k0:k1], p, preferred_element_type=F32)
            outs.append(acc[0:dh] / acc[dh:2 * dh])
        o_ref[...] = jnp.concatenate(outs, axis=1).T

    for tile in range(seq // tq):
        pl.when(j == tile)(functools.partial(attend, tile))


def moba(p3):
    bsz, seq, _ = p3.shape
    blk = MOBA_BLOCK
    grp = MOBA_GROUP
    tq = MOBA_Q_BLOCKS * blk
    assert seq % (blk * grp) == 0 and seq % tq == 0
    nb = seq // blk
    assert nb <= HEAD_DIM
    max_count = (nb - 1 + grp - 1) // grp
    kernel = functools.partial(_moba_kernel, nb=nb, scale=HEAD_DIM ** -0.5)
    col = lambda c: pl.BlockSpec((None, seq, HEAD_DIM), lambda b, h, j: (b, 0, c * N_HEADS + h))
    return pl.pallas_call(
        kernel,
        grid=(bsz, N_HEADS, seq // tq),
        in_specs=[col(0), col(1), col(2)],
        out_specs=pl.BlockSpec((None, tq, HEAD_DIM), lambda b, h, j: (b, j, h)),
        out_shape=jax.ShapeDtypeStruct((bsz, seq, D_GROUP), F32),
        scratch_shapes=[
            pltpu.VMEM((seq // tq, 2 * HEAD_DIM, tq), BF16),
            pltpu.VMEM((seq, 2 * HEAD_DIM), BF16),
            pltpu.VMEM((2 * HEAD_DIM, seq), BF16),
            pltpu.VMEM((nb, 2 * HEAD_DIM, blk), BF16),
            pltpu.VMEM((max_count, grp * blk, tq), F32),
            pltpu.VMEM((MOBA_Q_BLOCKS, blk, blk), F32),
        ],
        compiler_params=_params("parallel", "parallel", "arbitrary"),
        name="moba",
    )(p3, p3, p3)


def _rglru_kernel(gate_ref, x_ref, cw_ref, cb_ref, wa_ref, ba_ref, wx_ref, bx_ref, lam_ref,
                  o_ref, halo_ref, h_ref):
    t = pl.program_id(1)
    rows = x_ref.shape[0]

    @pl.when(t == 0)
    def _():
        halo_ref[...] = jnp.zeros(halo_ref.shape, F32)
        h_ref[...] = jnp.zeros(h_ref.shape, F32)

    x = x_ref[...]
    ext = jnp.concatenate([halo_ref[...], x], axis=0)
    halo_ref[...] = x[rows - SUBLANES:]
    xc = jnp.zeros_like(x) + cb_ref[...]
    for j in range(RG_CONV):
        xc = xc + cw_ref[j:j + 1, :] * _shift_rows(ext, RG_CONV - 1 - j, SUBLANES, rows)

    xcb = xc.astype(BF16)
    r_parts, i_parts = [], []
    for g in range(D_GROUP // HEAD_DIM):
        blk = xcb[:, g * HEAD_DIM:(g + 1) * HEAD_DIM]
        r_parts.append(jnp.dot(blk, wa_ref[g].astype(BF16), preferred_element_type=F32))
        i_parts.append(jnp.dot(blk, wx_ref[g].astype(BF16), preferred_element_type=F32))
    r = jax.nn.sigmoid(jnp.concatenate(r_parts, axis=-1) + ba_ref[...])
    ig = jax.nn.sigmoid(jnp.concatenate(i_parts, axis=-1) + bx_ref[...])
    neg_lam = -lam_ref[...]
    softplus = jnp.maximum(neg_lam, 0.0) + jnp.log1p(jnp.exp(-jnp.abs(neg_lam)))
    log_a = -RG_C * r * softplus
    a = jnp.exp(log_a)
    u = jnp.sqrt(jnp.maximum(1.0 - jnp.exp(2.0 * log_a), 0.0)) * (ig * xc)

    in_group = lax.broadcasted_iota(jnp.int32, a.shape, 0) & (SUBLANES - 1)
    big_a, big_b = a, u
    k = 1
    while k < SUBLANES:
        a_sh = jnp.where(in_group >= k, pltpu.roll(big_a, k, axis=0), 1.0)
        b_sh = jnp.where(in_group >= k, pltpu.roll(big_b, k, axis=0), 0.0)
        big_b = big_a * b_sh + big_b
        big_a = big_a * a_sh
        k *= 2
    carry = h_ref[0:1, :]
    groups = []
    for r0 in range(0, rows, SUBLANES):
        h_grp = big_a[r0:r0 + SUBLANES] * carry + big_b[r0:r0 + SUBLANES]
        groups.append(h_grp)
        carry = h_grp[SUBLANES - 1:SUBLANES]
    h = jnp.concatenate(groups, axis=0)
    h_ref[0:1, :] = carry
    o_ref[...] = h * jax.nn.gelu(gate_ref[...], approximate=True)


def rglru(p3, conv_w, conv_b, w_a, b_a, w_x, b_x, lam, l, tt):
    bsz, seq, _ = p3.shape
    tt = min(tt, seq)
    assert seq % tt == 0
    c = D_GROUP
    vec = lambda: pl.BlockSpec((None, 1, c), lambda b, t: (l, 0, 0))
    return pl.pallas_call(
        _rglru_kernel,
        grid=(bsz, seq // tt),
        in_specs=[
            pl.BlockSpec((None, tt, c), lambda b, t: (b, t, 3)),
            pl.BlockSpec((None, tt, c), lambda b, t: (b, t, 4)),
            pl.BlockSpec((None, RG_CONV, c), lambda b, t: (l, 0, 0)),
            vec(),
            pl.BlockSpec((None, c // HEAD_DIM, HEAD_DIM, HEAD_DIM), lambda b, t: (l, 0, 0, 0)),
            vec(),
            pl.BlockSpec((None, c // HEAD_DIM, HEAD_DIM, HEAD_DIM), lambda b, t: (l, 0, 0, 0)),
            vec(),
            vec(),
        ],
        out_specs=pl.BlockSpec((None, tt, c), lambda b, t: (b, t, 0)),
        out_shape=jax.ShapeDtypeStruct((bsz, seq, c), F32),
        scratch_shapes=[pltpu.VMEM((SUBLANES, c), F32), pltpu.VMEM((SUBLANES, c), F32)],
        compiler_params=_params("parallel", "arbitrary"),
        name="rglru",
    )(p3, p3, conv_w, conv_b, w_a, b_a, w_x, b_x, lam)


def _conformer_units(val_ref, gate_ref, cw_ref, cb_ref, ng_ref, nb_ref, o_ref, halo_ref):
    rows = val_ref.shape[0]

    def unit(g):
        cs = slice(g * HEAD_DIM, (g + 1) * HEAD_DIM)
        glu = val_ref[:, cs] * jax.nn.sigmoid(gate_ref[:, cs])
        ext = jnp.concatenate([halo_ref[:, cs], glu], axis=0)
        halo_ref[:, cs] = glu[rows - CV_HALO:]
        rolled = [ext] + [pltpu.roll(ext, r, axis=0) for r in range(1, SUBLANES)]
        u = jnp.zeros_like(glu) + cb_ref[:, cs]
        for j in range(CV_WIDTH):
            d = CV_WIDTH - 1 - j
            r = d % SUBLANES
            start = CV_HALO - (d - r)
            u = u + cw_ref[j:j + 1, cs] * rolled[r][start:start + rows]
        o_ref[:, cs] = _silu(_standardize(u) * ng_ref[:, cs] + nb_ref[:, cs]).astype(o_ref.dtype)

    return [functools.partial(unit, g) for g in range(D_GROUP // HEAD_DIM)]


def _conformer_kernel(val_ref, gate_ref, cw_ref, cb_ref, ng_ref, nb_ref, o_ref, halo_ref):
    @pl.when(pl.program_id(1) == 0)
    def _():
        halo_ref[...] = jnp.zeros(halo_ref.shape, F32)

    for unit in _conformer_units(val_ref, gate_ref, cw_ref, cb_ref, ng_ref, nb_ref, o_ref, halo_ref):
        unit()


def conformer(p3, cv_w, cv_b, ng, nb, l, tt):
    bsz, seq, _ = p3.shape
    tt = min(tt, seq)
    assert seq % tt == 0 and tt >= CV_HALO
    c = D_GROUP
    vec = lambda: pl.BlockSpec((None, 1, c), lambda b, t: (l, 0, 0))
    return pl.pallas_call(
        _conformer_kernel,
        grid=(bsz, seq // tt),
        in_specs=[
            pl.BlockSpec((None, tt, c), lambda b, t: (b, t, 5)),
            pl.BlockSpec((None, tt, c), lambda b, t: (b, t, 6)),
            pl.BlockSpec((None, CV_WIDTH, c), lambda b, t: (l, 0, 0)),
            vec(), vec(), vec(),
        ],
        out_specs=pl.BlockSpec((None, tt, c), lambda b, t: (b, t, 0)),
        out_shape=jax.ShapeDtypeStruct((bsz, seq, c), F32),
        scratch_shapes=[pltpu.VMEM((CV_HALO, c), F32)],
        compiler_params=_params("parallel", "arbitrary"),
        name="conformer",
    )(p3, p3, cv_w, cv_b, ng, nb)


def _hgrn2_chunk(q, kk, v, lf2, state_t):
    c = HG_CHUNK
    sub = HG_SUB
    row = lax.broadcasted_iota(jnp.int32, (c, HEAD_DIM), 0)
    b = lf2
    k = 1
    while k < c:
        b = b + jnp.where(row >= k, pltpu.roll(b, k, axis=0), 0.0)
        k *= 2
    vb = v.astype(BF16)

    o = lax.dot_general((q * jnp.exp2(b)).astype(BF16), state_t.astype(BF16), NT_DIMS,
                        preferred_element_type=F32)

    lane_s = lax.broadcasted_iota(jnp.int32, (sub, c), 1)
    row_s = lax.broadcasted_iota(jnp.int32, (sub, c), 0)
    att_rows = []
    for blk in range(c // sub):
        lo = blk * sub
        q_i = q[lo:lo + sub]
        b_i = b[lo:lo + sub]
        k_i = kk[lo:lo + sub]
        att = jnp.zeros((sub, c), F32)
        if blk > 0:
            ref = b[lo - 1:lo]
            qs = (q_i * jnp.exp2(b_i - ref)).astype(BF16)
            ks = kk[:lo] * jnp.exp2(ref - b[:lo])
            ks = jnp.concatenate([ks, jnp.zeros((c - lo, HEAD_DIM), F32)], axis=0).astype(BF16)
            att = lax.dot_general(qs, ks, NT_DIMS, preferred_element_type=F32)
        for s in range(sub):
            decay = jnp.exp2(jnp.minimum(b_i - b_i[s:s + 1], 0.0))
            w = jnp.sum(q_i * k_i[s:s + 1] * decay, axis=-1, keepdims=True)
            att = att + jnp.where((lane_s == lo + s) & (row_s >= s), w, 0.0)
        att_rows.append(att)
    att = jnp.concatenate(att_rows, axis=0)
    o = o + jnp.dot(att.astype(BF16), vb, preferred_element_type=F32)

    b_last = b[c - 1:c]
    ks = (kk * jnp.exp2(b_last - b)).astype(BF16)
    new_state_t = state_t * jnp.exp2(b_last) + lax.dot_general(vb, ks, TN_DIMS, preferred_element_type=F32)
    return o, new_state_t


def _hgrn2_units(q_ref, f_ref, v_ref, g_ref, lbp_ref, ng_ref, o_ref, state_ref, layer):
    rows = q_ref.shape[0]
    n_chunks = rows // HG_CHUNK

    lbp = lbp_ref[...]
    e = jnp.exp(lbp - jnp.max(lbp, axis=0, keepdims=True))
    sm = e / jnp.sum(e, axis=0, keepdims=True)
    lb = jnp.sum(sm[0:layer + 1], axis=0, keepdims=True) - sm[0:1]
    states = [state_ref[h] for h in range(N_HEADS)]

    def unit(ci, h):
        cs = slice(h * HEAD_DIM, (h + 1) * HEAD_DIM)
        rs = slice(ci * HG_CHUNK, (ci + 1) * HG_CHUNK)
        lb_h = lb[:, cs]
        sig = jax.nn.sigmoid(f_ref[rs, cs])
        lf2 = jnp.log(lb_h + (1.0 - lb_h) * sig) * LOG2_E
        kk = (1.0 - lb_h) * (1.0 - sig)
        o, states[h] = _hgrn2_chunk(q_ref[rs, cs], kk, v_ref[rs, cs], lf2, states[h])
        o = o * lax.rsqrt(jnp.mean(o * o, axis=-1, keepdims=True) + LN_EPS)
        o_ref[rs, cs] = (o * ng_ref[:, cs] * _silu(g_ref[rs, cs])).astype(o_ref.dtype)
        if ci == n_chunks - 1:
            state_ref[h] = states[h]

    return [functools.partial(unit, ci, h) for ci in range(n_chunks) for h in range(N_HEADS)]


def _hgrn2_kernel(q_ref, f_ref, v_ref, g_ref, lbp_ref, ng_ref, o_ref, state_ref, *, layer):
    @pl.when(pl.program_id(1) == 0)
    def _():
        state_ref[...] = jnp.zeros(state_ref.shape, F32)

    for unit in _hgrn2_units(q_ref, f_ref, v_ref, g_ref, lbp_ref, ng_ref, o_ref, state_ref, layer):
        unit()


def hgrn2(p3, lower_bound_params, norm_g, l, tt):
    bsz, seq, _ = p3.shape
    tt = min(tt, seq)
    assert seq % tt == 0 and tt % HG_CHUNK == 0
    c = D_GROUP
    depth = lower_bound_params.shape[0]
    tok = lambda col: pl.BlockSpec((None, tt, c), lambda b, t: (b, t, col))
    return pl.pallas_call(
        functools.partial(_hgrn2_kernel, layer=l),
        grid=(bsz, seq // tt),
        in_specs=[
            tok(7), tok(8), tok(9), tok(10),
            pl.BlockSpec((depth, c), lambda b, t: (0, 0)),
            pl.BlockSpec((None, 1, c), lambda b, t: (l, 0, 0)),
        ],
        out_specs=pl.BlockSpec((None, tt, c), lambda b, t: (b, t, 0)),
        out_shape=jax.ShapeDtypeStruct((bsz, seq, c), F32),
        scratch_shapes=[pltpu.VMEM((N_HEADS, HEAD_DIM, HEAD_DIM), F32)],
        compiler_params=_params("parallel", "arbitrary"),
        name="hgrn2",
    )(p3, p3, p3, p3, lower_bound_params, norm_g)


def _out_proj_ln_units(x_ref, y_refs, w_ref, g_ref, b_ref, o_ref, alpha):
    c = D_GROUP
    rows = x_ref.shape[0]
    step = min(rows, OUT_PROJ_ROWS)
    units = []
    for r0 in range(0, rows, step):
        rs = slice(r0, r0 + step)
        acc = []

        def matmul(y_ref, k, rs=rs, acc=acc):
            part = jnp.dot(y_ref[rs, :].astype(BF16), w_ref[k * c:(k + 1) * c, :], preferred_element_type=F32)
            acc[:] = [part if not acc else acc[0] + part]

        def norm(rs=rs, acc=acc):
            z = alpha * x_ref[rs, :] + acc[0]
            o_ref[rs, :] = _standardize(z) * g_ref[...] + b_ref[...]

        units += [functools.partial(matmul, y_ref, k) for k, y_ref in enumerate(y_refs)]
        units.append(norm)
    return units


def _out_proj_ln_kernel(x_ref, ya_ref, yb_ref, yc_ref, yd_ref, w_ref, g_ref, b_ref, o_ref, *, alpha):
    for unit in _out_proj_ln_units(x_ref, (ya_ref, yb_ref, yc_ref, yd_ref), w_ref, g_ref, b_ref, o_ref, alpha):
        unit()


def out_proj_ln(x, ya, yb, yc, yd, w_out, ln_g, ln_b, l, jn, alpha, tm):
    m, d = x.shape
    c = D_GROUP
    tm = min(tm, m)
    assert m % tm == 0
    mix = lambda: pl.BlockSpec((tm, c), lambda i: (i, 0))
    return pl.pallas_call(
        functools.partial(_out_proj_ln_kernel, alpha=alpha),
        grid=(m // tm,),
        in_specs=[
            pl.BlockSpec((tm, d), lambda i: (i, 0)),
            mix(), mix(), mix(), mix(),
            pl.BlockSpec((None, 4 * c, d), lambda i: (l, 0, 0)),
            pl.BlockSpec((None, 1, d), lambda i: (jn, 0, 0)),
            pl.BlockSpec((None, 1, d), lambda i: (jn, 0, 0)),
        ],
        out_specs=pl.BlockSpec((tm, d), lambda i: (i, 0)),
        out_shape=jax.ShapeDtypeStruct((m, d), F32),
        compiler_params=_params("parallel"),
        name="out_proj_ln",
    )(x, ya, yb, yc, yd, w_out, ln_g, ln_b)


def kernel(x, ln_g, ln_b, ffn_w_gate, ffn_w_up, ffn_w_down, w_in, w_out, rg_conv_w, rg_conv_b,
           rg_w_a, rg_b_a, rg_w_x, rg_b_x, rg_lambda, cv_w, cv_b, cv_ln_g, cv_ln_b,
           hg_lower_bounds, hg_norm_g):
    bsz, seq, d = x.shape
    depth = w_in.shape[0]
    alpha = (2 * depth) ** 0.25
    m = bsz * seq

    wgu = interleave_cols(ffn_w_gate, ffn_w_up, FFN_COLS).astype(BF16)
    wd = ffn_w_down.astype(BF16)
    w_in_b = w_in.astype(BF16)
    w_out_b = w_out.astype(BF16)
    ln_g3 = ln_g.reshape(depth * 3, 1, d)
    ln_b3 = ln_b.reshape(depth * 3, 1, d)
    row = lambda a: a.reshape(depth, 1, a.shape[-1])

    h = x.reshape(m, d)
    for l in range(depth):
        h = ffn_ln(h, wgu, wd, ln_g3, ln_b3, l, 0, 3 * l, alpha, tm=1024, tf=FFN_COLS)
        p3 = in_proj(h, w_in_b, l, tm=1024, tn=IN_PROJ_COLS).reshape(bsz, seq, -1)
        y_a = moba(p3)
        y_b = rglru(p3, rg_conv_w, row(rg_conv_b), rg_w_a, row(rg_b_a), rg_w_x, row(rg_b_x),
                    row(rg_lambda), l, tt=1024)
        y_c = conformer(p3, cv_w, row(cv_b), row(cv_ln_g), row(cv_ln_b), l, tt=1024)
        y_d = hgrn2(p3, hg_lower_bounds, row(hg_norm_g), l, tt=512)
        flat = lambda y: y.reshape(m, D_GROUP)
        h = out_proj_ln(h, flat(y_a), flat(y_b), flat(y_c), flat(y_d), w_out_b, ln_g3, ln_b3,
                        l, 3 * l + 1, alpha, tm=512)
        h = ffn_ln(h, wgu, wd, ln_g3, ln_b3, l, 1, 3 * l + 2, alpha, tm=1024, tf=FFN_COLS)
    return h.reshape(bsz, seq, d)
```

```python
import functools

import jax
import jax.numpy as jnp
from jax import lax
from jax.experimental import pallas as pl
from jax.experimental.pallas import tpu as pltpu

F32 = jnp.float32
BF16 = jnp.bfloat16
MIXER_OUT_DTYPE = BF16

LANES = 128
SUBLANES = 8
VMEM_LIMIT_BYTES = 56 * 1024 * 1024

D_GROUP = 512
HEAD_DIM = 128
N_HEADS = D_GROUP // HEAD_DIM
MOBA_BLOCK = 256
MOBA_TOPK = 3
MOBA_GROUP = 4
MOBA_Q_BLOCKS = 4
MOBA_Q_PARTS = 2
LOG2_E = 1.4426950408889634
MASK_VALUE = -1e30
RG_C = 8.0
RG_CONV = 4
CV_WIDTH = 31
CV_HALO = 32
FFN_COLS = 512
IN_PROJ_COLS = 1408
OUT_PROJ_ROWS = 256
FFN_DOWN_COLS = 512
FFN_NORM_ROWS = 256
HG_CHUNK = 64
HG_SUB = 8
LN_EPS = 1e-5

NT_DIMS = (((1,), (1,)), ((), ()))
TN_DIMS = (((0,), (0,)), ((), ()))


def _params(*sem):
    return pltpu.CompilerParams(dimension_semantics=sem, vmem_limit_bytes=VMEM_LIMIT_BYTES)


def _standardize(z):
    mu = jnp.mean(z, axis=-1, keepdims=True)
    zc = z - mu
    var = jnp.mean(zc * zc, axis=-1, keepdims=True)
    return zc * lax.rsqrt(var + LN_EPS)


def _silu(z):
    return z * jax.nn.sigmoid(z)


def _shift_rows(ext, d, halo, rows):
    if d % SUBLANES == 0:
        return ext[halo - d:halo - d + rows]
    r = d % SUBLANES
    base = d - r
    rolled = pltpu.roll(ext, r, axis=0)
    return rolled[halo - base:halo - base + rows]


def _ffn_ln_kernel(x_ref, wg_ref, wu_ref, wd_ref, g_ref, b_ref, o_ref, xb_ref, *, alpha):
    f = pl.program_id(1)
    last = pl.num_programs(1) - 1
    rows, d = o_ref.shape

    def hidden(xb):
        hg = jnp.dot(xb, wg_ref[...], preferred_element_type=F32)
        hu = jnp.dot(xb, wu_ref[...], preferred_element_type=F32)
        return (_silu(hg) * hu).astype(BF16)

    def down(h, accumulate):
        for c0 in range(0, d, FFN_DOWN_COLS):
            cs = slice(c0, min(c0 + FFN_DOWN_COLS, d))
            part = jnp.dot(h, wd_ref[:, cs], preferred_element_type=F32)
            o_ref[:, cs] = o_ref[:, cs] + part if accumulate else part

    @pl.when(f == 0)
    def _():
        xb = x_ref[...].astype(BF16)
        xb_ref[...] = xb
        down(hidden(xb), accumulate=False)

    @pl.when((f > 0) & (f < last))
    def _():
        down(hidden(xb_ref[...]), accumulate=True)

    @pl.when(f == last)
    def _():
        h = hidden(xb_ref[...])
        for r0 in range(0, rows, FFN_NORM_ROWS):
            rs = slice(r0, min(r0 + FFN_NORM_ROWS, rows))
            acc = o_ref[rs, :] + jnp.dot(h[rs, :], wd_ref[...], preferred_element_type=F32)
            z = alpha * x_ref[rs, :] + 0.5 * acc
            o_ref[rs, :] = _standardize(z) * g_ref[...] + b_ref[...]


def ffn_ln(x, wg, wu, wd, ln_g, ln_b, l, j, jn, alpha, tm, tf):
    m, d = x.shape
    dff = wg.shape[-1]
    tm = min(tm, m)
    tf = min(tf, dff)
    assert m % tm == 0 and dff % tf == 0
    grid = (m // tm, dff // tf)
    return pl.pallas_call(
        functools.partial(_ffn_ln_kernel, alpha=alpha),
        grid=grid,
        in_specs=[
            pl.BlockSpec((tm, d), lambda i, f: (i, 0)),
            pl.BlockSpec((None, None, d, tf), lambda i, f: (l, j, 0, f)),
            pl.BlockSpec((None, None, d, tf), lambda i, f: (l, j, 0, f)),
            pl.BlockSpec((None, None, tf, d), lambda i, f: (l, j, f, 0)),
            pl.BlockSpec((None, 1, d), lambda i, f: (jn, 0, 0)),
            pl.BlockSpec((None, 1, d), lambda i, f: (jn, 0, 0)),
        ],
        out_specs=pl.BlockSpec((tm, d), lambda i, f: (i, 0)),
        out_shape=jax.ShapeDtypeStruct((m, d), F32),
        scratch_shapes=[pltpu.VMEM((tm, d), BF16)],
        compiler_params=_params("parallel", "arbitrary"),
        name="ffn_ln",
    )(x, wg, wu, wd, ln_g, ln_b)


def _in_proj_kernel(x_ref, w_ref, o_ref, xb_ref):
    c = pl.program_id(1)

    @pl.when(c == 0)
    def _():
        xb = x_ref[...].astype(BF16)
        xb_ref[...] = xb
        o_ref[...] = jnp.dot(xb, w_ref[...], preferred_element_type=F32)

    @pl.when(c > 0)
    def _():
        o_ref[...] = jnp.dot(xb_ref[...], w_ref[...], preferred_element_type=F32)


def in_proj(x, w_in, l, tm, tn):
    m, d = x.shape
    n = w_in.shape[-1]
    tm = min(tm, m)
    assert m % tm == 0 and n % tn == 0
    return pl.pallas_call(
        _in_proj_kernel,
        grid=(m // tm, n // tn),
        in_specs=[
            pl.BlockSpec((tm, d), lambda i, c: (i, 0)),
            pl.BlockSpec((None, d, tn), lambda i, c: (l, 0, c)),
        ],
        out_specs=pl.BlockSpec((tm, tn), lambda i, c: (i, c)),
        out_shape=jax.ShapeDtypeStruct((m, n), F32),
        scratch_shapes=[pltpu.VMEM((tm, d), BF16)],
        compiler_params=_params("parallel", "arbitrary"),
        name="in_proj",
    )(x, w_in)


def _fold_rows_max(a):
    rows, cols = a.shape
    return jnp.max(a.reshape(rows // SUBLANES, SUBLANES, cols), axis=0)


def _moba_kernel(q_ref, k_ref, v_ref, o_ref, qat_ref, ka_ref, vat_ref, vat3_ref, s_ref, so_ref, *, nb, scale):
    j = pl.program_id(2)
    blk = MOBA_BLOCK
    grp = MOBA_GROUP
    dh = HEAD_DIM
    seq = nb * blk
    tq = MOBA_Q_BLOCKS * blk
    span = grp * blk
    nbp = -(-nb // SUBLANES) * SUBLANES

    @pl.when(j == 0)
    def _():
        k = k_ref[...]
        lane = lax.broadcasted_iota(jnp.int32, (seq, LANES), 1)
        tile = lax.broadcasted_iota(jnp.int32, (nb, blk, LANES), 0).reshape(seq, LANES)
        ka_ref[:, 0:dh] = k.astype(BF16)
        ka_ref[:, dh:2 * dh] = jnp.where(tile == lane, 1.0, 0.0).astype(BF16)
        vat = jnp.concatenate([v_ref[...].T, jnp.ones((dh, seq), F32)], axis=0).astype(BF16)
        vat_ref[...] = vat
        for n in range(nb):
            vat3_ref[n] = vat[:, n * blk:(n + 1) * blk]
        k_mean = jnp.mean(k.reshape(nb, blk, dh), axis=1)
        k_mean = jnp.concatenate([k_mean, jnp.zeros((LANES - nb, dh), F32)], axis=0)

        q = q_ref[...]
        gate = lax.dot_general(k_mean.astype(BF16), q.astype(BF16), NT_DIMS, preferred_element_type=F32)
        g = gate[0:nbp]
        blk_row = lax.broadcasted_iota(jnp.int32, (nbp, seq), 0)
        q_blk = lax.shift_right_logical(lax.broadcasted_iota(jnp.int32, (nbp, seq), 1), blk.bit_length() - 1)
        row_f = blk_row.astype(F32)
        past = blk_row < q_blk
        g = jnp.where(past, g, MASK_VALUE)
        sel = jnp.zeros((nbp, seq), F32)
        for _ in range(MOBA_TOPK):
            top = jnp.max(g, axis=0, keepdims=True)
            first = jnp.min(jnp.where(g == top, row_f, float(LANES)), axis=0, keepdims=True)
            pick = row_f == first
            sel = jnp.where(pick, 1.0, sel)
            g = jnp.where(pick, -jnp.inf, g)
        bias = jnp.where(past & (sel > 0.5), 0.0, MASK_VALUE)
        qt = (q * (scale * LOG2_E)).T
        qat = jnp.concatenate([qt, bias, jnp.zeros((dh - nbp, seq), F32)], axis=0).astype(BF16)
        for t in range(seq // tq):
            qat_ref[t] = qat[:, t * tq:(t + 1) * tq]

    qat = qat_ref[j]
    first_blk = j * MOBA_Q_BLOCKS

    row = lax.broadcasted_iota(jnp.int32, (blk, blk), 0)
    col = lax.broadcasted_iota(jnp.int32, (blk, blk), 1)
    own_max = []
    for c in range(MOBA_Q_BLOCKS):
        start = pl.multiple_of((first_blk + c) * blk, blk)
        s = jnp.dot(ka_ref[pl.ds(start, blk), 0:dh], qat[0:dh, c * blk:(c + 1) * blk],
                    preferred_element_type=F32)
        s = jnp.where(row <= col, s, MASK_VALUE)
        so_ref[c] = s
        own_max.append(_fold_rows_max(s))
    mrun0 = jnp.concatenate(own_max, axis=1)

    def attend(tile):
        width = tq // MOBA_Q_PARTS
        own_per_part = MOBA_Q_BLOCKS // MOBA_Q_PARTS
        extents = [(tile * MOBA_Q_BLOCKS + (part + 1) * own_per_part - 1) * blk for part in range(MOBA_Q_PARTS)]
        pieces = [[(k0, min(k0 + span, ext)) for k0 in range(0, ext, span)] for ext in extents]
        maxes = []
        for part in range(MOBA_Q_PARTS):
            qs = slice(part * width, (part + 1) * width)
            mrun = mrun0[:, qs]
            for g, (k0, k1) in enumerate(pieces[part]):
                s = jnp.dot(ka_ref[k0:k1, :], qat[:, qs], preferred_element_type=F32)
                s_ref[g, 0:k1 - k0, qs] = s
                mrun = jnp.maximum(mrun, _fold_rows_max(s))
            maxes.append(jnp.max(mrun, axis=0, keepdims=True))
        outs = []
        for part in range(MOBA_Q_PARTS):
            qs = slice(part * width, (part + 1) * width)
            m = maxes[part]
            cols = []
            for c in range(own_per_part):
                p = jnp.exp2((so_ref[part * own_per_part + c] - m[:, c * blk:(c + 1) * blk]).astype(BF16))
                cols.append(jnp.dot(vat3_ref[first_blk + part * own_per_part + c], p,
                                    preferred_element_type=F32))
            acc = jnp.concatenate(cols, axis=1)
            for g, (k0, k1) in enumerate(pieces[part]):
                p = jnp.exp2((s_ref[g, 0:k1 - k0, qs] - m).astype(BF16))
                acc = acc + jnp.dot(vat_ref[:, k0:k1], p, preferred_element_type=F32)
            outs.append(acc[0:dh] / acc[dh:2 * dh])
        o_ref[...] = jnp.concatenate(outs, axis=1).T.astype(o_ref.dtype)

    for tile in range(seq // tq):
        pl.when(j == tile)(functools.partial(attend, tile))


def moba(p3):
    bsz, seq, _ = p3.shape
    blk = MOBA_BLOCK
    grp = MOBA_GROUP
    tq = MOBA_Q_BLOCKS * blk
    assert seq % (blk * grp) == 0 and seq % tq == 0
    nb = seq // blk
    assert nb <= HEAD_DIM
    max_count = (nb - 1 + grp - 1) // grp
    kernel = functools.partial(_moba_kernel, nb=nb, scale=HEAD_DIM ** -0.5)
    col = lambda c: pl.BlockSpec((None, seq, HEAD_DIM), lambda b, h, j: (b, 0, c * N_HEADS + h))
    return pl.pallas_call(
        kernel,
        grid=(bsz, N_HEADS, seq // tq),
        in_specs=[col(0), col(1), col(2)],
        out_specs=pl.BlockSpec((None, tq, HEAD_DIM), lambda b, h, j: (b, j, h)),
        out_shape=jax.ShapeDtypeStruct((bsz, seq, D_GROUP), MIXER_OUT_DTYPE),
        scratch_shapes=[
            pltpu.VMEM((seq // tq, 2 * HEAD_DIM, tq), BF16),
            pltpu.VMEM((seq, 2 * HEAD_DIM), BF16),
            pltpu.VMEM((2 * HEAD_DIM, seq), BF16),
            pltpu.VMEM((nb, 2 * HEAD_DIM, blk), BF16),
            pltpu.VMEM((max_count, grp * blk, tq), F32),
            pltpu.VMEM((MOBA_Q_BLOCKS, blk, blk), F32),
        ],
        compiler_params=_params("parallel", "parallel", "arbitrary"),
        name="moba",
    )(p3, p3, p3)


def _rglru_kernel(gate_ref, x_ref, cw_ref, cb_ref, wa_ref, ba_ref, wx_ref, bx_ref, lam_ref,
                  o_ref, halo_ref, h_ref):
    t = pl.program_id(1)
    rows = x_ref.shape[0]

    @pl.when(t == 0)
    def _():
        halo_ref[...] = jnp.zeros(halo_ref.shape, F32)
        h_ref[...] = jnp.zeros(h_ref.shape, F32)

    x = x_ref[...]
    ext = jnp.concatenate([halo_ref[...], x], axis=0)
    halo_ref[...] = x[rows - SUBLANES:]
    xc = jnp.zeros_like(x) + cb_ref[...]
    for j in range(RG_CONV):
        xc = xc + cw_ref[j:j + 1, :] * _shift_rows(ext, RG_CONV - 1 - j, SUBLANES, rows)

    xcb = xc.astype(BF16)
    r_parts, i_parts = [], []
    for g in range(D_GROUP // HEAD_DIM):
        blk = xcb[:, g * HEAD_DIM:(g + 1) * HEAD_DIM]
        r_parts.append(jnp.dot(blk, wa_ref[g].astype(BF16), preferred_element_type=F32))
        i_parts.append(jnp.dot(blk, wx_ref[g].astype(BF16), preferred_element_type=F32))
    r = jax.nn.sigmoid(jnp.concatenate(r_parts, axis=-1) + ba_ref[...])
    ig = jax.nn.sigmoid(jnp.concatenate(i_parts, axis=-1) + bx_ref[...])
    neg_lam = -lam_ref[...]
    softplus = jnp.maximum(neg_lam, 0.0) + jnp.log1p(jnp.exp(-jnp.abs(neg_lam)))
    log_a = -RG_C * r * softplus
    a = jnp.exp(log_a)
    u = jnp.sqrt(jnp.maximum(1.0 - jnp.exp(2.0 * log_a), 0.0)) * (ig * xc)

    in_group = lax.broadcasted_iota(jnp.int32, a.shape, 0) & (SUBLANES - 1)
    big_a, big_b = a, u
    k = 1
    while k < SUBLANES:
        a_sh = jnp.where(in_group >= k, pltpu.roll(big_a, k, axis=0), 1.0)
        b_sh = jnp.where(in_group >= k, pltpu.roll(big_b, k, axis=0), 0.0)
        big_b = big_a * b_sh + big_b
        big_a = big_a * a_sh
        k *= 2
    carry = h_ref[0:1, :]
    groups = []
    for r0 in range(0, rows, SUBLANES):
        h_grp = big_a[r0:r0 + SUBLANES] * carry + big_b[r0:r0 + SUBLANES]
        groups.append(h_grp)
        carry = h_grp[SUBLANES - 1:SUBLANES]
    h = jnp.concatenate(groups, axis=0)
    h_ref[0:1, :] = carry
    o_ref[...] = (h * jax.nn.gelu(gate_ref[...], approximate=True)).astype(o_ref.dtype)


def rglru(p3, conv_w, conv_b, w_a, b_a, w_x, b_x, lam, l, tt):
    bsz, seq, _ = p3.shape
    tt = min(tt, seq)
    assert seq % tt == 0
    c = D_GROUP
    vec = lambda: pl.BlockSpec((None, 1, c), lambda b, t: (l, 0, 0))
    return pl.pallas_call(
        _rglru_kernel,
        grid=(bsz, seq // tt),
        in_specs=[
            pl.BlockSpec((None, tt, c), lambda b, t: (b, t, 3)),
            pl.BlockSpec((None, tt, c), lambda b, t: (b, t, 4)),
            pl.BlockSpec((None, RG_CONV, c), lambda b, t: (l, 0, 0)),
            vec(),
            pl.BlockSpec((None, c // HEAD_DIM, HEAD_DIM, HEAD_DIM), lambda b, t: (l, 0, 0, 0)),
            vec(),
            pl.BlockSpec((None, c // HEAD_DIM, HEAD_DIM, HEAD_DIM), lambda b, t: (l, 0, 0, 0)),
            vec(),
            vec(),
        ],
        out_specs=pl.BlockSpec((None, tt, c), lambda b, t: (b, t, 0)),
        out_shape=jax.ShapeDtypeStruct((bsz, seq, c), MIXER_OUT_DTYPE),
        scratch_shapes=[pltpu.VMEM((SUBLANES, c), F32), pltpu.VMEM((SUBLANES, c), F32)],
        compiler_params=_params("parallel", "arbitrary"),
        name="rglru",
    )(p3, p3, conv_w, conv_b, w_a, b_a, w_x, b_x, lam)


def _conformer_units(val_ref, gate_ref, cw_ref, cb_ref, ng_ref, nb_ref, o_ref, halo_ref):
    rows = val_ref.shape[0]

    def unit(g):
        cs = slice(g * HEAD_DIM, (g + 1) * HEAD_DIM)
        glu = val_ref[:, cs] * jax.nn.sigmoid(gate_ref[:, cs])
        ext = jnp.concatenate([halo_ref[:, cs], glu], axis=0)
        halo_ref[:, cs] = glu[rows - CV_HALO:]
        rolled = [ext] + [pltpu.roll(ext, r, axis=0) for r in range(1, SUBLANES)]
        u = jnp.zeros_like(glu) + cb_ref[:, cs]
        for j in range(CV_WIDTH):
            d = CV_WIDTH - 1 - j
            r = d % SUBLANES
            start = CV_HALO - (d - r)
            u = u + cw_ref[j:j + 1, cs] * rolled[r][start:start + rows]
        o_ref[:, cs] = _silu(_standardize(u) * ng_ref[:, cs] + nb_ref[:, cs]).astype(o_ref.dtype)

    return [functools.partial(unit, g) for g in range(D_GROUP // HEAD_DIM)]


def _conformer_kernel(val_ref, gate_ref, cw_ref, cb_ref, ng_ref, nb_ref, o_ref, halo_ref):
    @pl.when(pl.program_id(1) == 0)
    def _():
        halo_ref[...] = jnp.zeros(halo_ref.shape, F32)

    for unit in _conformer_units(val_ref, gate_ref, cw_ref, cb_ref, ng_ref, nb_ref, o_ref, halo_ref):
        unit()


def conformer(p3, cv_w, cv_b, ng, nb, l, tt):
    bsz, seq, _ = p3.shape
    tt = min(tt, seq)
    assert seq % tt == 0 and tt >= CV_HALO
    c = D_GROUP
    vec = lambda: pl.BlockSpec((None, 1, c), lambda b, t: (l, 0, 0))
    return pl.pallas_call(
        _conformer_kernel,
        grid=(bsz, seq // tt),
        in_specs=[
            pl.BlockSpec((None, tt, c), lambda b, t: (b, t, 5)),
            pl.BlockSpec((None, tt, c), lambda b, t: (b, t, 6)),
            pl.BlockSpec((None, CV_WIDTH, c), lambda b, t: (l, 0, 0)),
            vec(), vec(), vec(),
        ],
        out_specs=pl.BlockSpec((None, tt, c), lambda b, t: (b, t, 0)),
        out_shape=jax.ShapeDtypeStruct((bsz, seq, c), MIXER_OUT_DTYPE),
        scratch_shapes=[pltpu.VMEM((CV_HALO, c), F32)],
        compiler_params=_params("parallel", "arbitrary"),
        name="conformer",
    )(p3, p3, cv_w, cv_b, ng, nb)


def _hgrn2_chunk(q, kk, v, lf2, state_t):
    c = HG_CHUNK
    sub = HG_SUB
    row = lax.broadcasted_iota(jnp.int32, (c, HEAD_DIM), 0)
    b = lf2
    k = 1
    while k < c:
        b = b + jnp.where(row >= k, pltpu.roll(b, k, axis=0), 0.0)
        k *= 2
    vb = v.astype(BF16)

    o = lax.dot_general((q * jnp.exp2(b)).astype(BF16), state_t.astype(BF16), NT_DIMS,
                        preferred_element_type=F32)

    lane_s = lax.broadcasted_iota(jnp.int32, (sub, c), 1)
    row_s = lax.broadcasted_iota(jnp.int32, (sub, c), 0)
    att_rows = []
    for blk in range(c // sub):
        lo = blk * sub
        q_i = q[lo:lo + sub]
        b_i = b[lo:lo + sub]
        k_i = kk[lo:lo + sub]
        att = jnp.zeros((sub, c), F32)
        if blk > 0:
            ref = b[lo - 1:lo]
            qs = (q_i * jnp.exp2(b_i - ref)).astype(BF16)
            ks = kk[:lo] * jnp.exp2(ref - b[:lo])
            ks = jnp.concatenate([ks, jnp.zeros((c - lo, HEAD_DIM), F32)], axis=0).astype(BF16)
            att = lax.dot_general(qs, ks, NT_DIMS, preferred_element_type=F32)
        for s in range(sub):
            decay = jnp.exp2(jnp.minimum(b_i - b_i[s:s + 1], 0.0))
            w = jnp.sum(q_i * k_i[s:s + 1] * decay, axis=-1, keepdims=True)
            att = att + jnp.where((lane_s == lo + s) & (row_s >= s), w, 0.0)
        att_rows.append(att)
    att = jnp.concatenate(att_rows, axis=0)
    o = o + jnp.dot(att.astype(BF16), vb, preferred_element_type=F32)

    b_last = b[c - 1:c]
    ks = (kk * jnp.exp2(b_last - b)).astype(BF16)
    new_state_t = state_t * jnp.exp2(b_last) + lax.dot_general(vb, ks, TN_DIMS, preferred_element_type=F32)
    return o, new_state_t


def _hgrn2_units(q_ref, f_ref, v_ref, g_ref, lbp_ref, ng_ref, o_ref, state_ref, layer):
    rows = q_ref.shape[0]
    n_chunks = rows // HG_CHUNK

    lbp = lbp_ref[...]
    e = jnp.exp(lbp - jnp.max(lbp, axis=0, keepdims=True))
    sm = e / jnp.sum(e, axis=0, keepdims=True)
    lb = jnp.sum(sm[0:layer + 1], axis=0, keepdims=True) - sm[0:1]
    states = [state_ref[h] for h in range(N_HEADS)]

    def unit(ci, h):
        cs = slice(h * HEAD_DIM, (h + 1) * HEAD_DIM)
        rs = slice(ci * HG_CHUNK, (ci + 1) * HG_CHUNK)
        lb_h = lb[:, cs]
        sig = jax.nn.sigmoid(f_ref[rs, cs])
        lf2 = jnp.log(lb_h + (1.0 - lb_h) * sig) * LOG2_E
        kk = (1.0 - lb_h) * (1.0 - sig)
        o, states[h] = _hgrn2_chunk(q_ref[rs, cs], kk, v_ref[rs, cs], lf2, states[h])
        o = o * lax.rsqrt(jnp.mean(o * o, axis=-1, keepdims=True) + LN_EPS)
        o_ref[rs, cs] = (o * ng_ref[:, cs] * _silu(g_ref[rs, cs])).astype(o_ref.dtype)
        if ci == n_chunks - 1:
            state_ref[h] = states[h]

    return [functools.partial(unit, ci, h) for ci in range(n_chunks) for h in range(N_HEADS)]


def _hgrn2_kernel(q_ref, f_ref, v_ref, g_ref, lbp_ref, ng_ref, o_ref, state_ref, *, layer):
    @pl.when(pl.program_id(1) == 0)
    def _():
        state_ref[...] = jnp.zeros(state_ref.shape, F32)

    for unit in _hgrn2_units(q_ref, f_ref, v_ref, g_ref, lbp_ref, ng_ref, o_ref, state_ref, layer):
        unit()


def hgrn2(p3, lower_bound_params, norm_g, l, tt):
    bsz, seq, _ = p3.shape
    tt = min(tt, seq)
    assert seq % tt == 0 and tt % HG_CHUNK == 0
    c = D_GROUP
    depth = lower_bound_params.shape[0]
    tok = lambda col: pl.BlockSpec((None, tt, c), lambda b, t: (b, t, col))
    return pl.pallas_call(
        functools.partial(_hgrn2_kernel, layer=l),
        grid=(bsz, seq // tt),
        in_specs=[
            tok(7), tok(8), tok(9), tok(10),
            pl.BlockSpec((depth, c), lambda b, t: (0, 0)),
            pl.BlockSpec((None, 1, c), lambda b, t: (l, 0, 0)),
        ],
        out_specs=pl.BlockSpec((None, tt, c), lambda b, t: (b, t, 0)),
        out_shape=jax.ShapeDtypeStruct((bsz, seq, c), MIXER_OUT_DTYPE),
        scratch_shapes=[pltpu.VMEM((N_HEADS, HEAD_DIM, HEAD_DIM), F32)],
        compiler_params=_params("parallel", "arbitrary"),
        name="hgrn2",
    )(p3, p3, p3, p3, lower_bound_params, norm_g)


def _out_proj_ln_kernel(x_ref, ya_ref, yb_ref, yc_ref, yd_ref, w_ref, g_ref, b_ref, o_ref, *, alpha):
    rows = x_ref.shape[0]
    step = min(rows, OUT_PROJ_ROWS)
    for r0 in range(0, rows, step):
        rs = slice(r0, r0 + step)
        y_cat = jnp.concatenate([y_ref[rs, :].astype(BF16) for y_ref in (ya_ref, yb_ref, yc_ref, yd_ref)], axis=1)
        z = alpha * x_ref[rs, :] + jnp.dot(y_cat, w_ref[...], preferred_element_type=F32)
        o_ref[rs, :] = _standardize(z) * g_ref[...] + b_ref[...]


def out_proj_ln(x, ya, yb, yc, yd, w_out, ln_g, ln_b, l, jn, alpha, tm):
    m, d = x.shape
    c = D_GROUP
    tm = min(tm, m)
    assert m % tm == 0
    mix = lambda: pl.BlockSpec((tm, c), lambda i: (i, 0))
    return pl.pallas_call(
        functools.partial(_out_proj_ln_kernel, alpha=alpha),
        grid=(m // tm,),
        in_specs=[
            pl.BlockSpec((tm, d), lambda i: (i, 0)),
            mix(), mix(), mix(), mix(),
            pl.BlockSpec((None, 4 * c, d), lambda i: (l, 0, 0), pipeline_mode=pl.Buffered(1)),
            pl.BlockSpec((None, 1, d), lambda i: (jn, 0, 0)),
            pl.BlockSpec((None, 1, d), lambda i: (jn, 0, 0)),
        ],
        out_specs=pl.BlockSpec((tm, d), lambda i: (i, 0)),
        out_shape=jax.ShapeDtypeStruct((m, d), F32),
        compiler_params=_params("parallel"),
        name="out_proj_ln",
    )(x, ya, yb, yc, yd, w_out, ln_g, ln_b)


def kernel(x, ln_g, ln_b, ffn_w_gate, ffn_w_up, ffn_w_down, w_in, w_out, rg_conv_w, rg_conv_b,
           rg_w_a, rg_b_a, rg_w_x, rg_b_x, rg_lambda, cv_w, cv_b, cv_ln_g, cv_ln_b,
           hg_lower_bounds, hg_norm_g):
    bsz, seq, d = x.shape
    depth = w_in.shape[0]
    alpha = (2 * depth) ** 0.25
    m = bsz * seq

    wg = ffn_w_gate.astype(BF16)
    wu = ffn_w_up.astype(BF16)
    wd = ffn_w_down.astype(BF16)
    w_in_b = w_in.astype(BF16)
    w_out_b = w_out.astype(BF16)
    ln_g3 = ln_g.reshape(depth * 3, 1, d)
    ln_b3 = ln_b.reshape(depth * 3, 1, d)
    row = lambda a: a.reshape(depth, 1, a.shape[-1])

    h = x.reshape(m, d)
    for l in range(depth):
        h = ffn_ln(h, wg, wu, wd, ln_g3, ln_b3, l, 0, 3 * l, alpha, tm=1024, tf=FFN_COLS)
        p3 = in_proj(h, w_in_b, l, tm=1024, tn=IN_PROJ_COLS).reshape(bsz, seq, -1)
        y_a = moba(p3)
        y_b = rglru(p3, rg_conv_w, row(rg_conv_b), rg_w_a, row(rg_b_a), rg_w_x, row(rg_b_x),
                    row(rg_lambda), l, tt=1024)
        y_c = conformer(p3, cv_w, row(cv_b), row(cv_ln_g), row(cv_ln_b), l, tt=1024)
        y_d = hgrn2(p3, hg_lower_bounds, row(hg_norm_g), l, tt=512)
        flat = lambda y: y.reshape(m, D_GROUP)
        h = out_proj_ln(h, flat(y_a), flat(y_b), flat(y_c), flat(y_d), w_out_b, ln_g3, ln_b3,
                        l, 3 * l + 1, alpha, tm=1024)
        h = ffn_ln(h, wg, wu, wd, ln_g3, ln_b3, l, 1, 3 * l + 2, alpha, tm=1024, tf=FFN_COLS)
    return h.reshape(bsz, seq, d)
```

```python
import functools

import jax
import jax.numpy as jnp
from jax import lax
from jax.experimental import pallas as pl
from jax.experimental.pallas import tpu as pltpu

F32 = jnp.float32
BF16 = jnp.bfloat16
MIXER_OUT_DTYPE = BF16

LANES = 128
SUBLANES = 8
VMEM_LIMIT_BYTES = 56 * 1024 * 1024

D_GROUP = 512
HEAD_DIM = 128
N_HEADS = D_GROUP // HEAD_DIM
MOBA_BLOCK = 256
MOBA_TOPK = 3
MOBA_GROUP = 4
MOBA_Q_BLOCKS = 4
MOBA_Q_PARTS = 2
LOG2_E = 1.4426950408889634
MASK_VALUE = -1e30
RG_C = 8.0
RG_CONV = 4
CV_WIDTH = 31
CV_HALO = 32
FFN_COLS = 512
IN_PROJ_COLS = 1408
OUT_PROJ_ROWS = 256
FFN_DOWN_COLS = 512
FFN_NORM_ROWS = 256
HG_CHUNK = 64
HG_SUB = 8
LN_EPS = 1e-5

NT_DIMS = (((1,), (1,)), ((), ()))
TN_DIMS = (((0,), (0,)), ((), ()))


def _params(*sem):
    return pltpu.CompilerParams(dimension_semantics=sem, vmem_limit_bytes=VMEM_LIMIT_BYTES)


def _standardize(z):
    mu = jnp.mean(z, axis=-1, keepdims=True)
    zc = z - mu
    var = jnp.mean(zc * zc, axis=-1, keepdims=True)
    return zc * lax.rsqrt(var + LN_EPS)


def _silu(z):
    return z * jax.nn.sigmoid(z)


def _shift_rows(ext, d, halo, rows):
    if d % SUBLANES == 0:
        return ext[halo - d:halo - d + rows]
    r = d % SUBLANES
    base = d - r
    rolled = pltpu.roll(ext, r, axis=0)
    return rolled[halo - base:halo - base + rows]


def _ffn_ln_kernel(x_ref, wg_ref, wu_ref, wd_ref, g_ref, b_ref, o_ref, xb_ref, *, alpha):
    f = pl.program_id(1)
    last = pl.num_programs(1) - 1
    rows, d = o_ref.shape

    def hidden(xb):
        hg = jnp.dot(xb, wg_ref[...], preferred_element_type=F32)
        hu = jnp.dot(xb, wu_ref[...], preferred_element_type=F32)
        return (_silu(hg) * hu).astype(BF16)

    def down(h, accumulate):
        for c0 in range(0, d, FFN_DOWN_COLS):
            cs = slice(c0, min(c0 + FFN_DOWN_COLS, d))
            part = jnp.dot(h, wd_ref[:, cs], preferred_element_type=F32)
            o_ref[:, cs] = o_ref[:, cs] + part if accumulate else part

    @pl.when(f == 0)
    def _():
        xb = x_ref[...].astype(BF16)
        xb_ref[...] = xb
        down(hidden(xb), accumulate=False)

    @pl.when((f > 0) & (f < last))
    def _():
        down(hidden(xb_ref[...]), accumulate=True)

    @pl.when(f == last)
    def _():
        h = hidden(xb_ref[...])
        for r0 in range(0, rows, FFN_NORM_ROWS):
            rs = slice(r0, min(r0 + FFN_NORM_ROWS, rows))
            acc = o_ref[rs, :] + jnp.dot(h[rs, :], wd_ref[...], preferred_element_type=F32)
            z = alpha * x_ref[rs, :] + 0.5 * acc
            o_ref[rs, :] = _standardize(z) * g_ref[...] + b_ref[...]


def ffn_ln(x, wg, wu, wd, ln_g, ln_b, l, j, jn, alpha, tm, tf):
    m, d = x.shape
    dff = wg.shape[-1]
    tm = min(tm, m)
    tf = min(tf, dff)
    assert m % tm == 0 and dff % tf == 0
    grid = (m // tm, dff // tf)
    return pl.pallas_call(
        functools.partial(_ffn_ln_kernel, alpha=alpha),
        grid=grid,
        in_specs=[
            pl.BlockSpec((tm, d), lambda i, f: (i, 0)),
            pl.BlockSpec((None, None, d, tf), lambda i, f: (l, j, 0, f)),
            pl.BlockSpec((None, None, d, tf), lambda i, f: (l, j, 0, f)),
            pl.BlockSpec((None, None, tf, d), lambda i, f: (l, j, f, 0)),
            pl.BlockSpec((None, 1, d), lambda i, f: (jn, 0, 0)),
            pl.BlockSpec((None, 1, d), lambda i, f: (jn, 0, 0)),
        ],
        out_specs=pl.BlockSpec((tm, d), lambda i, f: (i, 0)),
        out_shape=jax.ShapeDtypeStruct((m, d), F32),
        scratch_shapes=[pltpu.VMEM((tm, d), BF16)],
        compiler_params=_params("parallel", "arbitrary"),
        name="ffn_ln",
    )(x, wg, wu, wd, ln_g, ln_b)


def _in_proj_kernel(x_ref, w_ref, o_ref, xb_ref):
    c = pl.program_id(1)

    @pl.when(c == 0)
    def _():
        xb = x_ref[...].astype(BF16)
        xb_ref[...] = xb
        o_ref[...] = jnp.dot(xb, w_ref[...], preferred_element_type=F32)

    @pl.when(c > 0)
    def _():
        o_ref[...] = jnp.dot(xb_ref[...], w_ref[...], preferred_element_type=F32)


def in_proj(x, w_in, l, tm, tn):
    m, d = x.shape
    n = w_in.shape[-1]
    tm = min(tm, m)
    assert m % tm == 0 and n % tn == 0
    return pl.pallas_call(
        _in_proj_kernel,
        grid=(m // tm, n // tn),
        in_specs=[
            pl.BlockSpec((tm, d), lambda i, c: (i, 0)),
            pl.BlockSpec((None, d, tn), lambda i, c: (l, 0, c)),
        ],
        out_specs=pl.BlockSpec((tm, tn), lambda i, c: (i, c)),
        out_shape=jax.ShapeDtypeStruct((m, n), F32),
        scratch_shapes=[pltpu.VMEM((tm, d), BF16)],
        compiler_params=_params("parallel", "arbitrary"),
        name="in_proj",
    )(x, w_in)


def _fold_rows_max(a):
    rows, cols = a.shape
    return jnp.max(a.reshape(rows // SUBLANES, SUBLANES, cols), axis=0)


def _moba_kernel(q_ref, k_ref, v_ref, o_ref, qat_ref, ka_ref, vat_ref, vat3_ref, s_ref, so_ref, *, nb, scale):
    j = pl.program_id(2)
    blk = MOBA_BLOCK
    grp = MOBA_GROUP
    dh = HEAD_DIM
    seq = nb * blk
    tq = MOBA_Q_BLOCKS * blk
    span = grp * blk
    nbp = -(-nb // SUBLANES) * SUBLANES

    @pl.when(j == 0)
    def _():
        k = k_ref[...]
        lane = lax.broadcasted_iota(jnp.int32, (seq, LANES), 1)
        tile = lax.broadcasted_iota(jnp.int32, (nb, blk, LANES), 0).reshape(seq, LANES)
        ka_ref[:, 0:dh] = k.astype(BF16)
        ka_ref[:, dh:2 * dh] = jnp.where(tile == lane, 1.0, 0.0).astype(BF16)
        vat = jnp.concatenate([v_ref[...].T, jnp.ones((dh, seq), F32)], axis=0).astype(BF16)
        vat_ref[...] = vat
        for n in range(nb):
            vat3_ref[n] = vat[:, n * blk:(n + 1) * blk]
        k_mean = jnp.mean(k.reshape(nb, blk, dh), axis=1)
        k_mean = jnp.concatenate([k_mean, jnp.zeros((LANES - nb, dh), F32)], axis=0)

        q = q_ref[...]
        gate = lax.dot_general(k_mean.astype(BF16), q.astype(BF16), NT_DIMS, preferred_element_type=F32)
        g = gate[0:nbp]
        blk_row = lax.broadcasted_iota(jnp.int32, (nbp, seq), 0)
        q_blk = lax.shift_right_logical(lax.broadcasted_iota(jnp.int32, (nbp, seq), 1), blk.bit_length() - 1)
        row_f = blk_row.astype(F32)
        past = blk_row < q_blk
        g = jnp.where(past, g, MASK_VALUE)
        sel = jnp.zeros((nbp, seq), F32)
        for _ in range(MOBA_TOPK):
            top = jnp.max(g, axis=0, keepdims=True)
            first = jnp.min(jnp.where(g == top, row_f, float(LANES)), axis=0, keepdims=True)
            pick = row_f == first
            sel = jnp.where(pick, 1.0, sel)
            g = jnp.where(pick, -jnp.inf, g)
        bias = jnp.where(past & (sel > 0.5), 0.0, MASK_VALUE)
        qt = (q * (scale * LOG2_E)).T
        qat = jnp.concatenate([qt, bias, jnp.zeros((dh - nbp, seq), F32)], axis=0).astype(BF16)
        for t in range(seq // tq):
            qat_ref[t] = qat[:, t * tq:(t + 1) * tq]

    qat = qat_ref[j]
    first_blk = j * MOBA_Q_BLOCKS

    row = lax.broadcasted_iota(jnp.int32, (blk, blk), 0)
    col = lax.broadcasted_iota(jnp.int32, (blk, blk), 1)
    own_max = []
    for c in range(MOBA_Q_BLOCKS):
        start = pl.multiple_of((first_blk + c) * blk, blk)
        s = jnp.dot(ka_ref[pl.ds(start, blk), 0:dh], qat[0:dh, c * blk:(c + 1) * blk],
                    preferred_element_type=F32)
        s = jnp.where(row <= col, s, MASK_VALUE)
        so_ref[c] = s
        own_max.append(_fold_rows_max(s))
    mrun0 = jnp.concatenate(own_max, axis=1)

    def attend(tile):
        width = tq // MOBA_Q_PARTS
        own_per_part = MOBA_Q_BLOCKS // MOBA_Q_PARTS
        extents = [(tile * MOBA_Q_BLOCKS + (part + 1) * own_per_part - 1) * blk for part in range(MOBA_Q_PARTS)]
        pieces = [[(k0, min(k0 + span, ext)) for k0 in range(0, ext, span)] for ext in extents]
        maxes = []
        for part in range(MOBA_Q_PARTS):
            qs = slice(part * width, (part + 1) * width)
            mrun = mrun0[:, qs]
            for g, (k0, k1) in enumerate(pieces[part]):
                s = jnp.dot(ka_ref[k0:k1, :], qat[:, qs], preferred_element_type=F32)
                s_ref[g, 0:k1 - k0, qs] = s
                mrun = jnp.maximum(mrun, _fold_rows_max(s))
            maxes.append(jnp.max(mrun, axis=0, keepdims=True))
        outs = []
        for part in range(MOBA_Q_PARTS):
            qs = slice(part * width, (part + 1) * width)
            m = maxes[part]
            cols = []
            for c in range(own_per_part):
                p = jnp.exp2((so_ref[part * own_per_part + c] - m[:, c * blk:(c + 1) * blk]).astype(BF16))
                cols.append(jnp.dot(vat3_ref[first_blk + part * own_per_part + c], p,
                                    preferred_element_type=F32))
            acc = jnp.concatenate(cols, axis=1)
            for g, (k0, k1) in enumerate(pieces[part]):
                p = jnp.exp2((s_ref[g, 0:k1 - k0, qs] - m).astype(BF16))
                acc = acc + jnp.dot(vat_ref[:, k0:k1], p, preferred_element_type=F32)
            outs.append(acc[0:dh] / acc[dh:2 * dh])
        o_ref[...] = jnp.concatenate(outs, axis=1).T.astype(o_ref.dtype)

    for tile in range(seq // tq):
        pl.when(j == tile)(functools.partial(attend, tile))


def moba(p3):
    bsz, seq, _ = p3.shape
    blk = MOBA_BLOCK
    grp = MOBA_GROUP
    tq = MOBA_Q_BLOCKS * blk
    assert seq % (blk * grp) == 0 and seq % tq == 0
    nb = seq // blk
    assert nb <= HEAD_DIM
    max_count = (nb - 1 + grp - 1) // grp
    kernel = functools.partial(_moba_kernel, nb=nb, scale=HEAD_DIM ** -0.5)
    col = lambda c: pl.BlockSpec((None, seq, HEAD_DIM), lambda b, h, j: (b, 0, c * N_HEADS + h))
    return pl.pallas_call(
        kernel,
        grid=(bsz, N_HEADS, seq // tq),
        in_specs=[col(0), col(1), col(2)],
        out_specs=pl.BlockSpec((None, tq, HEAD_DIM), lambda b, h, j: (b, j, h)),
        out_shape=jax.ShapeDtypeStruct((bsz, seq, D_GROUP), MIXER_OUT_DTYPE),
        scratch_shapes=[
            pltpu.VMEM((seq // tq, 2 * HEAD_DIM, tq), BF16),
            pltpu.VMEM((seq, 2 * HEAD_DIM), BF16),
            pltpu.VMEM((2 * HEAD_DIM, seq), BF16),
            pltpu.VMEM((nb, 2 * HEAD_DIM, blk), BF16),
            pltpu.VMEM((max_count, grp * blk, tq), F32),
            pltpu.VMEM((MOBA_Q_BLOCKS, blk, blk), F32),
        ],
        compiler_params=_params("parallel", "parallel", "arbitrary"),
        name="moba",
    )(p3, p3, p3)


def _rglru_kernel(gate_ref, x_ref, cw_ref, cb_ref, wa_ref, ba_ref, wx_ref, bx_ref, lam_ref,
                  o_ref, halo_ref, h_ref):
    t = pl.program_id(1)
    rows = x_ref.shape[0]

    @pl.when(t == 0)
    def _():
        halo_ref[...] = jnp.zeros(halo_ref.shape, F32)
        h_ref[...] = jnp.zeros(h_ref.shape, F32)

    x = x_ref[...]
    ext = jnp.concatenate([halo_ref[...], x], axis=0)
    halo_ref[...] = x[rows - SUBLANES:]
    xc = jnp.zeros_like(x) + cb_ref[...]
    for j in range(RG_CONV):
        xc = xc + cw_ref[j:j + 1, :] * _shift_rows(ext, RG_CONV - 1 - j, SUBLANES, rows)

    xcb = xc.astype(BF16)
    r_parts, i_parts = [], []
    for g in range(D_GROUP // HEAD_DIM):
        blk = xcb[:, g * HEAD_DIM:(g + 1) * HEAD_DIM]
        r_parts.append(jnp.dot(blk, wa_ref[g].astype(BF16), preferred_element_type=F32))
        i_parts.append(jnp.dot(blk, wx_ref[g].astype(BF16), preferred_element_type=F32))
    r = jax.nn.sigmoid(jnp.concatenate(r_parts, axis=-1) + ba_ref[...])
    ig = jax.nn.sigmoid(jnp.concatenate(i_parts, axis=-1) + bx_ref[...])
    neg_lam = -lam_ref[...]
    softplus = jnp.maximum(neg_lam, 0.0) + jnp.log1p(jnp.exp(-jnp.abs(neg_lam)))
    log_a = -RG_C * r * softplus
    a = jnp.exp(log_a)
    u = jnp.sqrt(jnp.maximum(1.0 - jnp.exp(2.0 * log_a), 0.0)) * (ig * xc)

    grouped = (rows // SUBLANES, SUBLANES, a.shape[1])
    in_group = lax.broadcasted_iota(jnp.int32, grouped, 1)
    big_a, big_b = a.reshape(grouped), u.reshape(grouped)
    k = 1
    while k < SUBLANES:
        a_sh = jnp.where(in_group >= k, pltpu.roll(big_a, k, axis=1), 1.0)
        b_sh = jnp.where(in_group >= k, pltpu.roll(big_b, k, axis=1), 0.0)
        big_b = big_a * b_sh + big_b
        big_a = big_a * a_sh
        k *= 2
    big_a, big_b = big_a.reshape(a.shape), big_b.reshape(a.shape)
    carry = h_ref[0:1, :]
    groups = []
    for r0 in range(0, rows, SUBLANES):
        h_grp = big_a[r0:r0 + SUBLANES] * carry + big_b[r0:r0 + SUBLANES]
        groups.append(h_grp)
        carry = h_grp[SUBLANES - 1:SUBLANES]
    h = jnp.concatenate(groups, axis=0)
    h_ref[0:1, :] = carry
    o_ref[...] = (h * jax.nn.gelu(gate_ref[...], approximate=True)).astype(o_ref.dtype)


def rglru(p3, conv_w, conv_b, w_a, b_a, w_x, b_x, lam, l, tt):
    bsz, seq, _ = p3.shape
    tt = min(tt, seq)
    assert seq % tt == 0
    c = D_GROUP
    vec = lambda: pl.BlockSpec((None, 1, c), lambda b, t: (l, 0, 0))
    return pl.pallas_call(
        _rglru_kernel,
        grid=(bsz, seq // tt),
        in_specs=[
            pl.BlockSpec((None, tt, c), lambda b, t: (b, t, 3)),
            pl.BlockSpec((None, tt, c), lambda b, t: (b, t, 4)),
            pl.BlockSpec((None, RG_CONV, c), lambda b, t: (l, 0, 0)),
            vec(),
            pl.BlockSpec((None, c // HEAD_DIM, HEAD_DIM, HEAD_DIM), lambda b, t: (l, 0, 0, 0)),
            vec(),
            pl.BlockSpec((None, c // HEAD_DIM, HEAD_DIM, HEAD_DIM), lambda b, t: (l, 0, 0, 0)),
            vec(),
            vec(),
        ],
        out_specs=pl.BlockSpec((None, tt, c), lambda b, t: (b, t, 0)),
        out_shape=jax.ShapeDtypeStruct((bsz, seq, c), MIXER_OUT_DTYPE),
        scratch_shapes=[pltpu.VMEM((SUBLANES, c), F32), pltpu.VMEM((SUBLANES, c), F32)],
        compiler_params=_params("parallel", "arbitrary"),
        name="rglru",
    )(p3, p3, conv_w, conv_b, w_a, b_a, w_x, b_x, lam)


def _conformer_units(val_ref, gate_ref, cw_ref, cb_ref, ng_ref, nb_ref, o_ref, halo_ref):
    rows = val_ref.shape[0]

    def unit(g):
        cs = slice(g * HEAD_DIM, (g + 1) * HEAD_DIM)
        glu = val_ref[:, cs] * jax.nn.sigmoid(gate_ref[:, cs])
        ext = jnp.concatenate([halo_ref[:, cs], glu], axis=0)
        halo_ref[:, cs] = glu[rows - CV_HALO:]
        rolled = [ext] + [pltpu.roll(ext, r, axis=0) for r in range(1, SUBLANES)]
        u = jnp.zeros_like(glu) + cb_ref[:, cs]
        for j in range(CV_WIDTH):
            d = CV_WIDTH - 1 - j
            r = d % SUBLANES
            start = CV_HALO - (d - r)
            u = u + cw_ref[j:j + 1, cs] * rolled[r][start:start + rows]
        o_ref[:, cs] = _silu(_standardize(u) * ng_ref[:, cs] + nb_ref[:, cs]).astype(o_ref.dtype)

    return [functools.partial(unit, g) for g in range(D_GROUP // HEAD_DIM)]


def _conformer_kernel(val_ref, gate_ref, cw_ref, cb_ref, ng_ref, nb_ref, o_ref, halo_ref):
    @pl.when(pl.program_id(1) == 0)
    def _():
        halo_ref[...] = jnp.zeros(halo_ref.shape, F32)

    for unit in _conformer_units(val_ref, gate_ref, cw_ref, cb_ref, ng_ref, nb_ref, o_ref, halo_ref):
        unit()


def conformer(p3, cv_w, cv_b, ng, nb, l, tt):
    bsz, seq, _ = p3.shape
    tt = min(tt, seq)
    assert seq % tt == 0 and tt >= CV_HALO
    c = D_GROUP
    vec = lambda: pl.BlockSpec((None, 1, c), lambda b, t: (l, 0, 0))
    return pl.pallas_call(
        _conformer_kernel,
        grid=(bsz, seq // tt),
        in_specs=[
            pl.BlockSpec((None, tt, c), lambda b, t: (b, t, 5)),
            pl.BlockSpec((None, tt, c), lambda b, t: (b, t, 6)),
            pl.BlockSpec((None, CV_WIDTH, c), lambda b, t: (l, 0, 0)),
            vec(), vec(), vec(),
        ],
        out_specs=pl.BlockSpec((None, tt, c), lambda b, t: (b, t, 0)),
        out_shape=jax.ShapeDtypeStruct((bsz, seq, c), MIXER_OUT_DTYPE),
        scratch_shapes=[pltpu.VMEM((CV_HALO, c), F32)],
        compiler_params=_params("parallel", "arbitrary"),
        name="conformer",
    )(p3, p3, cv_w, cv_b, ng, nb)


def _hgrn2_chunk(q, kk, v, lf2, state_t):
    c = HG_CHUNK
    sub = HG_SUB
    grouped = (c // SUBLANES, SUBLANES, HEAD_DIM)
    in_group = lax.broadcasted_iota(jnp.int32, grouped, 1)
    b = lf2.reshape(grouped)
    k = 1
    while k < SUBLANES:
        b = b + jnp.where(in_group >= k, pltpu.roll(b, k, axis=1), 0.0)
        k *= 2
    b = b.reshape(c, HEAD_DIM)
    groups = [b[0:SUBLANES]]
    for r0 in range(SUBLANES, c, SUBLANES):
        groups.append(b[r0:r0 + SUBLANES] + groups[-1][SUBLANES - 1:SUBLANES])
    b = jnp.concatenate(groups, axis=0)
    vb = v.astype(BF16)

    o = lax.dot_general((q * jnp.exp2(b)).astype(BF16), state_t.astype(BF16), NT_DIMS,
                        preferred_element_type=F32)

    lane_s = lax.broadcasted_iota(jnp.int32, (sub, c), 1)
    row_s = lax.broadcasted_iota(jnp.int32, (sub, c), 0)
    att_rows = []
    for blk in range(c // sub):
        lo = blk * sub
        q_i = q[lo:lo + sub]
        b_i = b[lo:lo + sub]
        k_i = kk[lo:lo + sub]
        att = jnp.zeros((sub, c), F32)
        if blk > 0:
            ref = b[lo - 1:lo]
            qs = (q_i * jnp.exp2(b_i - ref)).astype(BF16)
            ks = kk[:lo] * jnp.exp2(ref - b[:lo])
            ks = jnp.concatenate([ks, jnp.zeros((c - lo, HEAD_DIM), F32)], axis=0).astype(BF16)
            att = lax.dot_general(qs, ks, NT_DIMS, preferred_element_type=F32)
        for s in range(sub):
            decay = jnp.exp2(jnp.minimum(b_i - b_i[s:s + 1], 0.0))
            w = jnp.sum(q_i * k_i[s:s + 1] * decay, axis=-1, keepdims=True)
            att = att + jnp.where((lane_s == lo + s) & (row_s >= s), w, 0.0)
        att_rows.append(att)
    att = jnp.concatenate(att_rows, axis=0)
    o = o + jnp.dot(att.astype(BF16), vb, preferred_element_type=F32)

    b_last = b[c - 1:c]
    ks = (kk * jnp.exp2(b_last - b)).astype(BF16)
    new_state_t = state_t * jnp.exp2(b_last) + lax.dot_general(vb, ks, TN_DIMS, preferred_element_type=F32)
    return o, new_state_t


def _hgrn2_units(q_ref, f_ref, v_ref, g_ref, lbp_ref, ng_ref, o_ref, state_ref, layer):
    rows = q_ref.shape[0]
    n_chunks = rows // HG_CHUNK

    lbp = lbp_ref[...]
    e = jnp.exp(lbp - jnp.max(lbp, axis=0, keepdims=True))
    sm = e / jnp.sum(e, axis=0, keepdims=True)
    lb = jnp.sum(sm[0:layer + 1], axis=0, keepdims=True) - sm[0:1]
    states = [state_ref[h] for h in range(N_HEADS)]

    def unit(ci, h):
        cs = slice(h * HEAD_DIM, (h + 1) * HEAD_DIM)
        rs = slice(ci * HG_CHUNK, (ci + 1) * HG_CHUNK)
        lb_h = lb[:, cs]
        sig = jax.nn.sigmoid(f_ref[rs, cs])
        lf2 = jnp.log(lb_h + (1.0 - lb_h) * sig) * LOG2_E
        kk = (1.0 - lb_h) * (1.0 - sig)
        o, states[h] = _hgrn2_chunk(q_ref[rs, cs], kk, v_ref[rs, cs], lf2, states[h])
        o = o * lax.rsqrt(jnp.mean(o * o, axis=-1, keepdims=True) + LN_EPS)
        o_ref[rs, cs] = (o * ng_ref[:, cs] * _silu(g_ref[rs, cs])).astype(o_ref.dtype)
        if ci == n_chunks - 1:
            state_ref[h] = states[h]

    return [functools.partial(unit, ci, h) for ci in range(n_chunks) for h in range(N_HEADS)]


def _hgrn2_kernel(q_ref, f_ref, v_ref, g_ref, lbp_ref, ng_ref, o_ref, state_ref, *, layer):
    @pl.when(pl.program_id(1) == 0)
    def _():
        state_ref[...] = jnp.zeros(state_ref.shape, F32)

    for unit in _hgrn2_units(q_ref, f_ref, v_ref, g_ref, lbp_ref, ng_ref, o_ref, state_ref, layer):
        unit()


def hgrn2(p3, lower_bound_params, norm_g, l, tt):
    bsz, seq, _ = p3.shape
    tt = min(tt, seq)
    assert seq % tt == 0 and tt % HG_CHUNK == 0
    c = D_GROUP
    depth = lower_bound_params.shape[0]
    tok = lambda col: pl.BlockSpec((None, tt, c), lambda b, t: (b, t, col))
    return pl.pallas_call(
        functools.partial(_hgrn2_kernel, layer=l),
        grid=(bsz, seq // tt),
        in_specs=[
            tok(7), tok(8), tok(9), tok(10),
            pl.BlockSpec((depth, c), lambda b, t: (0, 0)),
            pl.BlockSpec((None, 1, c), lambda b, t: (l, 0, 0)),
        ],
        out_specs=pl.BlockSpec((None, tt, c), lambda b, t: (b, t, 0)),
        out_shape=jax.ShapeDtypeStruct((bsz, seq, c), MIXER_OUT_DTYPE),
        scratch_shapes=[pltpu.VMEM((N_HEADS, HEAD_DIM, HEAD_DIM), F32)],
        compiler_params=_params("parallel", "arbitrary"),
        name="hgrn2",
    )(p3, p3, p3, p3, lower_bound_params, norm_g)


def _out_proj_ln_kernel(x_ref, ya_ref, yb_ref, yc_ref, yd_ref, w_ref, g_ref, b_ref, o_ref, *, alpha):
    rows = x_ref.shape[0]
    step = min(rows, OUT_PROJ_ROWS)
    for r0 in range(0, rows, step):
        rs = slice(r0, r0 + step)
        y_cat = jnp.concatenate([y_ref[rs, :].astype(BF16) for y_ref in (ya_ref, yb_ref, yc_ref, yd_ref)], axis=1)
        z = alpha * x_ref[rs, :] + jnp.dot(y_cat, w_ref[...], preferred_element_type=F32)
        o_ref[rs, :] = _standardize(z) * g_ref[...] + b_ref[...]


def out_proj_ln(x, ya, yb, yc, yd, w_out, ln_g, ln_b, l, jn, alpha, tm):
    m, d = x.shape
    c = D_GROUP
    tm = min(tm, m)
    assert m % tm == 0
    mix = lambda: pl.BlockSpec((tm, c), lambda i: (i, 0))
    return pl.pallas_call(
        functools.partial(_out_proj_ln_kernel, alpha=alpha),
        grid=(m // tm,),
        in_specs=[
            pl.BlockSpec((tm, d), lambda i: (i, 0)),
            mix(), mix(), mix(), mix(),
            pl.BlockSpec((None, 4 * c, d), lambda i: (l, 0, 0), pipeline_mode=pl.Buffered(1)),
            pl.BlockSpec((None, 1, d), lambda i: (jn, 0, 0)),
            pl.BlockSpec((None, 1, d), lambda i: (jn, 0, 0)),
        ],
        out_specs=pl.BlockSpec((tm, d), lambda i: (i, 0)),
        out_shape=jax.ShapeDtypeStruct((m, d), F32),
        compiler_params=_params("parallel"),
        name="out_proj_ln",
    )(x, ya, yb, yc, yd, w_out, ln_g, ln_b)


def kernel(x, ln_g, ln_b, ffn_w_gate, ffn_w_up, ffn_w_down, w_in, w_out, rg_conv_w, rg_conv_b,
           rg_w_a, rg_b_a, rg_w_x, rg_b_x, rg_lambda, cv_w, cv_b, cv_ln_g, cv_ln_b,
           hg_lower_bounds, hg_norm_g):
    bsz, seq, d = x.shape
    depth = w_in.shape[0]
    alpha = (2 * depth) ** 0.25
    m = bsz * seq

    wg = ffn_w_gate.astype(BF16)
    wu = ffn_w_up.astype(BF16)
    wd = ffn_w_down.astype(BF16)
    w_in_b = w_in.astype(BF16)
    w_out_b = w_out.astype(BF16)
    ln_g3 = ln_g.reshape(depth * 3, 1, d)
    ln_b3 = ln_b.reshape(depth * 3, 1, d)
    row = lambda a: a.reshape(depth, 1, a.shape[-1])

    h = x.reshape(m, d)
    for l in range(depth):
        h = ffn_ln(h, wg, wu, wd, ln_g3, ln_b3, l, 0, 3 * l, alpha, tm=1024, tf=FFN_COLS)
        p3 = in_proj(h, w_in_b, l, tm=1024, tn=IN_PROJ_COLS).reshape(bsz, seq, -1)
        y_a = moba(p3)
        y_b = rglru(p3, rg_conv_w, row(rg_conv_b), rg_w_a, row(rg_b_a), rg_w_x, row(rg_b_x),
                    row(rg_lambda), l, tt=1024)
        y_c = conformer(p3, cv_w, row(cv_b), row(cv_ln_g), row(cv_ln_b), l, tt=1024)
        y_d = hgrn2(p3, hg_lower_bounds, row(hg_norm_g), l, tt=512)
        flat = lambda y: y.reshape(m, D_GROUP)
        h = out_proj_ln(h, flat(y_a), flat(y_b), flat(y_c), flat(y_d), w_out_b, ln_g3, ln_b3,
                        l, 3 * l + 1, alpha, tm=1024)
        h = ffn_ln(h, wg, wu, wd, ln_g3, ln_b3, l, 1, 3 * l + 2, alpha, tm=1024, tf=FFN_COLS)
    return h.reshape(bsz, seq, d)
```

```python
import functools

import jax
import jax.numpy as jnp
from jax import lax
from jax.experimental import pallas as pl
from jax.experimental.pallas import tpu as pltpu

F32 = jnp.float32
BF16 = jnp.bfloat16
MIXER_OUT_DTYPE = BF16

LANES = 128
SUBLANES = 8
VMEM_LIMIT_BYTES = 56 * 1024 * 1024

D_GROUP = 512
HEAD_DIM = 128
N_HEADS = D_GROUP // HEAD_DIM
MOBA_BLOCK = 256
MOBA_TOPK = 3
MOBA_GROUP = 4
MOBA_Q_BLOCKS = 4
MOBA_Q_PARTS = 2
LOG2_E = 1.4426950408889634
MASK_VALUE = -1e30
RG_C = 8.0
RG_CONV = 4
CV_WIDTH = 31
CV_HALO = 32
FFN_COLS = 512
IN_PROJ_COLS = 1408
OUT_PROJ_ROWS = 256
FFN_DOWN_COLS = 512
FFN_NORM_ROWS = 256
HG_CHUNK = 64
HG_SUB = 8
LN_EPS = 1e-5

NT_DIMS = (((1,), (1,)), ((), ()))
TN_DIMS = (((0,), (0,)), ((), ()))


def _params(*sem):
    return pltpu.CompilerParams(dimension_semantics=sem, vmem_limit_bytes=VMEM_LIMIT_BYTES)


def _standardize(z):
    mu = jnp.mean(z, axis=-1, keepdims=True)
    zc = z - mu
    var = jnp.mean(zc * zc, axis=-1, keepdims=True)
    return zc * lax.rsqrt(var + LN_EPS)


def _silu(z):
    return z * jax.nn.sigmoid(z)


def _shift_rows(ext, d, halo, rows):
    if d % SUBLANES == 0:
        return ext[halo - d:halo - d + rows]
    r = d % SUBLANES
    base = d - r
    rolled = pltpu.roll(ext, r, axis=0)
    return rolled[halo - base:halo - base + rows]


def _ffn_ln_kernel(x_ref, wg_ref, wu_ref, wd_ref, g_ref, b_ref, o_ref, xb_ref, *, alpha):
    f = pl.program_id(1)
    last = pl.num_programs(1) - 1
    rows, d = o_ref.shape

    def hidden(xb):
        hg = jnp.dot(xb, wg_ref[...], preferred_element_type=F32)
        hu = jnp.dot(xb, wu_ref[...], preferred_element_type=F32)
        return (_silu(hg) * hu).astype(BF16)

    def down(h, accumulate):
        for c0 in range(0, d, FFN_DOWN_COLS):
            cs = slice(c0, min(c0 + FFN_DOWN_COLS, d))
            part = jnp.dot(h, wd_ref[:, cs], preferred_element_type=F32)
            o_ref[:, cs] = o_ref[:, cs] + part if accumulate else part

    @pl.when(f == 0)
    def _():
        xb = x_ref[...].astype(BF16)
        xb_ref[...] = xb
        down(hidden(xb), accumulate=False)

    @pl.when((f > 0) & (f < last))
    def _():
        down(hidden(xb_ref[...]), accumulate=True)

    @pl.when(f == last)
    def _():
        h = hidden(xb_ref[...])
        for r0 in range(0, rows, FFN_NORM_ROWS):
            rs = slice(r0, min(r0 + FFN_NORM_ROWS, rows))
            acc = o_ref[rs, :] + jnp.dot(h[rs, :], wd_ref[...], preferred_element_type=F32)
            z = alpha * x_ref[rs, :] + 0.5 * acc
            o_ref[rs, :] = _standardize(z) * g_ref[...] + b_ref[...]


def ffn_ln(x, wg, wu, wd, ln_g, ln_b, l, j, jn, alpha, tm, tf):
    m, d = x.shape
    dff = wg.shape[-1]
    tm = min(tm, m)
    tf = min(tf, dff)
    assert m % tm == 0 and dff % tf == 0
    grid = (m // tm, dff // tf)
    return pl.pallas_call(
        functools.partial(_ffn_ln_kernel, alpha=alpha),
        grid=grid,
        in_specs=[
            pl.BlockSpec((tm, d), lambda i, f: (i, 0)),
            pl.BlockSpec((None, None, d, tf), lambda i, f: (l, j, 0, f)),
            pl.BlockSpec((None, None, d, tf), lambda i, f: (l, j, 0, f)),
            pl.BlockSpec((None, None, tf, d), lambda i, f: (l, j, f, 0)),
            pl.BlockSpec((None, 1, d), lambda i, f: (jn, 0, 0)),
            pl.BlockSpec((None, 1, d), lambda i, f: (jn, 0, 0)),
        ],
        out_specs=pl.BlockSpec((tm, d), lambda i, f: (i, 0)),
        out_shape=jax.ShapeDtypeStruct((m, d), F32),
        scratch_shapes=[pltpu.VMEM((tm, d), BF16)],
        compiler_params=_params("parallel", "arbitrary"),
        name="ffn_ln",
    )(x, wg, wu, wd, ln_g, ln_b)


def _in_proj_kernel(x_ref, w_ref, o_ref, xb_ref):
    c = pl.program_id(1)

    @pl.when(c == 0)
    def _():
        xb = x_ref[...].astype(BF16)
        xb_ref[...] = xb
        o_ref[...] = jnp.dot(xb, w_ref[...], preferred_element_type=F32)

    @pl.when(c > 0)
    def _():
        o_ref[...] = jnp.dot(xb_ref[...], w_ref[...], preferred_element_type=F32)


def in_proj(x, w_in, l, tm, tn):
    m, d = x.shape
    n = w_in.shape[-1]
    tm = min(tm, m)
    assert m % tm == 0 and n % tn == 0
    return pl.pallas_call(
        _in_proj_kernel,
        grid=(m // tm, n // tn),
        in_specs=[
            pl.BlockSpec((tm, d), lambda i, c: (i, 0)),
            pl.BlockSpec((None, d, tn), lambda i, c: (l, 0, c)),
        ],
        out_specs=pl.BlockSpec((tm, tn), lambda i, c: (i, c)),
        out_shape=jax.ShapeDtypeStruct((m, n), F32),
        scratch_shapes=[pltpu.VMEM((tm, d), BF16)],
        compiler_params=_params("parallel", "arbitrary"),
        name="in_proj",
    )(x, w_in)


def _fold_rows_max(a):
    rows, cols = a.shape
    return jnp.max(a.reshape(rows // SUBLANES, SUBLANES, cols), axis=0)


def _moba_kernel(q_ref, k_ref, v_ref, o_ref, qat_ref, ka_ref, vat_ref, vat3_ref, s_ref, so_ref, *, nb, scale):
    j = pl.program_id(2)
    blk = MOBA_BLOCK
    grp = MOBA_GROUP
    dh = HEAD_DIM
    seq = nb * blk
    tq = MOBA_Q_BLOCKS * blk
    span = grp * blk
    nbp = -(-nb // SUBLANES) * SUBLANES

    @pl.when(j == 0)
    def _():
        k = k_ref[...]
        lane = lax.broadcasted_iota(jnp.int32, (seq, LANES), 1)
        tile = lax.broadcasted_iota(jnp.int32, (nb, blk, LANES), 0).reshape(seq, LANES)
        ka_ref[:, 0:dh] = k.astype(BF16)
        ka_ref[:, dh:2 * dh] = jnp.where(tile == lane, 1.0, 0.0).astype(BF16)
        vat = jnp.concatenate([v_ref[...].T, jnp.ones((dh, seq), F32)], axis=0).astype(BF16)
        vat_ref[...] = vat
        for n in range(nb):
            vat3_ref[n] = vat[:, n * blk:(n + 1) * blk]
        k_mean = jnp.mean(k.reshape(nb, blk, dh), axis=1)
        k_mean = jnp.concatenate([k_mean, jnp.zeros((LANES - nb, dh), F32)], axis=0)

        q = q_ref[...]
        gate = lax.dot_general(k_mean.astype(BF16), q.astype(BF16), NT_DIMS, preferred_element_type=F32)
        g = gate[0:nbp]
        blk_row = lax.broadcasted_iota(jnp.int32, (nbp, seq), 0)
        q_blk = lax.shift_right_logical(lax.broadcasted_iota(jnp.int32, (nbp, seq), 1), blk.bit_length() - 1)
        row_f = blk_row.astype(F32)
        past = blk_row < q_blk
        g = jnp.where(past, g, MASK_VALUE)
        sel = jnp.zeros((nbp, seq), F32)
        for _ in range(MOBA_TOPK):
            top = jnp.max(g, axis=0, keepdims=True)
            first = jnp.min(jnp.where(g == top, row_f, float(LANES)), axis=0, keepdims=True)
            pick = row_f == first
            sel = jnp.where(pick, 1.0, sel)
            g = jnp.where(pick, -jnp.inf, g)
        bias = jnp.where(past & (sel > 0.5), 0.0, MASK_VALUE)
        qt = (q * (scale * LOG2_E)).T
        qat = jnp.concatenate([qt, bias, jnp.zeros((dh - nbp, seq), F32)], axis=0).astype(BF16)
        for t in range(seq // tq):
            qat_ref[t] = qat[:, t * tq:(t + 1) * tq]

    qat = qat_ref[j]
    first_blk = j * MOBA_Q_BLOCKS

    row = lax.broadcasted_iota(jnp.int32, (blk, blk), 0)
    col = lax.broadcasted_iota(jnp.int32, (blk, blk), 1)
    own_max = []
    for c in range(MOBA_Q_BLOCKS):
        start = pl.multiple_of((first_blk + c) * blk, blk)
        s = jnp.dot(ka_ref[pl.ds(start, blk), 0:dh], qat[0:dh, c * blk:(c + 1) * blk],
                    preferred_element_type=F32)
        s = jnp.where(row <= col, s, MASK_VALUE)
        so_ref[c] = s
        own_max.append(_fold_rows_max(s))
    mrun0 = jnp.concatenate(own_max, axis=1)

    def attend(tile):
        width = tq // MOBA_Q_PARTS
        own_per_part = MOBA_Q_BLOCKS // MOBA_Q_PARTS
        extents = [(tile * MOBA_Q_BLOCKS + (part + 1) * own_per_part - 1) * blk for part in range(MOBA_Q_PARTS)]
        pieces = [[(k0, min(k0 + span, ext)) for k0 in range(0, ext, span)] for ext in extents]
        maxes = []
        for part in range(MOBA_Q_PARTS):
            qs = slice(part * width, (part + 1) * width)
            mrun = mrun0[:, qs]
            for g, (k0, k1) in enumerate(pieces[part]):
                s = jnp.dot(ka_ref[k0:k1, :], qat[:, qs], preferred_element_type=F32)
                s_ref[g, 0:k1 - k0, qs] = s
                mrun = jnp.maximum(mrun, _fold_rows_max(s))
            maxes.append(jnp.max(mrun, axis=0, keepdims=True))
        outs = []
        for part in range(MOBA_Q_PARTS):
            qs = slice(part * width, (part + 1) * width)
            m = maxes[part]
            cols = []
            for c in range(own_per_part):
                p = jnp.exp2((so_ref[part * own_per_part + c] - m[:, c * blk:(c + 1) * blk]).astype(BF16))
                cols.append(jnp.dot(vat3_ref[first_blk + part * own_per_part + c], p,
                                    preferred_element_type=F32))
            acc = jnp.concatenate(cols, axis=1)
            for g, (k0, k1) in enumerate(pieces[part]):
                p = jnp.exp2((s_ref[g, 0:k1 - k0, qs] - m).astype(BF16))
                acc = acc + jnp.dot(vat_ref[:, k0:k1], p, preferred_element_type=F32)
            outs.append(acc[0:dh] / acc[dh:2 * dh])
        o_ref[...] = jnp.concatenate(outs, axis=1).T.astype(o_ref.dtype)

    for tile in range(seq // tq):
        pl.when(j == tile)(functools.partial(attend, tile))


def moba(p3):
    bsz, seq, _ = p3.shape
    blk = MOBA_BLOCK
    grp = MOBA_GROUP
    tq = MOBA_Q_BLOCKS * blk
    assert seq % (blk * grp) == 0 and seq % tq == 0
    nb = seq // blk
    assert nb <= HEAD_DIM
    max_count = (nb - 1 + grp - 1) // grp
    kernel = functools.partial(_moba_kernel, nb=nb, scale=HEAD_DIM ** -0.5)
    col = lambda c: pl.BlockSpec((None, seq, HEAD_DIM), lambda b, h, j: (b, 0, c * N_HEADS + h))
    return pl.pallas_call(
        kernel,
        grid=(bsz, N_HEADS, seq // tq),
        in_specs=[col(0), col(1), col(2)],
        out_specs=pl.BlockSpec((None, tq, HEAD_DIM), lambda b, h, j: (b, j, h)),
        out_shape=jax.ShapeDtypeStruct((bsz, seq, D_GROUP), MIXER_OUT_DTYPE),
        scratch_shapes=[
            pltpu.VMEM((seq // tq, 2 * HEAD_DIM, tq), BF16),
            pltpu.VMEM((seq, 2 * HEAD_DIM), BF16),
            pltpu.VMEM((2 * HEAD_DIM, seq), BF16),
            pltpu.VMEM((nb, 2 * HEAD_DIM, blk), BF16),
            pltpu.VMEM((max_count, grp * blk, tq), F32),
            pltpu.VMEM((MOBA_Q_BLOCKS, blk, blk), F32),
        ],
        compiler_params=_params("parallel", "parallel", "arbitrary"),
        name="moba",
    )(p3, p3, p3)


def _rglru_kernel(gate_ref, x_ref, cw_ref, cb_ref, wa_ref, ba_ref, wx_ref, bx_ref, lam_ref,
                  o_ref, halo_ref, h_ref):
    t = pl.program_id(1)
    rows = x_ref.shape[0]

    @pl.when(t == 0)
    def _():
        halo_ref[...] = jnp.zeros(halo_ref.shape, F32)
        h_ref[...] = jnp.zeros(h_ref.shape, F32)

    x = x_ref[...]
    ext = jnp.concatenate([halo_ref[...], x], axis=0)
    halo_ref[...] = x[rows - SUBLANES:]
    xc = jnp.zeros_like(x) + cb_ref[...]
    for j in range(RG_CONV):
        xc = xc + cw_ref[j:j + 1, :] * _shift_rows(ext, RG_CONV - 1 - j, SUBLANES, rows)

    xcb = xc.astype(BF16)
    r_parts, i_parts = [], []
    for g in range(D_GROUP // HEAD_DIM):
        blk = xcb[:, g * HEAD_DIM:(g + 1) * HEAD_DIM]
        r_parts.append(jnp.dot(blk, wa_ref[g].astype(BF16), preferred_element_type=F32))
        i_parts.append(jnp.dot(blk, wx_ref[g].astype(BF16), preferred_element_type=F32))
    r = jax.nn.sigmoid(jnp.concatenate(r_parts, axis=-1) + ba_ref[...])
    ig = jax.nn.sigmoid(jnp.concatenate(i_parts, axis=-1) + bx_ref[...])
    neg_lam = -lam_ref[...]
    softplus = jnp.maximum(neg_lam, 0.0) + jnp.log1p(jnp.exp(-jnp.abs(neg_lam)))
    log_a = -RG_C * r * softplus
    a = jnp.exp(log_a)
    u = jnp.sqrt(jnp.maximum(1.0 - jnp.exp(2.0 * log_a), 0.0)) * (ig * xc)

    grouped = (rows // SUBLANES, SUBLANES, a.shape[1])
    in_group = lax.broadcasted_iota(jnp.int32, grouped, 1)
    big_a, big_b = a.reshape(grouped), u.reshape(grouped)
    k = 1
    while k < SUBLANES:
        a_sh = jnp.where(in_group >= k, pltpu.roll(big_a, k, axis=1), 1.0)
        b_sh = jnp.where(in_group >= k, pltpu.roll(big_b, k, axis=1), 0.0)
        big_b = big_a * b_sh + big_b
        big_a = big_a * a_sh
        k *= 2
    big_a, big_b = big_a.reshape(a.shape), big_b.reshape(a.shape)
    carry = h_ref[0:1, :]
    groups = []
    for r0 in range(0, rows, SUBLANES):
        h_grp = big_a[r0:r0 + SUBLANES] * carry + big_b[r0:r0 + SUBLANES]
        groups.append(h_grp)
        carry = h_grp[SUBLANES - 1:SUBLANES]
    h = jnp.concatenate(groups, axis=0)
    h_ref[0:1, :] = carry
    o_ref[...] = (h * jax.nn.gelu(gate_ref[...], approximate=True)).astype(o_ref.dtype)


def rglru(p3, conv_w, conv_b, w_a, b_a, w_x, b_x, lam, l, tt):
    bsz, seq, _ = p3.shape
    tt = min(tt, seq)
    assert seq % tt == 0
    c = D_GROUP
    vec = lambda: pl.BlockSpec((None, 1, c), lambda b, t: (l, 0, 0))
    return pl.pallas_call(
        _rglru_kernel,
        grid=(bsz, seq // tt),
        in_specs=[
            pl.BlockSpec((None, tt, c), lambda b, t: (b, t, 3)),
            pl.BlockSpec((None, tt, c), lambda b, t: (b, t, 4)),
            pl.BlockSpec((None, RG_CONV, c), lambda b, t: (l, 0, 0)),
            vec(),
            pl.BlockSpec((None, c // HEAD_DIM, HEAD_DIM, HEAD_DIM), lambda b, t: (l, 0, 0, 0)),
            vec(),
            pl.BlockSpec((None, c // HEAD_DIM, HEAD_DIM, HEAD_DIM), lambda b, t: (l, 0, 0, 0)),
            vec(),
            vec(),
        ],
        out_specs=pl.BlockSpec((None, tt, c), lambda b, t: (b, t, 0)),
        out_shape=jax.ShapeDtypeStruct((bsz, seq, c), MIXER_OUT_DTYPE),
        scratch_shapes=[pltpu.VMEM((SUBLANES, c), F32), pltpu.VMEM((SUBLANES, c), F32)],
        compiler_params=_params("parallel", "arbitrary"),
        name="rglru",
    )(p3, p3, conv_w, conv_b, w_a, b_a, w_x, b_x, lam)


def _conformer_units(val_ref, gate_ref, cw_ref, cb_ref, ng_ref, nb_ref, o_ref, halo_ref):
    rows = val_ref.shape[0]

    def unit(g):
        cs = slice(g * HEAD_DIM, (g + 1) * HEAD_DIM)
        glu = val_ref[:, cs] * jax.nn.sigmoid(gate_ref[:, cs])
        ext = jnp.concatenate([halo_ref[:, cs], glu], axis=0)
        halo_ref[:, cs] = glu[rows - CV_HALO:]
        rolled = [ext] + [pltpu.roll(ext, r, axis=0) for r in range(1, SUBLANES)]
        u = jnp.zeros_like(glu) + cb_ref[:, cs]
        for j in range(CV_WIDTH):
            d = CV_WIDTH - 1 - j
            r = d % SUBLANES
            start = CV_HALO - (d - r)
            u = u + cw_ref[j:j + 1, cs] * rolled[r][start:start + rows]
        o_ref[:, cs] = _silu(_standardize(u) * ng_ref[:, cs] + nb_ref[:, cs]).astype(o_ref.dtype)

    return [functools.partial(unit, g) for g in range(D_GROUP // HEAD_DIM)]


def _conformer_kernel(val_ref, gate_ref, cw_ref, cb_ref, ng_ref, nb_ref, o_ref, halo_ref):
    @pl.when(pl.program_id(1) == 0)
    def _():
        halo_ref[...] = jnp.zeros(halo_ref.shape, F32)

    for unit in _conformer_units(val_ref, gate_ref, cw_ref, cb_ref, ng_ref, nb_ref, o_ref, halo_ref):
        unit()


def conformer(p3, cv_w, cv_b, ng, nb, l, tt):
    bsz, seq, _ = p3.shape
    tt = min(tt, seq)
    assert seq % tt == 0 and tt >= CV_HALO
    c = D_GROUP
    vec = lambda: pl.BlockSpec((None, 1, c), lambda b, t: (l, 0, 0))
    return pl.pallas_call(
        _conformer_kernel,
        grid=(bsz, seq // tt),
        in_specs=[
            pl.BlockSpec((None, tt, c), lambda b, t: (b, t, 5)),
            pl.BlockSpec((None, tt, c), lambda b, t: (b, t, 6)),
            pl.BlockSpec((None, CV_WIDTH, c), lambda b, t: (l, 0, 0)),
            vec(), vec(), vec(),
        ],
        out_specs=pl.BlockSpec((None, tt, c), lambda b, t: (b, t, 0)),
        out_shape=jax.ShapeDtypeStruct((bsz, seq, c), MIXER_OUT_DTYPE),
        scratch_shapes=[pltpu.VMEM((CV_HALO, c), F32)],
        compiler_params=_params("parallel", "arbitrary"),
        name="conformer",
    )(p3, p3, cv_w, cv_b, ng, nb)


def _hgrn2_local(q, kk, v, lf2):
    c = HG_CHUNK
    sub = HG_SUB
    grouped = (c // SUBLANES, SUBLANES, HEAD_DIM)
    in_group = lax.broadcasted_iota(jnp.int32, grouped, 1)
    b = lf2.reshape(grouped)
    k = 1
    while k < SUBLANES:
        b = b + jnp.where(in_group >= k, pltpu.roll(b, k, axis=1), 0.0)
        k *= 2
    b = b.reshape(c, HEAD_DIM)
    groups = [b[0:SUBLANES]]
    for r0 in range(SUBLANES, c, SUBLANES):
        groups.append(b[r0:r0 + SUBLANES] + groups[-1][SUBLANES - 1:SUBLANES])
    b = jnp.concatenate(groups, axis=0)
    vb = v.astype(BF16)
    qe = (q * jnp.exp2(b)).astype(BF16)

    lane_s = lax.broadcasted_iota(jnp.int32, (sub, c), 1)
    row_s = lax.broadcasted_iota(jnp.int32, (sub, c), 0)
    att_rows = []
    for blk in range(c // sub):
        lo = blk * sub
        q_i = q[lo:lo + sub]
        b_i = b[lo:lo + sub]
        k_i = kk[lo:lo + sub]
        att = jnp.zeros((sub, c), F32)
        if blk > 0:
            ref = b[lo - 1:lo]
            qs = (q_i * jnp.exp2(b_i - ref)).astype(BF16)
            ks = kk[:lo] * jnp.exp2(ref - b[:lo])
            ks = jnp.concatenate([ks, jnp.zeros((c - lo, HEAD_DIM), F32)], axis=0).astype(BF16)
            att = lax.dot_general(qs, ks, NT_DIMS, preferred_element_type=F32)
        for s in range(sub):
            decay = jnp.exp2(jnp.minimum(b_i - b_i[s:s + 1], 0.0))
            w = jnp.sum(q_i * k_i[s:s + 1] * decay, axis=-1, keepdims=True)
            att = att + jnp.where((lane_s == lo + s) & (row_s >= s), w, 0.0)
        att_rows.append(att)
    att = jnp.concatenate(att_rows, axis=0)
    o_intra = jnp.dot(att.astype(BF16), vb, preferred_element_type=F32)

    b_last = b[c - 1:c]
    ks_last = (kk * jnp.exp2(b_last - b)).astype(BF16)
    return qe, o_intra, vb, ks_last, jnp.exp2(b_last)


def _hgrn2_carry(local, state_t):
    qe, o_intra, vb, ks_last, decay_last = local
    o = o_intra + lax.dot_general(qe, state_t.astype(BF16), NT_DIMS, preferred_element_type=F32)
    new_state_t = state_t * decay_last + lax.dot_general(vb, ks_last, TN_DIMS, preferred_element_type=F32)
    return o, new_state_t


def _hgrn2_units(q_ref, f_ref, v_ref, g_ref, lbp_ref, ng_ref, o_ref, state_ref, layer):
    rows = q_ref.shape[0]
    n_chunks = rows // HG_CHUNK

    lbp = lbp_ref[...]
    e = jnp.exp(lbp - jnp.max(lbp, axis=0, keepdims=True))
    sm = e / jnp.sum(e, axis=0, keepdims=True)
    lb = jnp.sum(sm[0:layer + 1], axis=0, keepdims=True) - sm[0:1]
    states = [state_ref[h] for h in range(N_HEADS)]
    local = {}

    def free_part(ci, h):
        cs = slice(h * HEAD_DIM, (h + 1) * HEAD_DIM)
        rs = slice(ci * HG_CHUNK, (ci + 1) * HG_CHUNK)
        lb_h = lb[:, cs]
        sig = jax.nn.sigmoid(f_ref[rs, cs])
        lf2 = jnp.log(lb_h + (1.0 - lb_h) * sig) * LOG2_E
        kk = (1.0 - lb_h) * (1.0 - sig)
        local[ci, h] = _hgrn2_local(q_ref[rs, cs], kk, v_ref[rs, cs], lf2)

    def carried_part(ci, h):
        cs = slice(h * HEAD_DIM, (h + 1) * HEAD_DIM)
        rs = slice(ci * HG_CHUNK, (ci + 1) * HG_CHUNK)
        o, states[h] = _hgrn2_carry(local.pop((ci, h)), states[h])
        o = o * lax.rsqrt(jnp.mean(o * o, axis=-1, keepdims=True) + LN_EPS)
        o_ref[rs, cs] = (o * ng_ref[:, cs] * _silu(g_ref[rs, cs])).astype(o_ref.dtype)
        if ci == n_chunks - 1:
            state_ref[h] = states[h]

    units = [functools.partial(free_part, 0, h) for h in range(N_HEADS)]
    for ci in range(n_chunks):
        for h in range(N_HEADS):
            units.append(functools.partial(carried_part, ci, h))
            if ci + 1 < n_chunks:
                units.append(functools.partial(free_part, ci + 1, h))
    return units


def _hgrn2_kernel(q_ref, f_ref, v_ref, g_ref, lbp_ref, ng_ref, o_ref, state_ref, *, layer):
    @pl.when(pl.program_id(1) == 0)
    def _():
        state_ref[...] = jnp.zeros(state_ref.shape, F32)

    for unit in _hgrn2_units(q_ref, f_ref, v_ref, g_ref, lbp_ref, ng_ref, o_ref, state_ref, layer):
        unit()


def hgrn2(p3, lower_bound_params, norm_g, l, tt):
    bsz, seq, _ = p3.shape
    tt = min(tt, seq)
    assert seq % tt == 0 and tt % HG_CHUNK == 0
    c = D_GROUP
    depth = lower_bound_params.shape[0]
    tok = lambda col: pl.BlockSpec((None, tt, c), lambda b, t: (b, t, col))
    return pl.pallas_call(
        functools.partial(_hgrn2_kernel, layer=l),
        grid=(bsz, seq // tt),
        in_specs=[
            tok(7), tok(8), tok(9), tok(10),
            pl.BlockSpec((depth, c), lambda b, t: (0, 0)),
            pl.BlockSpec((None, 1, c), lambda b, t: (l, 0, 0)),
        ],
        out_specs=pl.BlockSpec((None, tt, c), lambda b, t: (b, t, 0)),
        out_shape=jax.ShapeDtypeStruct((bsz, seq, c), MIXER_OUT_DTYPE),
        scratch_shapes=[pltpu.VMEM((N_HEADS, HEAD_DIM, HEAD_DIM), F32)],
        compiler_params=_params("parallel", "arbitrary"),
        name="hgrn2",
    )(p3, p3, p3, p3, lower_bound_params, norm_g)


def _out_proj_ln_kernel(x_ref, ya_ref, yb_ref, yc_ref, yd_ref, w_ref, g_ref, b_ref, o_ref, *, alpha):
    rows = x_ref.shape[0]
    step = min(rows, OUT_PROJ_ROWS)
    for r0 in range(0, rows, step):
        rs = slice(r0, r0 + step)
        y_cat = jnp.concatenate([y_ref[rs, :].astype(BF16) for y_ref in (ya_ref, yb_ref, yc_ref, yd_ref)], axis=1)
        z = alpha * x_ref[rs, :] + jnp.dot(y_cat, w_ref[...], preferred_element_type=F32)
        o_ref[rs, :] = _standardize(z) * g_ref[...] + b_ref[...]


def out_proj_ln(x, ya, yb, yc, yd, w_out, ln_g, ln_b, l, jn, alpha, tm):
    m, d = x.shape
    c = D_GROUP
    tm = min(tm, m)
    assert m % tm == 0
    mix = lambda: pl.BlockSpec((tm, c), lambda i: (i, 0))
    return pl.pallas_call(
        functools.partial(_out_proj_ln_kernel, alpha=alpha),
        grid=(m // tm,),
        in_specs=[
            pl.BlockSpec((tm, d), lambda i: (i, 0)),
            mix(), mix(), mix(), mix(),
            pl.BlockSpec((None, 4 * c, d), lambda i: (l, 0, 0), pipeline_mode=pl.Buffered(1)),
            pl.BlockSpec((None, 1, d), lambda i: (jn, 0, 0)),
            pl.BlockSpec((None, 1, d), lambda i: (jn, 0, 0)),
        ],
        out_specs=pl.BlockSpec((tm, d), lambda i: (i, 0)),
        out_shape=jax.ShapeDtypeStruct((m, d), F32),
        compiler_params=_params("parallel"),
        name="out_proj_ln",
    )(x, ya, yb, yc, yd, w_out, ln_g, ln_b)


def kernel(x, ln_g, ln_b, ffn_w_gate, ffn_w_up, ffn_w_down, w_in, w_out, rg_conv_w, rg_conv_b,
           rg_w_a, rg_b_a, rg_w_x, rg_b_x, rg_lambda, cv_w, cv_b, cv_ln_g, cv_ln_b,
           hg_lower_bounds, hg_norm_g):
    bsz, seq, d = x.shape
    depth = w_in.shape[0]
    alpha = (2 * depth) ** 0.25
    m = bsz * seq

    wg = ffn_w_gate.astype(BF16)
    wu = ffn_w_up.astype(BF16)
    wd = ffn_w_down.astype(BF16)
    w_in_b = w_in.astype(BF16)
    w_out_b = w_out.astype(BF16)
    ln_g3 = ln_g.reshape(depth * 3, 1, d)
    ln_b3 = ln_b.reshape(depth * 3, 1, d)
    row = lambda a: a.reshape(depth, 1, a.shape[-1])

    h = x.reshape(m, d)
    for l in range(depth):
        h = ffn_ln(h, wg, wu, wd, ln_g3, ln_b3, l, 0, 3 * l, alpha, tm=1024, tf=FFN_COLS)
        p3 = in_proj(h, w_in_b, l, tm=1024, tn=IN_PROJ_COLS).reshape(bsz, seq, -1)
        y_a = moba(p3)
        y_b = rglru(p3, rg_conv_w, row(rg_conv_b), rg_w_a, row(rg_b_a), rg_w_x, row(rg_b_x),
                    row(rg_lambda), l, tt=1024)
        y_c = conformer(p3, cv_w, row(cv_b), row(cv_ln_g), row(cv_ln_b), l, tt=1024)
        y_d = hgrn2(p3, hg_lower_bounds, row(hg_norm_g), l, tt=512)
        flat = lambda y: y.reshape(m, D_GROUP)
        h = out_proj_ln(h, flat(y_a), flat(y_b), flat(y_c), flat(y_d), w_out_b, ln_g3, ln_b3,
                        l, 3 * l + 1, alpha, tm=1024)
        h = ffn_ln(h, wg, wu, wd, ln_g3, ln_b3, l, 1, 3 * l + 2, alpha, tm=1024, tf=FFN_COLS)
    return h.reshape(bsz, seq, d)
```
